```python
import math
import jax, jax.numpy as jnp
from jax import lax
import numpy as np

D_MODEL = 1024
BATCH = 32
SEQ = 2048
DEPTH = 2
DEC_BATCH = 128
DEC_SEQ = 1
PAST_LEN = 16384
PAGE_SIZE = 128

N_MIXERS = 2
N_ATTN_LAYERS = (DEPTH + 1) // 2
N_RWKV_LAYERS = DEPTH // 2
MLA_HEADS = 16
QK_NOPE = 64
QK_ROPE = 32
V_HEAD = 64
Q_LORA = 512
KV_LORA = 256
ROPE_THETA = 10000.0
Q_BLOCK = 128
MLA_SCALE = 1.0 / math.sqrt(QK_NOPE + QK_ROPE)
RWKV_HEAD = 64
RWKV_HEADS = D_MODEL // RWKV_HEAD
DECAY_LORA = 64
AAA_LORA = 64
GATE_LORA = 160
GN_EPS = 64e-5
FFN_HIDDEN = -(-(8 * D_MODEL) // (3 * 256)) * 256
NORM_EPS = 1e-6

kernel_name = 'mla_rwkv7_hybrid_step'


def rms_norm(x, g):
    xf = x.astype(jnp.float32)
    y = xf * lax.rsqrt(jnp.mean(xf * xf, axis=-1, keepdims=True) + NORM_EPS)
    return (y * g.astype(jnp.float32)).astype(x.dtype)


def rope_angles(pos):
    inv = 1.0 / (ROPE_THETA ** (jnp.arange(0, QK_ROPE, 2, dtype=jnp.float32) / QK_ROPE))
    ang = pos[:, None] * inv[None, :]
    return jnp.cos(ang), jnp.sin(ang)


def apply_rope(x, cos, sin):
    xf = x.astype(jnp.float32)
    x1, x2 = jnp.split(xf, 2, axis=-1)
    return jnp.concatenate([x1 * cos - x2 * sin, x2 * cos + x1 * sin], axis=-1).astype(x.dtype)


def mla_project(h, pos, w_a, q_norm, kv_norm, w_uq):
    B, T, _ = h.shape
    a = h @ w_a
    c_q = rms_norm(a[..., :Q_LORA], q_norm)
    c_kv = rms_norm(a[..., Q_LORA:Q_LORA + KV_LORA], kv_norm)
    k_pe = a[..., Q_LORA + KV_LORA:]
    q = (c_q @ w_uq).reshape(B, T, MLA_HEADS, QK_NOPE + QK_ROPE)
    cos, sin = rope_angles(pos)
    q_nope = q[..., :QK_NOPE]
    q_pe = apply_rope(q[..., QK_NOPE:], cos[None, :, None, :], sin[None, :, None, :])
    k_pe = apply_rope(k_pe, cos[None], sin[None])
    return q_nope, q_pe, c_kv, k_pe


def mla_attend_prompt(q_nope, q_pe, c_kv, k_pe, w_ukv):
    B, S = q_nope.shape[:2]
    kv = jnp.einsum('bsl,lhe->bshe', c_kv, w_ukv)
    k_nope, v = kv[..., :QK_NOPE], kv[..., QK_NOPE:]
    kpos = jnp.arange(S)

    def block(i):
        start = i * Q_BLOCK
        qn = lax.dynamic_slice_in_dim(q_nope, start, Q_BLOCK, axis=1)
        qp = lax.dynamic_slice_in_dim(q_pe, start, Q_BLOCK, axis=1)
        s = (jnp.einsum('bqhd,bkhd->bhqk', qn, k_nope)
             + jnp.einsum('bqhr,bkr->bhqk', qp, k_pe)).astype(jnp.float32) * MLA_SCALE
        qpos = start + jnp.arange(Q_BLOCK)
        s = jnp.where(kpos[None, :] <= qpos[:, None], s, -jnp.inf)
        p = jax.nn.softmax(s, axis=-1).astype(v.dtype)
        return jnp.einsum('bhqk,bkhv->bqhv', p, v)

    o = lax.map(block, jnp.arange(S // Q_BLOCK))
    return jnp.moveaxis(o, 0, 1).reshape(B, S, MLA_HEADS * V_HEAD)


def mla_attend_cached(q_nope, q_pe, c_kv, k_pe, ckv_pool, kpe_pool, page_table, w_ukv):
    B, T = q_nope.shape[:2]
    past = page_table.shape[1] * PAGE_SIZE
    ckv_all = jnp.concatenate([ckv_pool[page_table].reshape(B, past, KV_LORA).astype(c_kv.dtype), c_kv], axis=1)
    kpe_all = jnp.concatenate([kpe_pool[page_table].reshape(B, past, QK_ROPE).astype(k_pe.dtype), k_pe], axis=1)
    w_uk, w_uv = w_ukv[..., :QK_NOPE], w_ukv[..., QK_NOPE:]
    q_lat = jnp.einsum('bqhd,lhd->bqhl', q_nope, w_uk)
    s = (jnp.einsum('bqhl,bkl->bhqk', q_lat, ckv_all)
         + jnp.einsum('bqhr,bkr->bhqk', q_pe, kpe_all)).astype(jnp.float32) * MLA_SCALE
    kpos = jnp.arange(past + T)
    qpos = past + jnp.arange(T)
    s = jnp.where(kpos[None, :] <= qpos[:, None], s, -jnp.inf)
    p = jax.nn.softmax(s, axis=-1).astype(ckv_all.dtype)
    o_lat = jnp.einsum('bhqk,bkl->bqhl', p, ckv_all)
    o = jnp.einsum('bqhl,lhv->bqhv', o_lat, w_uv)
    return o.reshape(B, T, MLA_HEADS * V_HEAD)


def wkv_step(S, inp):
    r, w, k, v, a, b = inp
    sa = jnp.einsum('bhvk,bhk->bhv', S, a)
    S = S * w[:, :, None, :] + sa[..., None] * b[:, :, None, :] + v[..., None] * k[:, :, None, :]
    return S, jnp.einsum('bhvk,bhk->bhv', S, r)


def rwkv7_mix(h, shift_prev, s0, mix, w_rkv, w0, w1, w2, a0, a1, a2, g1, g2, k_k, k_a, r_k, ln_w, ln_b, w_o):
    f32 = jnp.float32
    B, T, D = h.shape
    H, N = RWKV_HEADS, RWKV_HEAD
    h_prev = jnp.concatenate([shift_prev[:, None, :].astype(h.dtype), h[:, :-1]], axis=1)
    xx = h_prev - h
    xr, xw, xk, xv, xa, xg = (h + xx * mix[n] for n in range(6))
    r = xr @ w_rkv[0]
    k = xk @ w_rkv[1]
    v = xv @ w_rkv[2]
    w_log = -jax.nn.softplus(-(w0 + jnp.tanh(xw @ w1) @ w2).astype(f32)) - 0.5
    decay = jnp.exp(-jnp.exp(w_log))
    a = jax.nn.sigmoid((a0 + (xa @ a1) @ a2).astype(f32))
    g = jax.nn.sigmoid(xg @ g1) @ g2
    kk = (k * k_k).astype(f32).reshape(B, T, H, N)
    kk = kk / jnp.maximum(jnp.sqrt(jnp.sum(kk * kk, axis=-1, keepdims=True)), 1e-12)
    k = k.astype(f32) * (1.0 + (a - 1.0) * k_a.astype(f32))
    heads = lambda t: t.astype(f32).reshape(B, T, H, N)
    r_h, k_h, v_h, w_h, a_h = heads(r), heads(k), heads(v), heads(decay), heads(a)
    seq = tuple(jnp.moveaxis(t, 1, 0) for t in (r_h, w_h, k_h, v_h, -kk, kk * a_h))
    s_fin, y = lax.scan(wkv_step, s0.astype(f32), seq)
    y = jnp.moveaxis(y, 0, 1)
    mu = jnp.mean(y, axis=-1, keepdims=True)
    var = jnp.mean(jnp.square(y - mu), axis=-1, keepdims=True)
    y = ((y - mu) * lax.rsqrt(var + GN_EPS)).reshape(B, T, D) * ln_w.astype(f32) + ln_b.astype(f32)
    bonus = jnp.sum(r_h * k_h * r_k.astype(f32), axis=-1, keepdims=True) * v_h
    o = ((y + bonus.reshape(B, T, D)) * g.astype(f32)).astype(h.dtype) @ w_o
    return o, h[:, -1], s_fin.astype(h.dtype)


def swiglu(h, w_in, w_out):
    gu = h @ w_in
    return (jax.nn.silu(gu[..., :FFN_HIDDEN]) * gu[..., FFN_HIDDEN:]) @ w_out


def setup_inputs(seed: int = 0) -> dict:
    key = jax.random.key(seed)
    ks = jax.random.split(key, 40)
    f32 = jnp.float32
    nrm = lambda i, shape, scale: jax.random.normal(ks[i], shape, f32) * scale
    uni = lambda i, shape, lo, hi: jax.random.uniform(ks[i], shape, f32, lo, hi)
    n_pages = PAST_LEN // PAGE_SIZE
    n_used = DEC_BATCH * n_pages
    n_pool = n_used + n_used // 4
    A, R, L = N_ATTN_LAYERS, N_RWKV_LAYERS, DEPTH
    H, N, D = RWKV_HEADS, RWKV_HEAD, D_MODEL
    page_table = jax.random.permutation(ks[6], n_pool)[:n_used].reshape(DEC_BATCH, n_pages).astype(jnp.int32)
    return {
        'x_prompt': nrm(0, (BATCH, SEQ, D), 1.0),
        'x_sample': nrm(1, (DEC_BATCH, DEC_SEQ, D), 1.0),
        'cache_ckv': nrm(2, (A, n_pool, PAGE_SIZE, KV_LORA), 1.0),
        'cache_kpe': nrm(3, (A, n_pool, PAGE_SIZE, QK_ROPE), 1.0),
        'state_wkv': nrm(4, (R, DEC_BATCH, H, N, N), 0.3),
        'state_shift': nrm(5, (R, DEC_BATCH, D), 1.0),
        'page_table': page_table,
        'norm_mix': 1.0 + nrm(7, (L, D), 0.02),
        'norm_ffn': 1.0 + nrm(8, (L, D), 0.02),
        'norm_final': 1.0 + nrm(9, (D,), 0.02),
        'mla_w_a': nrm(10, (A, D, Q_LORA + KV_LORA + QK_ROPE), D ** -0.5),
        'mla_q_norm': 1.0 + nrm(11, (A, Q_LORA), 0.02),
        'mla_kv_norm': 1.0 + nrm(12, (A, KV_LORA), 0.02),
        'mla_w_uq': nrm(13, (A, Q_LORA, MLA_HEADS * (QK_NOPE + QK_ROPE)), Q_LORA ** -0.5),
        'mla_w_ukv': nrm(14, (A, KV_LORA, MLA_HEADS, QK_NOPE + V_HEAD), KV_LORA ** -0.5),
        'mla_w_o': nrm(15, (A, MLA_HEADS * V_HEAD, D), (MLA_HEADS * V_HEAD) ** -0.5),
        'rw_mix': uni(16, (R, 6, D), 0.0, 1.0),
        'rw_w_rkv': nrm(17, (R, 3, D, D), D ** -0.5),
        'rw_w0': uni(18, (R, D), -6.0, -1.0),
        'rw_w1': nrm(19, (R, D, DECAY_LORA), D ** -0.5),
        'rw_w2': nrm(20, (R, DECAY_LORA, D), 0.1 * DECAY_LORA ** -0.5),
        'rw_a0': nrm(21, (R, D), 0.5),
        'rw_a1': nrm(22, (R, D, AAA_LORA), D ** -0.5),
        'rw_a2': nrm(23, (R, AAA_LORA, D), 0.5 * AAA_LORA ** -0.5),
        'rw_g1': nrm(24, (R, D, GATE_LORA), D ** -0.5),
        'rw_g2': nrm(25, (R, GATE_LORA, D), GATE_LORA ** -0.5),
        'rw_k_k': 0.85 + nrm(26, (R, D), 0.05),
        'rw_k_a': 1.0 + nrm(27, (R, D), 0.05),
        'rw_r_k': nrm(28, (R, H, N), 0.1),
        'rw_ln_w': 1.0 + nrm(29, (R, D), 0.02),
        'rw_ln_b': nrm(30, (R, D), 0.02),
        'rw_w_o': nrm(31, (R, D, D), D ** -0.5),
        'ffn_w_in': nrm(32, (L, D, 2 * FFN_HIDDEN), D ** -0.5),
        'ffn_w_out': nrm(33, (L, FFN_HIDDEN, D), FFN_HIDDEN ** -0.5),
    }


def reference(x_prompt, x_sample, cache_ckv, cache_kpe, state_wkv, state_shift, page_table,
              norm_mix, norm_ffn, norm_final,
              mla_w_a, mla_q_norm, mla_kv_norm, mla_w_uq, mla_w_ukv, mla_w_o,
              rw_mix, rw_w_rkv, rw_w0, rw_w1, rw_w2, rw_a0, rw_a1, rw_a2, rw_g1, rw_g2,
              rw_k_k, rw_k_a, rw_r_k, rw_ln_w, rw_ln_b, rw_w_o,
              ffn_w_in, ffn_w_out):
    f32 = jnp.float32
    B, S, D = x_prompt.shape
    DB, DS, _ = x_sample.shape
    past_len = page_table.shape[1] * PAGE_SIZE
    pos_p = jnp.arange(S, dtype=f32)
    pos_s = past_len + jnp.arange(DS, dtype=f32)
    xp, xs = x_prompt, x_sample
    ckv_p, kpe_p, ckv_s, kpe_s = [], [], [], []
    wkv_p, shf_p, wkv_s, shf_s = [], [], [], []
    for i in range(DEPTH):
        j = i // N_MIXERS
        hp = rms_norm(xp, norm_mix[i])
        hs = rms_norm(xs, norm_mix[i])
        if i % N_MIXERS == 0:
            qn, qp, c, kp = mla_project(hp, pos_p, mla_w_a[j], mla_q_norm[j], mla_kv_norm[j], mla_w_uq[j])
            mp = mla_attend_prompt(qn, qp, c, kp, mla_w_ukv[j]) @ mla_w_o[j]
            ckv_p.append(c)
            kpe_p.append(kp)
            qn, qp, c, kp = mla_project(hs, pos_s, mla_w_a[j], mla_q_norm[j], mla_kv_norm[j], mla_w_uq[j])
            ms = mla_attend_cached(qn, qp, c, kp, cache_ckv[j], cache_kpe[j], page_table, mla_w_ukv[j]) @ mla_w_o[j]
            ckv_s.append(c)
            kpe_s.append(kp)
        else:
            rw = (rw_mix[j], rw_w_rkv[j], rw_w0[j], rw_w1[j], rw_w2[j], rw_a0[j], rw_a1[j], rw_a2[j],
                  rw_g1[j], rw_g2[j], rw_k_k[j], rw_k_a[j], rw_r_k[j], rw_ln_w[j], rw_ln_b[j], rw_w_o[j])
            zero_shift = jnp.zeros((B, D), xp.dtype)
            zero_state = jnp.zeros((B, RWKV_HEADS, RWKV_HEAD, RWKV_HEAD), f32)
            mp, sh, st = rwkv7_mix(hp, zero_shift, zero_state, *rw)
            wkv_p.append(st)
            shf_p.append(sh)
            ms, sh, st = rwkv7_mix(hs, state_shift[j], state_wkv[j], *rw)
            wkv_s.append(st)
            shf_s.append(sh)
        xp = xp + mp
        xs = xs + ms
        xp = xp + swiglu(rms_norm(xp, norm_ffn[i]), ffn_w_in[i], ffn_w_out[i])
        xs = xs + swiglu(rms_norm(xs, norm_ffn[i]), ffn_w_in[i], ffn_w_out[i])
    y_prompt = rms_norm(xp, norm_final)
    y_sample = rms_norm(xs, norm_final)
    return (y_prompt, y_sample,
            jnp.stack(ckv_p), jnp.stack(kpe_p), jnp.stack(ckv_s), jnp.stack(kpe_s),
            jnp.stack(wkv_p), jnp.stack(shf_p), jnp.stack(wkv_s), jnp.stack(shf_s))
```

```python
import functools
import math

import jax
import jax.numpy as jnp
from jax import lax
from jax.experimental import pallas as pl
from jax.experimental.pallas import tpu as pltpu

f32 = jnp.float32
bf16 = jnp.bfloat16

D_MODEL = 1024
MLA_HEADS = 16
QK_NOPE = 64
QK_ROPE = 32
V_HEAD = 64
Q_LORA = 512
KV_LORA = 256
ROPE_THETA = 10000.0
MLA_SCALE = 1.0 / math.sqrt(QK_NOPE + QK_ROPE)
RWKV_HEAD = 64
RWKV_HEADS = D_MODEL // RWKV_HEAD
FFN_HIDDEN = 2816
NORM_EPS = 1e-6
GN_EPS = 64e-5
PAGE_SIZE = 128

LANES = 128
HEAD_TILE = 128
WKV_CHUNK = 64
WKV_GROUP = 4
GROUP_W = WKV_GROUP * RWKV_HEAD
VMEM_LIMIT = 56 * 1024 * 1024


def _cparams(sem):
    return pltpu.CompilerParams(dimension_semantics=sem, vmem_limit_bytes=VMEM_LIMIT)


def _rms(x, g):
    return x * lax.rsqrt(jnp.mean(x * x, axis=-1, keepdims=True) + NORM_EPS) * g


def _dot(a, b):
    return jnp.dot(a, b, preferred_element_type=f32)


def _dot_nt(a, b):
    return lax.dot_general(a, b, (((1,), (1,)), ((), ())), preferred_element_type=f32)


def _dot_tn(a, b):
    return lax.dot_general(a, b, (((0,), (0,)), ((), ())), preferred_element_type=f32)


def _mla_proj_kernel(x_ref, gmix_ref, wa_ref, qn_ref, kvn_ref, wuq_ref, wukv_ref,
                     c_ref, s1_ref, s2_ref, q_ref, k_ref, kv_ref, ckv_ref, kpe_ref):
    tm = x_ref.shape[0]
    h = _rms(x_ref[...], gmix_ref[...]).astype(bf16)
    a = _dot(h, wa_ref[...])
    cq = _rms(a[:, :Q_LORA], qn_ref[...]).astype(bf16)
    ckv = _rms(a[:, Q_LORA:Q_LORA + KV_LORA], kvn_ref[...])
    ckv_ref[...] = ckv
    cos = c_ref[...]
    sin_lo = s1_ref[...]
    sin_hi = s2_ref[...]

    def rope(t):
        return t * cos + pltpu.roll(t, LANES - QK_ROPE // 2, 1) * sin_lo + pltpu.roll(t, QK_ROPE // 2, 1) * sin_hi

    kpe_t = rope(a[:, Q_LORA + KV_LORA:])
    kpe_ref[...] = kpe_t[:, QK_NOPE:QK_NOPE + QK_ROPE]
    kv = _dot(ckv.astype(bf16), wukv_ref[...])
    kv_ref[...] = kv.astype(bf16)
    q = _dot(cq, wuq_ref[...])
    lane = lax.broadcasted_iota(jnp.int32, (tm, HEAD_TILE), 1)
    for hh in range(MLA_HEADS):
        sl = slice(hh * HEAD_TILE, (hh + 1) * HEAD_TILE)
        q_ref[:, sl] = rope(q[:, sl]).astype(bf16)
        k_ref[:, sl] = jnp.where(lane < QK_NOPE, kv[:, sl], kpe_t).astype(bf16)


def _mla_proj(x, gmix, wa, qn, kvn, wuq, wukv, tabs, tm):
    T = x.shape[0]
    tab_c, tab_s1, tab_s2 = tabs
    nt = tab_c.shape[0] // tm
    HW = MLA_HEADS * HEAD_TILE
    const = lambda i: (0, 0)
    row = lambda i: (i, 0)
    tabmap = lambda i: (i % nt, 0)
    return pl.pallas_call(
        _mla_proj_kernel,
        grid=(T // tm,),
        in_specs=[
            pl.BlockSpec((tm, D_MODEL), row),
            pl.BlockSpec((1, D_MODEL), const),
            pl.BlockSpec(wa.shape, const),
            pl.BlockSpec((1, Q_LORA), const),
            pl.BlockSpec((1, KV_LORA), const),
            pl.BlockSpec(wuq.shape, const),
            pl.BlockSpec(wukv.shape, const),
            pl.BlockSpec((tm, HEAD_TILE), tabmap),
            pl.BlockSpec((tm, HEAD_TILE), tabmap),
            pl.BlockSpec((tm, HEAD_TILE), tabmap),
        ],
        out_specs=[
            pl.BlockSpec((tm, HW), row),
            pl.BlockSpec((tm, HW), row),
            pl.BlockSpec((tm, HW), row),
            pl.BlockSpec((tm, KV_LORA), row),
            pl.BlockSpec((tm, QK_ROPE), row),
        ],
        out_shape=[
            jax.ShapeDtypeStruct((T, HW), bf16),
            jax.ShapeDtypeStruct((T, HW), bf16),
            jax.ShapeDtypeStruct((T, HW), bf16),
            jax.ShapeDtypeStruct((T, KV_LORA), f32),
            jax.ShapeDtypeStruct((T, QK_ROPE), f32),
        ],
        compiler_params=_cparams(("parallel",)),
        name="mla_proj",
    )(x, gmix, wa, qn, kvn, wuq, wukv, tab_c, tab_s1, tab_s2)


def _flash_kernel(q_ref, k_ref, kv_ref, o_ref, m_ref, l_ref, acc_ref, *, tq, tk):
    S = q_ref.shape[0]
    nq = S // tq
    row = lax.broadcasted_iota(jnp.int32, (tq, tk), 0)
    col = lax.broadcasted_iota(jnp.int32, (tq, tk), 1)
    lane = lax.broadcasted_iota(jnp.int32, (tq, HEAD_TILE), 1)

    def one_head(hh, q0, qi):
        hs = slice(hh * HEAD_TILE, (hh + 1) * HEAD_TILE)
        q = q_ref[pl.ds(q0, tq), hs]
        m_ref[...] = jnp.full((tq, 1), -jnp.inf, f32)
        l_ref[...] = jnp.zeros((tq, 1), f32)
        acc_ref[...] = jnp.zeros((tq, HEAD_TILE), f32)

        def block(k0, diag_off):
            kb = k_ref[pl.ds(k0, tk), hs]
            vb = kv_ref[pl.ds(k0, tk), hs]
            s = _dot_nt(q, kb)
            if diag_off is not None:
                s = jnp.where(col + diag_off <= row, s, -jnp.inf)
            m_prev = m_ref[...]
            m_new = jnp.maximum(m_prev, jnp.max(s, axis=-1, keepdims=True))
            p = jnp.exp(s - m_new)
            alpha = jnp.exp(m_prev - m_new)
            l_ref[...] = alpha * l_ref[...] + jnp.sum(p, axis=-1, keepdims=True)
            acc_ref[...] = alpha * acc_ref[...] + _dot(p.astype(bf16), vb)
            m_ref[...] = m_new

        def k_body(ki, c):
            block(pl.multiple_of(ki * tk, tk), None)
            return c

        lax.fori_loop(0, qi * (tq // tk), k_body, 0)
        for d in range(tq // tk):
            block(pl.multiple_of(q0 + d * tk, tk), d * tk)
        return acc_ref[...] / l_ref[...]

    def q_body(qi, c):
        q0 = pl.multiple_of(qi * tq, tq)
        o0 = one_head(0, q0, qi)
        o1 = one_head(1, q0, qi)
        o = jnp.where(lane < V_HEAD, pltpu.roll(o0, V_HEAD, 1), o1)
        o_ref[pl.ds(q0, tq), :] = o.astype(bf16)
        return c

    lax.fori_loop(0, nq, q_body, 0)


def _flash_attention(q, k, kv, B, S, tq=256, tk=256):
    T = B * S
    blk = pl.BlockSpec((S, 2 * HEAD_TILE), lambda b, hp: (b, hp))
    return pl.pallas_call(
        functools.partial(_flash_kernel, tq=tq, tk=tk),
        grid=(B, MLA_HEADS // 2),
        in_specs=[blk, blk, blk],
        out_specs=pl.BlockSpec((S, 2 * V_HEAD), lambda b, hp: (b, hp)),
        out_shape=jax.ShapeDtypeStruct((T, MLA_HEADS * V_HEAD), bf16),
        scratch_shapes=[
            pltpu.VMEM((tq, 1), f32),
            pltpu.VMEM((tq, 1), f32),
            pltpu.VMEM((tq, HEAD_TILE), f32),
        ],
        compiler_params=_cparams(("parallel", "parallel")),
        name="mla_flash",
    )(q, k, kv)


def _absorb_q_kernel(q_ref, w_ref, o_ref):
    lane = lax.broadcasted_iota(jnp.int32, q_ref.shape, 1)
    qn = jnp.where(lane < QK_NOPE, q_ref[...], jnp.zeros_like(q_ref[...]))
    o_ref[0] = _dot_nt(qn, w_ref[...]).astype(bf16)


def _absorb_q(q, wukv):
    DB = q.shape[0]
    return pl.pallas_call(
        _absorb_q_kernel,
        grid=(MLA_HEADS,),
        in_specs=[
            pl.BlockSpec((DB, HEAD_TILE), lambda h: (0, h)),
            pl.BlockSpec((KV_LORA, HEAD_TILE), lambda h: (0, h)),
        ],
        out_specs=pl.BlockSpec((1, DB, KV_LORA), lambda h: (h, 0, 0)),
        out_shape=jax.ShapeDtypeStruct((MLA_HEADS, DB, KV_LORA), bf16),
        compiler_params=_cparams(("parallel",)),
        name="mla_absorb_q",
    )(q, wukv)


def _unabsorb_o_kernel(o_ref, w_ref, out_ref):
    out_ref[...] = _dot(o_ref[0].astype(bf16), w_ref[...])


def _unabsorb_o(o_lat, wukv):
    DB = o_lat.shape[1]
    return pl.pallas_call(
        _unabsorb_o_kernel,
        grid=(MLA_HEADS,),
        in_specs=[
            pl.BlockSpec((1, DB, KV_LORA), lambda h: (h, 0, 0)),
            pl.BlockSpec((KV_LORA, HEAD_TILE), lambda h: (0, h)),
        ],
        out_specs=pl.BlockSpec((DB, HEAD_TILE), lambda h: (0, h)),
        out_shape=jax.ShapeDtypeStruct((DB, MLA_HEADS * HEAD_TILE), f32),
        compiler_params=_cparams(("parallel",)),
        name="mla_unabsorb_o",
    )(o_lat, wukv)


def _decode_kernel(pt_ref, qlat_ref, qpe_ref, cnew_ref, pnew_ref, *refs, pages_per_step):
    PP = pages_per_step
    ckv_refs = refs[:PP]
    kpe_refs = refs[PP:2 * PP]
    o_ref, m_ref, l_ref, acc_ref = refs[2 * PP:]
    g = pl.program_id(1)

    @pl.when(g == 0)
    def _():
        m_ref[...] = jnp.full(m_ref.shape, -jnp.inf, f32)
        l_ref[...] = jnp.zeros(l_ref.shape, f32)
        acc_ref[...] = jnp.zeros(acc_ref.shape, f32)

    qlat = qlat_ref[0]
    qpe = qpe_ref[0]
    cks = [ckv_refs[p][0].astype(bf16) for p in range(PP)]
    s = jnp.concatenate(
        [_dot_nt(qlat, cks[p]) + _dot_nt(qpe, kpe_refs[p][0].astype(bf16)) for p in range(PP)],
        axis=1)
    m_prev = m_ref[...]
    m_new = jnp.maximum(m_prev, jnp.max(s, axis=-1, keepdims=True))
    p_all = jnp.exp(s - m_new)
    alpha = jnp.exp(m_prev - m_new)
    l_ref[...] = alpha * l_ref[...] + jnp.sum(p_all, axis=-1, keepdims=True)
    pv = _dot(p_all[:, :PAGE_SIZE].astype(bf16), cks[0])
    for p in range(1, PP):
        pv = pv + _dot(p_all[:, p * PAGE_SIZE:(p + 1) * PAGE_SIZE].astype(bf16), cks[p])
    acc_ref[...] = alpha * acc_ref[...] + pv
    m_ref[...] = m_new

    @pl.when(g == pl.num_programs(1) - 1)
    def _():
        cnew = cnew_ref[0].astype(bf16).astype(f32)
        pnew = pnew_ref[0].astype(bf16).astype(f32)
        s_self = (jnp.sum(qlat.astype(f32) * cnew, axis=-1, keepdims=True)
                  + jnp.sum(qpe.astype(f32) * pnew, axis=-1, keepdims=True))
        m_prev = m_ref[...]
        m_new = jnp.maximum(m_prev, s_self)
        p_self = jnp.exp(s_self - m_new)
        alpha = jnp.exp(m_prev - m_new)
        l_fin = alpha * l_ref[...] + p_self
        acc = alpha * acc_ref[...] + p_self * cnew
        o_ref[0] = acc / l_fin


def _decode_attention(page_table, qlat, qpe, ckv_new, kpe_new, pool_ckv, pool_kpe, pages_per_step=8):
    DB, n_pages = page_table.shape
    PP = math.gcd(n_pages, pages_per_step)
    H = MLA_HEADS

    def page_map(p):
        return lambda b, g, pt: (pt[b, g * PP + p], 0, 0)

    per_b = lambda b, g, pt: (b, 0, 0)
    in_specs = [
        pl.BlockSpec((1, H, KV_LORA), per_b),
        pl.BlockSpec((1, H, QK_ROPE), per_b),
        pl.BlockSpec((1, 1, KV_LORA), per_b),
        pl.BlockSpec((1, 1, QK_ROPE), per_b),
    ]
    in_specs += [pl.BlockSpec((1, PAGE_SIZE, KV_LORA), page_map(p)) for p in range(PP)]
    in_specs += [pl.BlockSpec((1, PAGE_SIZE, QK_ROPE), page_map(p)) for p in range(PP)]
    grid_spec = pltpu.PrefetchScalarGridSpec(
        num_scalar_prefetch=1,
        grid=(DB, n_pages // PP),
        in_specs=in_specs,
        out_specs=pl.BlockSpec((1, H, KV_LORA), per_b),
        scratch_shapes=[
            pltpu.VMEM((H, 1), f32),
            pltpu.VMEM((H, 1), f32),
            pltpu.VMEM((H, KV_LORA), f32),
        ],
    )
    return pl.pallas_call(
        functools.partial(_decode_kernel, pages_per_step=PP),
        grid_spec=grid_spec,
        out_shape=jax.ShapeDtypeStruct((DB, H, KV_LORA), f32),
        compiler_params=_cparams(("parallel", "arbitrary")),
        name="mla_decode",
    )(page_table, qlat, qpe, ckv_new, kpe_new, *([pool_ckv] * PP), *([pool_kpe] * PP))


def _proj_ffn_kernel(x_ref, o_ref, wo_ref, gffn_ref, wg_ref, wu_ref, wout_ref, gfin_ref,
                     out_ref, acc_ref, h_ref, *, final_norm):
    j = pl.program_id(1)

    @pl.when(j == 0)
    def _():
        x1 = x_ref[...] + _dot(o_ref[...], wo_ref[...])
        acc_ref[...] = x1
        h_ref[...] = _rms(x1, gffn_ref[...]).astype(bf16)

    h = h_ref[...]
    gate = _dot(h, wg_ref[...])
    up = _dot(h, wu_ref[...])
    act = (gate * jax.nn.sigmoid(gate) * up).astype(bf16)
    acc_ref[...] += _dot(act, wout_ref[...])

    @pl.when(j == pl.num_programs(1) - 1)
    def _():
        y = acc_ref[...]
        if final_norm:
            y = _rms(y, gfin_ref[...])
        out_ref[...] = y


def _proj_ffn(x, o, wo, gffn, w_in, w_out, gfin, final_norm, tm, th):
    T = x.shape[0]
    nh = FFN_HIDDEN // th
    row = lambda i, j: (i, 0)
    const = lambda i, j: (0, 0)
    return pl.pallas_call(
        functools.partial(_proj_ffn_kernel, final_norm=final_norm),
        grid=(T // tm, nh),
        in_specs=[
            pl.BlockSpec((tm, D_MODEL), row),
            pl.BlockSpec((tm, D_MODEL), row),
            pl.BlockSpec((D_MODEL, D_MODEL), const),
            pl.BlockSpec((1, D_MODEL), const),
            pl.BlockSpec((D_MODEL, th), lambda i, j: (0, j)),
            pl.BlockSpec((D_MODEL, th), lambda i, j: (0, j + nh)),
            pl.BlockSpec((th, D_MODEL), lambda i, j: (j, 0)),
            pl.BlockSpec((1, D_MODEL), const),
        ],
        out_specs=pl.BlockSpec((tm, D_MODEL), row),
        out_shape=jax.ShapeDtypeStruct((T, D_MODEL), f32),
        scratch_shapes=[pltpu.VMEM((tm, D_MODEL), f32), pltpu.VMEM((tm, D_MODEL), bf16)],
        compiler_params=_cparams(("parallel", "arbitrary")),
        name="proj_ffn",
    )(x, o, wo, gffn, w_in, w_in, w_out, gfin)


def _seg_sum(x, ones_blk):
    parts = []
    for gidx in range(x.shape[1] // LANES):
        xg = x[:, gidx * LANES:(gidx + 1) * LANES]
        hi = xg.astype(bf16)
        lo = (xg - hi.astype(f32)).astype(bf16)
        parts.append(_dot(hi, ones_blk) + _dot(lo, ones_blk))
    return jnp.concatenate(parts, axis=1)


def _rwkv_proj_kernel(x_ref, prev_ref, gmix_ref, mix_ref, wrkv_ref, w0_ref, w1_ref, w2_ref,
                      a0_ref, a1_ref, a2_ref, g1_ref, g2_ref, kk_ref, ka_ref, rk_ref, ones_ref,
                      r_out, k_out, v_out, kn_out, b_out, ld_out, g_out, bonus_out, h_out,
                      *, seq_mode, tiles_per_seq):
    tm = x_ref.shape[0]
    gm = gmix_ref[...]
    h = _rms(x_ref[...], gm)
    if seq_mode:
        hp_row = _rms(prev_ref[...], gm)[7:8, :]
        is_start = (pl.program_id(0) % tiles_per_seq) == 0
        hp_row = jnp.where(is_start, jnp.zeros_like(hp_row), hp_row)
        rowid = lax.broadcasted_iota(jnp.int32, (tm, 1), 0)
        hprev = jnp.where(rowid == 0, hp_row, pltpu.roll(h, 1, 0))
        h_out[0] = h[tm - 8:, :]
    else:
        hprev = prev_ref[...]
        h_out[...] = h
    xx = hprev - h
    mix = mix_ref[...]
    xr, xw, xk, xv, xa, xg = ((h + xx * mix[n:n + 1]).astype(bf16) for n in range(6))
    r = _dot(xr, wrkv_ref[0])
    k = _dot(xk, wrkv_ref[1])
    v = _dot(xv, wrkv_ref[2])
    wl = w0_ref[...] + _dot(jnp.tanh(_dot(xw, w1_ref[...])).astype(bf16), w2_ref[...])
    z = -wl
    w_log = -(jnp.maximum(z, 0.0) + jnp.log(1.0 + jnp.exp(-jnp.abs(z)))) - 0.5
    ld_out[...] = -jnp.exp(w_log)
    a = jax.nn.sigmoid(a0_ref[...] + _dot(_dot(xa, a1_ref[...]).astype(bf16), a2_ref[...]))
    g = _dot(jax.nn.sigmoid(_dot(xg, g1_ref[...])).astype(bf16), g2_ref[...])
    ones_blk = ones_ref[...]
    kk = k * kk_ref[...]
    kk = kk / jnp.maximum(jnp.sqrt(_seg_sum(kk * kk, ones_blk)), 1e-12)
    k = k * (1.0 + (a - 1.0) * ka_ref[...])
    bonus = _seg_sum(r * k * rk_ref[...], ones_blk) * v
    r_out[...] = r.astype(r_out.dtype)
    k_out[...] = k.astype(k_out.dtype)
    v_out[...] = v.astype(v_out.dtype)
    kn_out[...] = kk.astype(kn_out.dtype)
    b_out[...] = (kk * a).astype(b_out.dtype)
    g_out[...] = g.astype(g_out.dtype)
    bonus_out[...] = bonus.astype(bonus_out.dtype)


def _rwkv_proj(x, prev, p, seq_len, tm, vec_dtype):
    T = x.shape[0]
    seq_mode = seq_len > 1
    tiles_per_seq = max(seq_len // tm, 1)
    row = lambda i: (i, 0)
    const = lambda i: (0, 0)
    const3 = lambda i: (0, 0, 0)
    if seq_mode:
        prev_spec = pl.BlockSpec((8, D_MODEL), lambda i: (jnp.maximum(i * (tm // 8) - 1, 0), 0))
        h_spec = pl.BlockSpec((1, 8, D_MODEL), lambda i: (i, 0, 0))
        h_shape = jax.ShapeDtypeStruct((T // tm, 8, D_MODEL), f32)
    else:
        prev_spec = pl.BlockSpec((tm, D_MODEL), row)
        h_spec = pl.BlockSpec((tm, D_MODEL), row)
        h_shape = jax.ShapeDtypeStruct((T, D_MODEL), f32)
    vec = lambda dt: jax.ShapeDtypeStruct((T, D_MODEL), dt)
    tile = pl.BlockSpec((tm, D_MODEL), row)
    full = lambda arr: pl.BlockSpec(arr.shape, const3 if arr.ndim == 3 else const)
    weights = [p["gmix"], p["mix"], p["w_rkv"], p["w0"], p["w1"], p["w2"], p["a0"], p["a1"], p["a2"],
               p["g1"], p["g2"], p["k_k"], p["k_a"], p["r_k"], p["ones_blk"]]
    return pl.pallas_call(
        functools.partial(_rwkv_proj_kernel, seq_mode=seq_mode, tiles_per_seq=tiles_per_seq),
        grid=(T // tm,),
        in_specs=[tile, prev_spec] + [full(w) for w in weights],
        out_specs=[tile] * 8 + [h_spec],
        out_shape=[vec(vec_dtype)] * 5 + [vec(f32), vec(bf16), vec(bf16), h_shape],
        compiler_params=_cparams(("parallel",)),
        name="rwkv_proj",
    )(x, prev, *weights)


def _wkv_chunk_kernel(r_ref, k_ref, v_ref, kn_ref, b_ref, ld_ref, y_ref, st_ref, m_ref):
    C = WKV_CHUNK
    G = WKV_GROUP
    W = GROUP_W
    c_idx = pl.program_id(1)

    @pl.when(c_idx == 0)
    def _():
        m_ref[...] = jnp.zeros(m_ref.shape, f32)

    rr = lax.broadcasted_iota(jnp.int32, (W, W), 0)
    cc = lax.broadcasted_iota(jnp.int32, (W, W), 1)
    bd_mask = (rr // RWKV_HEAD) == (cc // RWKV_HEAD)
    t_idx = lax.broadcasted_iota(jnp.int32, (C, W), 0)
    s_idx = lax.broadcasted_iota(jnp.int32, (C, W), 1) % C
    strict = s_idx < t_idx
    incl = s_idx <= t_idx
    eye_cat = (s_idx == t_idx).astype(f32)
    tri = (lax.broadcasted_iota(jnp.int32, (C, C), 1) <= lax.broadcasted_iota(jnp.int32, (C, C), 0)).astype(f32)
    ones_c = jnp.ones((C, LANES), f32)

    def bd(y):
        yb = y.astype(bf16)
        return jnp.where(bd_mask, jnp.concatenate([yb] * G, axis=0), jnp.zeros((W, W), bf16))

    def bdmm(x, y):
        return _dot(x.astype(bf16), bd(y))

    for gi in range(r_ref.shape[1] // W):
        gs = slice(gi * W, (gi + 1) * W)
        ld = ld_ref[:, gs]
        cs = jnp.dot(tri, ld, precision=lax.Precision.HIGHEST, preferred_element_type=f32)
        tot = cs[C - 1:C, :]
        r = r_ref[:, gs].astype(f32)
        k = k_ref[:, gs].astype(f32)
        v = v_ref[:, gs].astype(f32)
        kn = kn_ref[:, gs].astype(f32)
        b = b_ref[:, gs].astype(f32)
        p_inv = jnp.exp(-cs)
        p_rest = jnp.exp(tot - cs)
        a_t = -kn * jnp.exp(cs - ld)
        r_t = r * jnp.exp(cs)
        lhs = jnp.concatenate([a_t, r_t], axis=0).astype(bf16)
        s_b = _dot_nt(lhs, bd(b * p_inv))
        s_k = _dot_nt(lhs, bd(k * p_inv))
        l_ab = jnp.where(strict, s_b[:C], 0.0)
        l_ak = jnp.where(strict, s_k[:C], 0.0)
        a_rb = jnp.where(incl, s_b[C:], 0.0)
        a_rk = jnp.where(incl, s_k[C:], 0.0)
        t_inv = eye_cat + l_ab
        pw = l_ab
        step = 1
        while 2 * step < C:
            pw = bdmm(pw, pw)
            t_inv = t_inv + bdmm(t_inv, pw)
            step *= 2
        w1 = bdmm(t_inv, a_t)
        w2 = bdmm(t_inv, bdmm(l_ak, v))
        y_v = bdmm(a_rk, v)
        m0 = m_ref[gi]
        m0b = m0.astype(bf16)
        u = _dot(w1.astype(bf16), m0b) + w2
        y = _dot(r_t.astype(bf16), m0b) + bdmm(a_rb, u) + y_v
        y_ref[:, gs] = y
        upd = _dot_tn(jnp.concatenate([b * p_rest, k * p_rest], axis=0).astype(bf16),
                      jnp.concatenate([u, v], axis=0).astype(bf16))
        tot_col = lax.dot_general(ld, ones_c, (((0,), (0,)), ((), ())),
                                  precision=lax.Precision.HIGHEST, preferred_element_type=f32)
        decay_col = jnp.exp(jnp.concatenate([tot_col] * (W // LANES), axis=1))
        m_new = m0 * decay_col + jnp.where(bd_mask, upd, 0.0)
        m_ref[gi] = m_new

        @pl.when(c_idx == pl.num_programs(1) - 1)
        def _():
            acc = m_new[:RWKV_HEAD]
            for hh in range(1, G):
                acc = acc + m_new[hh * RWKV_HEAD:(hh + 1) * RWKV_HEAD]
            st_ref[0, :, gs] = acc


def _wkv_chunked(r, k, v, kn, b, ld, B, S):
    T = B * S
    C = WKV_CHUNK
    nc = S // C
    tile = pl.BlockSpec((C, D_MODEL), lambda bi, ci: (bi * nc + ci, 0))
    return pl.pallas_call(
        _wkv_chunk_kernel,
        grid=(B, nc),
        in_specs=[tile] * 6,
        out_specs=[tile, pl.BlockSpec((1, RWKV_HEAD, D_MODEL), lambda bi, ci: (bi, 0, 0))],
        out_shape=[jax.ShapeDtypeStruct((T, D_MODEL), f32),
                   jax.ShapeDtypeStruct((B, RWKV_HEAD, D_MODEL), f32)],
        scratch_shapes=[pltpu.VMEM((D_MODEL // GROUP_W, GROUP_W, GROUP_W), f32)],
        compiler_params=_cparams(("parallel", "arbitrary")),
        name="wkv_chunked",
    )(r, k, v, kn, b, ld)


def _wkv_step_kernel(s_ref, r_ref, k_ref, v_ref, kn_ref, b_ref, ld_ref, snew_ref, y_ref):
    N = RWKV_HEAD
    eye = lax.broadcasted_iota(jnp.int32, (N, N), 0) == lax.broadcasted_iota(jnp.int32, (N, N), 1)
    ys = []
    for hh in range(RWKV_HEADS):
        sl = slice(hh * N, (hh + 1) * N)
        st = s_ref[0, hh].astype(f32)
        r = r_ref[0, :, sl]
        k = k_ref[0, :, sl]
        v = v_ref[0, :, sl]
        a = -kn_ref[0, :, sl]
        b = b_ref[0, :, sl]
        w = jnp.exp(ld_ref[0, :, sl])
        sa = jnp.sum(st * a, axis=1, keepdims=True)
        v_col = jnp.sum(jnp.where(eye, jnp.broadcast_to(v, (N, N)), 0.0), axis=1, keepdims=True)
        st = st * w + sa * b + v_col * k
        snew_ref[0, hh] = st.astype(snew_ref.dtype)
        y_col = jnp.sum(st * r, axis=1, keepdims=True)
        ys.append(jnp.sum(jnp.where(eye, jnp.broadcast_to(y_col, (N, N)), 0.0), axis=0, keepdims=True))
    y_ref[0] = jnp.concatenate(ys, axis=1)


def _wkv_step(state, r, k, v, kn, b, ld):
    DB = state.shape[0]
    vec = pl.BlockSpec((1, 1, D_MODEL), lambda i: (i, 0, 0))
    st = pl.BlockSpec((1, RWKV_HEADS, RWKV_HEAD, RWKV_HEAD), lambda i: (i, 0, 0, 0))
    r3 = lambda t: t.reshape(DB, 1, D_MODEL)
    snew, y = pl.pallas_call(
        _wkv_step_kernel,
        grid=(DB,),
        in_specs=[st] + [vec] * 6,
        out_specs=[st, vec],
        out_shape=[jax.ShapeDtypeStruct(state.shape, state.dtype),
                   jax.ShapeDtypeStruct((DB, 1, D_MODEL), f32)],
        compiler_params=_cparams(("parallel",)),
        name="wkv_step",
    )(state, r3(r), r3(k), r3(v), r3(kn), r3(b), r3(ld))
    return snew, y.reshape(DB, D_MODEL)


def _rwkv_post_kernel(y_ref, bonus_ref, g_ref, lnw_ref, lnb_ref, ones_ref, o_ref):
    ones_blk = ones_ref[...]
    y = y_ref[...]
    inv_n = 1.0 / RWKV_HEAD
    mu = _seg_sum(y, ones_blk) * inv_n
    d = y - mu
    var = _seg_sum(d * d, ones_blk) * inv_n
    yn = d * lax.rsqrt(var + GN_EPS) * lnw_ref[...] + lnb_ref[...]
    o_ref[...] = ((yn + bonus_ref[...].astype(f32)) * g_ref[...].astype(f32)).astype(bf16)


def _rwkv_post(y, bonus, g, lnw, lnb, ones_blk, tm):
    T = y.shape[0]
    row = lambda i: (i, 0)
    const = lambda i: (0, 0)
    tile = pl.BlockSpec((tm, D_MODEL), row)
    return pl.pallas_call(
        _rwkv_post_kernel,
        grid=(T // tm,),
        in_specs=[tile, tile, tile, pl.BlockSpec((1, D_MODEL), const), pl.BlockSpec((1, D_MODEL), const),
                  pl.BlockSpec((LANES, LANES), const)],
        out_specs=tile,
        out_shape=jax.ShapeDtypeStruct((T, D_MODEL), bf16),
        compiler_params=_cparams(("parallel",)),
        name="rwkv_post",
    )(y, bonus, g, lnw, lnb, ones_blk)


def _rope_tables(pos):
    half = QK_ROPE // 2
    inv = 1.0 / (ROPE_THETA ** (jnp.arange(0, QK_ROPE, 2, dtype=f32) / QK_ROPE))
    ang = pos[:, None] * inv[None, :]
    cos, sin = jnp.cos(ang), jnp.sin(ang)
    n = pos.shape[0]
    ones = jnp.ones((n, QK_NOPE), f32)
    z = lambda w: jnp.zeros((n, w), f32)
    tab_c = jnp.concatenate([ones, cos, cos, z(HEAD_TILE - QK_NOPE - QK_ROPE)], axis=1)
    tab_s1 = jnp.concatenate([z(QK_NOPE), -sin, z(HEAD_TILE - QK_NOPE - half)], axis=1)
    tab_s2 = jnp.concatenate([z(QK_NOPE + half), sin, z(HEAD_TILE - QK_NOPE - QK_ROPE)], axis=1)
    return tab_c, tab_s1, tab_s2


def _pick_tile(n, pref):
    t = min(n, pref)
    while n % t:
        t //= 2
    return t


def kernel(x_prompt, x_sample, cache_ckv, cache_kpe, state_wkv, state_shift, page_table,
           norm_mix, norm_ffn, norm_final,
           mla_w_a, mla_q_norm, mla_kv_norm, mla_w_uq, mla_w_ukv, mla_w_o,
           rw_mix, rw_w_rkv, rw_w0, rw_w1, rw_w2, rw_a0, rw_a1, rw_a2, rw_g1, rw_g2,
           rw_k_k, rw_k_a, rw_r_k, rw_ln_w, rw_ln_b, rw_w_o,
           ffn_w_in, ffn_w_out):
    B, S, D = x_prompt.shape
    DB, DS, _ = x_sample.shape
    assert D == D_MODEL and DS == 1 and S % WKV_CHUNK == 0
    n_pages = page_table.shape[1]
    past_len = n_pages * PAGE_SIZE
    H = MLA_HEADS
    row = lambda t: t.reshape(1, -1)
    xp = x_prompt.reshape(B * S, D)
    xs = x_sample.reshape(DB, D)

    w_a = mla_w_a[0]
    pad_pe = jnp.zeros((D, HEAD_TILE), f32).at[:, QK_NOPE:QK_NOPE + QK_ROPE].set(w_a[:, Q_LORA + KV_LORA:])
    wa_ext = jnp.concatenate([w_a[:, :Q_LORA + KV_LORA], pad_pe], axis=1).astype(bf16)
    wuq = (mla_w_uq[0] * MLA_SCALE).reshape(Q_LORA, H, QK_NOPE + QK_ROPE)
    wuq = jnp.pad(wuq, ((0, 0), (0, 0), (0, HEAD_TILE - QK_NOPE - QK_ROPE))).reshape(Q_LORA, H * HEAD_TILE).astype(bf16)
    wukv = mla_w_ukv[0].reshape(KV_LORA, H * HEAD_TILE).astype(bf16)
    mla_wo = mla_w_o[0].astype(bf16)
    qn, kvn = row(mla_q_norm[0]), row(mla_kv_norm[0])
    ffn_in = ffn_w_in.astype(bf16)
    ffn_out = ffn_w_out.astype(bf16)
    hh = jnp.arange(LANES) // RWKV_HEAD
    ones_blk = (hh[:, None] == hh[None, :]).astype(bf16)
    rwp = dict(
        gmix=row(norm_mix[1]), mix=jnp.pad(rw_mix[0], ((0, 2), (0, 0))), w_rkv=rw_w_rkv[0].astype(bf16),
        w0=row(rw_w0[0]), w1=rw_w1[0].astype(bf16), w2=rw_w2[0].astype(bf16),
        a0=row(rw_a0[0]), a1=rw_a1[0].astype(bf16), a2=rw_a2[0].astype(bf16),
        g1=rw_g1[0].astype(bf16), g2=rw_g2[0].astype(bf16),
        k_k=row(rw_k_k[0]), k_a=row(rw_k_a[0]), r_k=row(rw_r_k[0]), ones_blk=ones_blk)
    rw_wo = rw_w_o[0].astype(bf16)
    lnw, lnb = row(rw_ln_w[0]), row(rw_ln_b[0])
    gfin = row(norm_final)

    tm_p = _pick_tile(B * S, 512)
    tm_s = _pick_tile(DB, 128)
    th = FFN_HIDDEN // 2

    tabs_p = _rope_tables(jnp.arange(S, dtype=f32))
    tabs_s = _rope_tables(jnp.full((tm_s,), past_len, f32))
    q_p, k_p, kv_p, ckv_p, kpe_p = _mla_proj(xp, row(norm_mix[0]), wa_ext, qn, kvn, wuq, wukv, tabs_p,
                                             _pick_tile(S, 512))
    o_p = _flash_attention(q_p, k_p, kv_p, B, S)
    xp = _proj_ffn(xp, o_p, mla_wo, row(norm_ffn[0]), ffn_in[0], ffn_out[0], gfin, False, tm_p, th)

    q_s, _, _, ckv_s, kpe_s = _mla_proj(xs, row(norm_mix[0]), wa_ext, qn, kvn, wuq, wukv, tabs_s, tm_s)
    qlat = jnp.swapaxes(_absorb_q(q_s, wukv), 0, 1)
    qpe = q_s.reshape(DB, H, HEAD_TILE)[:, :, QK_NOPE:QK_NOPE + QK_ROPE]
    o_lat = _decode_attention(page_table, qlat, qpe, ckv_s.reshape(DB, 1, KV_LORA), kpe_s.reshape(DB, 1, QK_ROPE),
                              cache_ckv.reshape(-1, PAGE_SIZE, KV_LORA), cache_kpe.reshape(-1, PAGE_SIZE, QK_ROPE))
    o_s = _unabsorb_o(jnp.swapaxes(o_lat, 0, 1), wukv)
    o_s = o_s.reshape(DB, H, HEAD_TILE)[:, :, QK_NOPE:].reshape(DB, H * V_HEAD).astype(bf16)
    xs = _proj_ffn(xs, o_s, mla_wo, row(norm_ffn[0]), ffn_in[0], ffn_out[0], gfin, False, tm_s, th)

    tm_r = _pick_tile(S, 256)
    r, k, v, kn, b, ld, g, bonus, hl = _rwkv_proj(xp, xp, rwp, S, tm_r, bf16)
    shift_p = hl.reshape(B, S // tm_r, 8, D)[:, -1, 7, :]
    y, st = _wkv_chunked(r, k, v, kn, b, ld, B, S)
    wkv_p = jnp.transpose(st.reshape(B, RWKV_HEAD, RWKV_HEADS, RWKV_HEAD), (0, 2, 3, 1))
    yo = _rwkv_post(y, bonus, g, lnw, lnb, ones_blk, tm_p)
    y_prompt = _proj_ffn(xp, yo, rw_wo, row(norm_ffn[1]), ffn_in[1], ffn_out[1], gfin, True, tm_p, th)

    r, k, v, kn, b, ld, g, bonus, shift_s = _rwkv_proj(xs, state_shift[0], rwp, 1, tm_s, f32)
    wkv_s, y = _wkv_step(state_wkv[0], r, k, v, kn, b, ld)
    yo = _rwkv_post(y, bonus, g, lnw, lnb, ones_blk, tm_s)
    y_sample = _proj_ffn(xs, yo, rw_wo, row(norm_ffn[1]), ffn_in[1], ffn_out[1], gfin, True, tm_s, th)

    return (y_prompt.reshape(B, S, D), y_sample.reshape(DB, DS, D),
            ckv_p.reshape(1, B, S, KV_LORA), kpe_p.reshape(1, B, S, QK_ROPE),
            ckv_s.reshape(1, DB, DS, KV_LORA), kpe_s.reshape(1, DB, DS, QK_ROPE),
            wkv_p[None].astype(x_prompt.dtype), shift_p[None],
            wkv_s[None], shift_s[None])
```

```python
import functools
import math

import jax
import jax.numpy as jnp
from jax import lax
from jax.experimental import pallas as pl
from jax.experimental.pallas import tpu as pltpu

f32 = jnp.float32
bf16 = jnp.bfloat16

D_MODEL = 1024
MLA_HEADS = 16
QK_NOPE = 64
QK_ROPE = 32
V_HEAD = 64
Q_LORA = 512
KV_LORA = 256
ROPE_THETA = 10000.0
MLA_SCALE = 1.0 / math.sqrt(QK_NOPE + QK_ROPE)
LOG2_E = math.log2(math.e)
RWKV_HEAD = 64
RWKV_HEADS = D_MODEL // RWKV_HEAD
FFN_HIDDEN = 2816
NORM_EPS = 1e-6
GN_EPS = 64e-5
PAGE_SIZE = 128

LANES = 128
HEAD_TILE = 128
WKV_CHUNK = 64
WKV_GROUP = 4
WKV_CHUNKS_PER_STEP = 2
GROUP_W = WKV_GROUP * RWKV_HEAD
VMEM_LIMIT = 56 * 1024 * 1024


def _cparams(sem):
    return pltpu.CompilerParams(dimension_semantics=sem, vmem_limit_bytes=VMEM_LIMIT)


def _rms(x, g):
    return x * lax.rsqrt(jnp.mean(x * x, axis=-1, keepdims=True) + NORM_EPS) * g


def _dot(a, b):
    return jnp.dot(a, b, preferred_element_type=f32)


def _dot_nt(a, b):
    return lax.dot_general(a, b, (((1,), (1,)), ((), ())), preferred_element_type=f32)


def _dot_tn(a, b):
    return lax.dot_general(a, b, (((0,), (0,)), ((), ())), preferred_element_type=f32)


def _mla_proj_kernel(x_ref, gmix_ref, wa_ref, qn_ref, kvn_ref, wuq_ref, wukv_ref,
                     c_ref, s1_ref, s2_ref, q_ref, k_ref, kv_ref, ckv_ref, kpe_ref):
    tm = x_ref.shape[0]
    h = _rms(x_ref[...], gmix_ref[...]).astype(bf16)
    a = _dot(h, wa_ref[...])
    cq = _rms(a[:, :Q_LORA], qn_ref[...]).astype(bf16)
    ckv = _rms(a[:, Q_LORA:Q_LORA + KV_LORA], kvn_ref[...])
    ckv_ref[...] = ckv
    cos = c_ref[...]
    sin_lo = s1_ref[...]
    sin_hi = s2_ref[...]

    def rope(t):
        return t * cos + pltpu.roll(t, LANES - QK_ROPE // 2, 1) * sin_lo + pltpu.roll(t, QK_ROPE // 2, 1) * sin_hi

    kpe_t = rope(a[:, Q_LORA + KV_LORA:])
    kpe_ref[...] = kpe_t[:, QK_NOPE:QK_NOPE + QK_ROPE]
    kv = _dot(ckv.astype(bf16), wukv_ref[...])
    kv_ref[...] = kv.astype(bf16)
    q = _dot(cq, wuq_ref[...])
    lane = lax.broadcasted_iota(jnp.int32, (tm, HEAD_TILE), 1)
    for hh in range(MLA_HEADS):
        sl = slice(hh * HEAD_TILE, (hh + 1) * HEAD_TILE)
        q_ref[:, sl] = rope(q[:, sl]).astype(bf16)
        k_ref[:, sl] = jnp.where(lane < QK_NOPE, kv[:, sl], kpe_t).astype(bf16)


def _mla_proj(x, gmix, wa, qn, kvn, wuq, wukv, tabs, tm):
    T = x.shape[0]
    tab_c, tab_s1, tab_s2 = tabs
    nt = tab_c.shape[0] // tm
    HW = MLA_HEADS * HEAD_TILE
    const = lambda i: (0, 0)
    row = lambda i: (i, 0)
    tabmap = lambda i: (i % nt, 0)
    return pl.pallas_call(
        _mla_proj_kernel,
        grid=(T // tm,),
        in_specs=[
            pl.BlockSpec((tm, D_MODEL), row),
            pl.BlockSpec((1, D_MODEL), const),
            pl.BlockSpec(wa.shape, const),
            pl.BlockSpec((1, Q_LORA), const),
            pl.BlockSpec((1, KV_LORA), const),
            pl.BlockSpec(wuq.shape, const),
            pl.BlockSpec(wukv.shape, const),
            pl.BlockSpec((tm, HEAD_TILE), tabmap),
            pl.BlockSpec((tm, HEAD_TILE), tabmap),
            pl.BlockSpec((tm, HEAD_TILE), tabmap),
        ],
        out_specs=[
            pl.BlockSpec((tm, HW), row),
            pl.BlockSpec((tm, HW), row),
            pl.BlockSpec((tm, HW), row),
            pl.BlockSpec((tm, KV_LORA), row),
            pl.BlockSpec((tm, QK_ROPE), row),
        ],
        out_shape=[
            jax.ShapeDtypeStruct((T, HW), bf16),
            jax.ShapeDtypeStruct((T, HW), bf16),
            jax.ShapeDtypeStruct((T, HW), bf16),
            jax.ShapeDtypeStruct((T, KV_LORA), f32),
            jax.ShapeDtypeStruct((T, QK_ROPE), f32),
        ],
        compiler_params=_cparams(("parallel",)),
        name="mla_proj",
    )(x, gmix, wa, qn, kvn, wuq, wukv, tab_c, tab_s1, tab_s2)


def _flash_kernel(q_ref, k_ref, kv_ref, o_ref, vt_ref, m_ref, l_ref, acc_ref, *, tile):
    S = q_ref.shape[0]
    nt = S // tile
    krow = lax.broadcasted_iota(jnp.int32, (tile, tile), 0)
    qcol = lax.broadcasted_iota(jnp.int32, (tile, tile), 1)
    lane = lax.broadcasted_iota(jnp.int32, (tile, HEAD_TILE), 1)
    heads = range(q_ref.shape[1] // HEAD_TILE)
    hs = [slice(hh * HEAD_TILE, (hh + 1) * HEAD_TILE) for hh in heads]

    for hh in heads:
        for j in range(nt):
            vt_ref[hh, j] = kv_ref[j * tile:(j + 1) * tile, hs[hh]].astype(f32).T.astype(bf16)

    def q_body(qi, c):
        q0 = pl.multiple_of(qi * tile, tile)
        qs = [q_ref[pl.ds(q0, tile), hs[hh]] for hh in heads]
        for hh in heads:
            m_ref[hh] = jnp.full((1, tile), -jnp.inf, f32)
            l_ref[hh] = jnp.zeros((1, tile), f32)
            acc_ref[hh] = jnp.zeros((HEAD_TILE, tile), f32)

        def block(ki, diag):
            k0 = pl.multiple_of(ki * tile, tile)
            sts = [_dot_nt(k_ref[pl.ds(k0, tile), hs[hh]], qs[hh]) for hh in heads]
            if diag:
                sts = [jnp.where(krow <= qcol, st, -jnp.inf) for st in sts]
            m_prevs = [m_ref[hh] for hh in heads]
            m_news = [jnp.maximum(m_prevs[hh], jnp.max(sts[hh], axis=0, keepdims=True)) for hh in heads]
            ps = [jnp.exp2(sts[hh] - m_news[hh]) for hh in heads]
            alphas = [jnp.exp2(m_prevs[hh] - m_news[hh]) for hh in heads]
            pvs = [_dot(vt_ref[hh, ki], ps[hh].astype(bf16)) for hh in heads]
            for hh in heads:
                l_ref[hh] = alphas[hh] * l_ref[hh] + jnp.sum(ps[hh], axis=0, keepdims=True)
                m_ref[hh] = m_news[hh]
            for hh in heads:
                acc_ref[hh] = alphas[hh] * acc_ref[hh] + pvs[hh]

        def k_body(ki, c2):
            block(ki, False)
            return c2

        lax.fori_loop(0, qi, k_body, 0)
        block(qi, True)
        outs = [(acc_ref[hh] / l_ref[hh]).T for hh in heads]
        for pr in range(len(outs) // 2):
            o = jnp.where(lane < V_HEAD, pltpu.roll(outs[2 * pr], V_HEAD, 1), outs[2 * pr + 1])
            o_ref[pl.ds(q0, tile), pr * 2 * V_HEAD:(pr + 1) * 2 * V_HEAD] = o.astype(bf16)
        return c

    lax.fori_loop(0, nt, q_body, 0)


def _flash_attention(q, k, kv, B, S, tile=512, heads_per_step=4):
    T = B * S
    tile = min(tile, S)
    nh = heads_per_step
    blk = pl.BlockSpec((S, nh * HEAD_TILE), lambda b, hp: (b, hp))
    return pl.pallas_call(
        functools.partial(_flash_kernel, tile=tile),
        grid=(B, MLA_HEADS // nh),
        in_specs=[blk, blk, blk],
        out_specs=pl.BlockSpec((S, nh * V_HEAD), lambda b, hp: (b, hp)),
        out_shape=jax.ShapeDtypeStruct((T, MLA_HEADS * V_HEAD), bf16),
        scratch_shapes=[
            pltpu.VMEM((nh, S // tile, HEAD_TILE, tile), bf16),
            pltpu.VMEM((nh, 1, tile), f32),
            pltpu.VMEM((nh, 1, tile), f32),
            pltpu.VMEM((nh, HEAD_TILE, tile), f32),
        ],
        compiler_params=_cparams(("parallel", "parallel")),
        name="mla_flash",
    )(q, k, kv)


def _absorb_q_kernel(q_ref, w_ref, o_ref):
    lane = lax.broadcasted_iota(jnp.int32, q_ref.shape, 1)
    qn = jnp.where(lane < QK_NOPE, q_ref[...], jnp.zeros_like(q_ref[...]))
    o_ref[0] = _dot_nt(qn, w_ref[...]).astype(bf16)


def _absorb_q(q, wukv):
    DB = q.shape[0]
    return pl.pallas_call(
        _absorb_q_kernel,
        grid=(MLA_HEADS,),
        in_specs=[
            pl.BlockSpec((DB, HEAD_TILE), lambda h: (0, h)),
            pl.BlockSpec((KV_LORA, HEAD_TILE), lambda h: (0, h)),
        ],
        out_specs=pl.BlockSpec((1, DB, KV_LORA), lambda h: (h, 0, 0)),
        out_shape=jax.ShapeDtypeStruct((MLA_HEADS, DB, KV_LORA), bf16),
        compiler_params=_cparams(("parallel",)),
        name="mla_absorb_q",
    )(q, wukv)


def _unabsorb_o_kernel(o_ref, w_ref, out_ref):
    out_ref[...] = _dot(o_ref[0].astype(bf16), w_ref[...])


def _unabsorb_o(o_lat, wukv):
    DB = o_lat.shape[1]
    return pl.pallas_call(
        _unabsorb_o_kernel,
        grid=(MLA_HEADS,),
        in_specs=[
            pl.BlockSpec((1, DB, KV_LORA), lambda h: (h, 0, 0)),
            pl.BlockSpec((KV_LORA, HEAD_TILE), lambda h: (0, h)),
        ],
        out_specs=pl.BlockSpec((DB, HEAD_TILE), lambda h: (0, h)),
        out_shape=jax.ShapeDtypeStruct((DB, MLA_HEADS * HEAD_TILE), f32),
        compiler_params=_cparams(("parallel",)),
        name="mla_unabsorb_o",
    )(o_lat, wukv)


def _decode_kernel(pt_ref, qlat_ref, qpe_ref, cnew_ref, pnew_ref, *refs, pages_per_step):
    PP = pages_per_step
    ckv_refs = refs[:PP]
    kpe_refs = refs[PP:2 * PP]
    o_ref, m_ref, l_ref, acc_ref = refs[2 * PP:]
    g = pl.program_id(1)

    @pl.when(g == 0)
    def _():
        m_ref[...] = jnp.full(m_ref.shape, -jnp.inf, f32)
        l_ref[...] = jnp.zeros(l_ref.shape, f32)
        acc_ref[...] = jnp.zeros(acc_ref.shape, f32)

    qlat = qlat_ref[0]
    qpe = qpe_ref[0]
    cks = [ckv_refs[p][0].astype(bf16) for p in range(PP)]
    s = jnp.concatenate(
        [_dot_nt(qlat, cks[p]) + _dot(qpe, kpe_refs[p][0].astype(bf16)) for p in range(PP)],
        axis=1)
    m_prev = m_ref[...]
    m_new = jnp.maximum(m_prev, jnp.max(s, axis=-1, keepdims=True))
    p_all = jnp.exp2(s - m_new)
    alpha = jnp.exp2(m_prev - m_new)
    l_ref[...] = alpha * l_ref[...] + jnp.sum(p_all, axis=-1, keepdims=True)
    pv = _dot(p_all[:, :PAGE_SIZE].astype(bf16), cks[0])
    for p in range(1, PP):
        pv = pv + _dot(p_all[:, p * PAGE_SIZE:(p + 1) * PAGE_SIZE].astype(bf16), cks[p])
    acc_ref[...] = alpha * acc_ref[...] + pv
    m_ref[...] = m_new

    @pl.when(g == pl.num_programs(1) - 1)
    def _():
        cnew = cnew_ref[0].astype(bf16).astype(f32)
        pnew = pnew_ref[0].astype(bf16).astype(f32)
        s_self = (jnp.sum(qlat.astype(f32) * cnew, axis=-1, keepdims=True)
                  + jnp.sum(qpe.astype(f32) * pnew, axis=-1, keepdims=True))
        m_prev = m_ref[...]
        m_new = jnp.maximum(m_prev, s_self)
        p_self = jnp.exp2(s_self - m_new)
        alpha = jnp.exp2(m_prev - m_new)
        l_fin = alpha * l_ref[...] + p_self
        acc = alpha * acc_ref[...] + p_self * cnew
        o_ref[0] = acc / l_fin


def _decode_attention(page_table, qlat, qpe, ckv_new, kpe_new, pool_ckv, pool_kpe_t, pages_per_step=16):
    DB, n_pages = page_table.shape
    PP = math.gcd(n_pages, pages_per_step)
    H = MLA_HEADS

    def page_map(p):
        return lambda b, g, pt: (pt[b, g * PP + p], 0, 0)

    per_b = lambda b, g, pt: (b, 0, 0)
    in_specs = [
        pl.BlockSpec((1, H, KV_LORA), per_b),
        pl.BlockSpec((1, H, QK_ROPE), per_b),
        pl.BlockSpec((1, 1, KV_LORA), per_b),
        pl.BlockSpec((1, 1, QK_ROPE), per_b),
    ]
    in_specs += [pl.BlockSpec((1, PAGE_SIZE, KV_LORA), page_map(p)) for p in range(PP)]
    in_specs += [pl.BlockSpec((1, QK_ROPE, PAGE_SIZE), page_map(p)) for p in range(PP)]
    grid_spec = pltpu.PrefetchScalarGridSpec(
        num_scalar_prefetch=1,
        grid=(DB, n_pages // PP),
        in_specs=in_specs,
        out_specs=pl.BlockSpec((1, H, KV_LORA), per_b),
        scratch_shapes=[
            pltpu.VMEM((H, 1), f32),
            pltpu.VMEM((H, 1), f32),
            pltpu.VMEM((H, KV_LORA), f32),
        ],
    )
    return pl.pallas_call(
        functools.partial(_decode_kernel, pages_per_step=PP),
        grid_spec=grid_spec,
        out_shape=jax.ShapeDtypeStruct((DB, H, KV_LORA), f32),
        compiler_params=_cparams(("parallel", "arbitrary")),
        name="mla_decode",
    )(page_table, qlat, qpe, ckv_new, kpe_new, *([pool_ckv] * PP), *([pool_kpe_t] * PP))


def _proj_ffn_kernel(x_ref, o_ref, wo_ref, gffn_ref, wg_ref, wu_ref, wout_ref, gfin_ref,
                     out_ref, acc_ref, h_ref, *, final_norm):
    j = pl.program_id(1)

    @pl.when(j == 0)
    def _():
        x1 = x_ref[...] + _dot(o_ref[...], wo_ref[...])
        acc_ref[...] = x1
        h_ref[...] = _rms(x1, gffn_ref[...]).astype(bf16)

    h = h_ref[...]
    gate = _dot(h, wg_ref[...])
    up = _dot(h, wu_ref[...])
    act = (gate * jax.nn.sigmoid(gate) * up).astype(bf16)
    acc_ref[...] += _dot(act, wout_ref[...])

    @pl.when(j == pl.num_programs(1) - 1)
    def _():
        y = acc_ref[...]
        if final_norm:
            y = _rms(y, gfin_ref[...])
        out_ref[...] = y


def _proj_ffn(x, o, wo, gffn, w_in, w_out, gfin, final_norm, tm, th):
    T = x.shape[0]
    nh = FFN_HIDDEN // th
    row = lambda i, j: (i, 0)
    const = lambda i, j: (0, 0)
    return pl.pallas_call(
        functools.partial(_proj_ffn_kernel, final_norm=final_norm),
        grid=(T // tm, nh),
        in_specs=[
            pl.BlockSpec((tm, D_MODEL), row),
            pl.BlockSpec((tm, D_MODEL), row),
            pl.BlockSpec((D_MODEL, D_MODEL), const),
            pl.BlockSpec((1, D_MODEL), const),
            pl.BlockSpec((D_MODEL, th), lambda i, j: (0, j)),
            pl.BlockSpec((D_MODEL, th), lambda i, j: (0, j + nh)),
            pl.BlockSpec((th, D_MODEL), lambda i, j: (j, 0)),
            pl.BlockSpec((1, D_MODEL), const),
        ],
        out_specs=pl.BlockSpec((tm, D_MODEL), row),
        out_shape=jax.ShapeDtypeStruct((T, D_MODEL), f32),
        scratch_shapes=[pltpu.VMEM((tm, D_MODEL), f32), pltpu.VMEM((tm, D_MODEL), bf16)],
        compiler_params=_cparams(("parallel", "arbitrary")),
        name="proj_ffn",
    )(x, o, wo, gffn, w_in, w_in, w_out, gfin)


def _seg_sum(x, ones_blk):
    parts = []
    for gidx in range(x.shape[1] // LANES):
        xg = x[:, gidx * LANES:(gidx + 1) * LANES]
        hi = xg.astype(bf16)
        lo = (xg - hi.astype(f32)).astype(bf16)
        parts.append(_dot(hi, ones_blk) + _dot(lo, ones_blk))
    return jnp.concatenate(parts, axis=1)


def _rwkv_proj_kernel(x_ref, prev_ref, gmix_ref, mix_ref, wrkv_ref, w0_ref, w1_ref, w2_ref,
                      a0_ref, a1_ref, a2_ref, g1_ref, g2_ref, kk_ref, ka_ref, rk_ref, ones_ref,
                      r_out, k_out, v_out, kn_out, b_out, ld_out, g_out, bonus_out, h_out,
                      *, seq_mode, tiles_per_seq):
    tm = x_ref.shape[0]
    gm = gmix_ref[...]
    h = _rms(x_ref[...], gm)
    if seq_mode:
        hp_row = _rms(prev_ref[...], gm)[7:8, :]
        is_start = (pl.program_id(0) % tiles_per_seq) == 0
        hp_row = jnp.where(is_start, jnp.zeros_like(hp_row), hp_row)
        rowid = lax.broadcasted_iota(jnp.int32, (tm, 1), 0)
        hprev = jnp.where(rowid == 0, hp_row, pltpu.roll(h, 1, 0))
        h_out[0] = h[tm - 8:, :]
    else:
        hprev = prev_ref[...]
        h_out[...] = h
    xx = hprev - h
    mix = mix_ref[...]
    xr, xw, xk, xv, xa, xg = ((h + xx * mix[n:n + 1]).astype(bf16) for n in range(6))
    r = _dot(xr, wrkv_ref[0])
    k = _dot(xk, wrkv_ref[1])
    v = _dot(xv, wrkv_ref[2])
    wl = w0_ref[...] + _dot(jnp.tanh(_dot(xw, w1_ref[...])).astype(bf16), w2_ref[...])
    z = -wl
    w_log = -(jnp.maximum(z, 0.0) + jnp.log(1.0 + jnp.exp(-jnp.abs(z)))) - 0.5
    ld_out[...] = -jnp.exp(w_log)
    a = jax.nn.sigmoid(a0_ref[...] + _dot(_dot(xa, a1_ref[...]).astype(bf16), a2_ref[...]))
    g = _dot(jax.nn.sigmoid(_dot(xg, g1_ref[...])).astype(bf16), g2_ref[...])
    ones_blk = ones_ref[...]
    kk = k * kk_ref[...]
    kk = kk / jnp.maximum(jnp.sqrt(_seg_sum(kk * kk, ones_blk)), 1e-12)
    k = k * (1.0 + (a - 1.0) * ka_ref[...])
    bonus = _seg_sum(r * k * rk_ref[...], ones_blk) * v
    r_out[...] = r.astype(r_out.dtype)
    k_out[...] = k.astype(k_out.dtype)
    v_out[...] = v.astype(v_out.dtype)
    kn_out[...] = kk.astype(kn_out.dtype)
    b_out[...] = (kk * a).astype(b_out.dtype)
    g_out[...] = g.astype(g_out.dtype)
    bonus_out[...] = bonus.astype(bonus_out.dtype)


def _rwkv_proj(x, prev, p, seq_len, tm, vec_dtype):
    T = x.shape[0]
    seq_mode = seq_len > 1
    tiles_per_seq = max(seq_len // tm, 1)
    row = lambda i: (i, 0)
    const = lambda i: (0, 0)
    const3 = lambda i: (0, 0, 0)
    if seq_mode:
        prev_spec = pl.BlockSpec((8, D_MODEL), lambda i: (jnp.maximum(i * (tm // 8) - 1, 0), 0))
        h_spec = pl.BlockSpec((1, 8, D_MODEL), lambda i: (i, 0, 0))
        h_shape = jax.ShapeDtypeStruct((T // tm, 8, D_MODEL), f32)
    else:
        prev_spec = pl.BlockSpec((tm, D_MODEL), row)
        h_spec = pl.BlockSpec((tm, D_MODEL), row)
        h_shape = jax.ShapeDtypeStruct((T, D_MODEL), f32)
    vec = lambda dt: jax.ShapeDtypeStruct((T, D_MODEL), dt)
    tile = pl.BlockSpec((tm, D_MODEL), row)
    full = lambda arr: pl.BlockSpec(arr.shape, const3 if arr.ndim == 3 else const)
    weights = [p["gmix"], p["mix"], p["w_rkv"], p["w0"], p["w1"], p["w2"], p["a0"], p["a1"], p["a2"],
               p["g1"], p["g2"], p["k_k"], p["k_a"], p["r_k"], p["ones_blk"]]
    return pl.pallas_call(
        functools.partial(_rwkv_proj_kernel, seq_mode=seq_mode, tiles_per_seq=tiles_per_seq),
        grid=(T // tm,),
        in_specs=[tile, prev_spec] + [full(w) for w in weights],
        out_specs=[tile] * 8 + [h_spec],
        out_shape=[vec(vec_dtype)] * 5 + [vec(f32), vec(bf16), vec(bf16), h_shape],
        compiler_params=_cparams(("parallel",)),
        name="rwkv_proj",
    )(x, prev, *weights)


def _wkv_chunk_kernel(r_ref, k_ref, v_ref, kn_ref, b_ref, ld_ref, y_ref, st_ref, m_ref, *, chunks):
    C = WKV_CHUNK
    G = WKV_GROUP
    W = GROUP_W
    c_idx = pl.program_id(1)

    @pl.when(c_idx == 0)
    def _():
        m_ref[...] = jnp.zeros(m_ref.shape, f32)

    rr = lax.broadcasted_iota(jnp.int32, (W, W), 0)
    cc = lax.broadcasted_iota(jnp.int32, (W, W), 1)
    bd_mask = (rr // RWKV_HEAD) == (cc // RWKV_HEAD)
    t_idx = lax.broadcasted_iota(jnp.int32, (C, W), 0)
    s_idx = lax.broadcasted_iota(jnp.int32, (C, W), 1) % C
    strict = s_idx < t_idx
    incl = s_idx <= t_idx
    eye_cat = (s_idx == t_idx).astype(f32)
    tri = (lax.broadcasted_iota(jnp.int32, (C, C), 1) <= lax.broadcasted_iota(jnp.int32, (C, C), 0)).astype(f32)
    ones_c = jnp.ones((C, LANES), f32)

    lane_lo = lax.broadcasted_iota(jnp.int32, (C, LANES), 1) < RWKV_HEAD
    zero_tile = jnp.zeros((C, LANES), bf16)

    def bd(y):
        yb = y.astype(bf16)
        rows = []
        for hh in range(G):
            lt = hh // 2
            t = yb[:, lt * LANES:(lt + 1) * LANES]
            keep = jnp.where(lane_lo, t, zero_tile) if hh % 2 == 0 else jnp.where(lane_lo, zero_tile, t)
            tiles = [zero_tile] * (W // LANES)
            tiles[lt] = keep
            rows.append(jnp.concatenate(tiles, axis=1))
        return jnp.concatenate(rows, axis=0)

    def bdmm(x, y):
        return _dot(x.astype(bf16), bd(y))

    NG = r_ref.shape[1] // W
    streams = [(ci, gi) for ci in range(chunks) for gi in range(NG)]
    ns = range(len(streams))

    def load(ref, s):
        ci, gi = streams[s]
        return ref[ci * C:(ci + 1) * C, gi * W:(gi + 1) * W]

    hi = lax.Precision.HIGHEST
    ld = [load(ld_ref, s) for s in ns]
    cs = [jnp.dot(tri, ld[s], precision=hi, preferred_element_type=f32) for s in ns]
    tot_col = [lax.dot_general(ld[s], ones_c, (((0,), (0,)), ((), ())), precision=hi,
                               preferred_element_type=f32) for s in ns]
    r = [load(r_ref, s).astype(f32) for s in ns]
    k = [load(k_ref, s).astype(f32) for s in ns]
    v = [load(v_ref, s).astype(f32) for s in ns]
    kn = [load(kn_ref, s).astype(f32) for s in ns]
    b = [load(b_ref, s).astype(f32) for s in ns]
    p_inv = [jnp.exp(-cs[s]) for s in ns]
    p_rest = [jnp.exp(cs[s][C - 1:C, :] - cs[s]) for s in ns]
    a_t = [-kn[s] * jnp.exp(cs[s] - ld[s]) for s in ns]
    r_t = [r[s] * jnp.exp(cs[s]) for s in ns]
    lhs = [jnp.concatenate([a_t[s], r_t[s]], axis=0).astype(bf16) for s in ns]
    s_b = [_dot_nt(lhs[s], bd(b[s] * p_inv[s])) for s in ns]
    s_k = [_dot_nt(lhs[s], bd(k[s] * p_inv[s])) for s in ns]
    l_ab = [jnp.where(strict, s_b[s][:C], 0.0) for s in ns]
    l_ak = [jnp.where(strict, s_k[s][:C], 0.0) for s in ns]
    a_rb = [jnp.where(incl, s_b[s][C:], 0.0) for s in ns]
    a_rk = [jnp.where(incl, s_k[s][C:], 0.0) for s in ns]
    t_inv = [eye_cat + l_ab[s] for s in ns]
    pw = l_ab
    step = 1
    while 2 * step < C:
        pw = [bdmm(pw[s], pw[s]) for s in ns]
        t_inv = [t_inv[s] + bdmm(t_inv[s], pw[s]) for s in ns]
        step *= 2
    w1 = [bdmm(t_inv[s], a_t[s]) for s in ns]
    qv = [bdmm(l_ak[s], v[s]) for s in ns]
    w2 = [bdmm(t_inv[s], qv[s]) for s in ns]
    y_v = [bdmm(a_rk[s], v[s]) for s in ns]
    st_lhs = [jnp.concatenate([w1[s], r_t[s]], axis=0).astype(bf16) for s in ns]
    bk_rest = [jnp.concatenate([b[s] * p_rest[s], k[s] * p_rest[s]], axis=0).astype(bf16) for s in ns]
    decay_col = [jnp.exp(jnp.concatenate([tot_col[s]] * (W // LANES), axis=1)) for s in ns]

    m = [m_ref[gi] for gi in range(NG)]
    for ci in range(chunks):
        ss = [ci * NG + gi for gi in range(NG)]
        from_state = [_dot(st_lhs[s], m[gi].astype(bf16)) for gi, s in enumerate(ss)]
        u = [from_state[gi][:C] + w2[s] for gi, s in enumerate(ss)]
        y_u = [bdmm(a_rb[s], u[gi]) for gi, s in enumerate(ss)]
        upd = [_dot_tn(bk_rest[s], jnp.concatenate([u[gi], v[s]], axis=0).astype(bf16)) for gi, s in enumerate(ss)]
        for gi, s in enumerate(ss):
            y_ref[ci * C:(ci + 1) * C, gi * W:(gi + 1) * W] = from_state[gi][C:] + y_u[gi] + y_v[s]
        m = [m[gi] * decay_col[s] + jnp.where(bd_mask, upd[gi], 0.0) for gi, s in enumerate(ss)]
    for gi in range(NG):
        m_ref[gi] = m[gi]
        acc = m[gi][:RWKV_HEAD]
        for hh in range(1, G):
            acc = acc + m[gi][hh * RWKV_HEAD:(hh + 1) * RWKV_HEAD]
        st_ref[0, :, gi * W:(gi + 1) * W] = acc


def _wkv_chunked(r, k, v, kn, b, ld, B, S):
    T = B * S
    chunks = WKV_CHUNKS_PER_STEP if S % (WKV_CHUNK * WKV_CHUNKS_PER_STEP) == 0 else 1
    rows = WKV_CHUNK * chunks
    nc = S // rows
    tile = pl.BlockSpec((rows, D_MODEL), lambda bi, ci: (bi * nc + ci, 0))
    return pl.pallas_call(
        functools.partial(_wkv_chunk_kernel, chunks=chunks),
        grid=(B, nc),
        in_specs=[tile] * 6,
        out_specs=[tile, pl.BlockSpec((1, RWKV_HEAD, D_MODEL), lambda bi, ci: (bi, 0, 0))],
        out_shape=[jax.ShapeDtypeStruct((T, D_MODEL), f32),
                   jax.ShapeDtypeStruct((B, RWKV_HEAD, D_MODEL), f32)],
        scratch_shapes=[pltpu.VMEM((D_MODEL // GROUP_W, GROUP_W, GROUP_W), f32)],
        compiler_params=_cparams(("parallel", "arbitrary")),
        name="wkv_chunked",
    )(r, k, v, kn, b, ld)


def _wkv_step_kernel(s_ref, r_ref, k_ref, v_ref, kn_ref, b_ref, ld_ref, snew_ref, y_ref):
    N = RWKV_HEAD
    eye = lax.broadcasted_iota(jnp.int32, (N, N), 0) == lax.broadcasted_iota(jnp.int32, (N, N), 1)
    ys = []
    for hh in range(RWKV_HEADS):
        sl = slice(hh * N, (hh + 1) * N)
        st = s_ref[0, hh].astype(f32)
        r = r_ref[0, :, sl]
        k = k_ref[0, :, sl]
        v = v_ref[0, :, sl]
        a = -kn_ref[0, :, sl]
        b = b_ref[0, :, sl]
        w = jnp.exp(ld_ref[0, :, sl])
        sa = jnp.sum(st * a, axis=1, keepdims=True)
        v_col = jnp.sum(jnp.where(eye, jnp.broadcast_to(v, (N, N)), 0.0), axis=1, keepdims=True)
        st = st * w + sa * b + v_col * k
        snew_ref[0, hh] = st.astype(snew_ref.dtype)
        y_col = jnp.sum(st * r, axis=1, keepdims=True)
        ys.append(jnp.sum(jnp.where(eye, jnp.broadcast_to(y_col, (N, N)), 0.0), axis=0, keepdims=True))
    y_ref[0] = jnp.concatenate(ys, axis=1)


def _wkv_step(state, r, k, v, kn, b, ld):
    DB = state.shape[0]
    vec = pl.BlockSpec((1, 1, D_MODEL), lambda i: (i, 0, 0))
    st = pl.BlockSpec((1, RWKV_HEADS, RWKV_HEAD, RWKV_HEAD), lambda i: (i, 0, 0, 0))
    r3 = lambda t: t.reshape(DB, 1, D_MODEL)
    snew, y = pl.pallas_call(
        _wkv_step_kernel,
        grid=(DB,),
        in_specs=[st] + [vec] * 6,
        out_specs=[st, vec],
        out_shape=[jax.ShapeDtypeStruct(state.shape, state.dtype),
                   jax.ShapeDtypeStruct((DB, 1, D_MODEL), f32)],
        compiler_params=_cparams(("parallel",)),
        name="wkv_step",
    )(state, r3(r), r3(k), r3(v), r3(kn), r3(b), r3(ld))
    return snew, y.reshape(DB, D_MODEL)


def _rwkv_post_kernel(y_ref, bonus_ref, g_ref, lnw_ref, lnb_ref, ones_ref, o_ref):
    ones_blk = ones_ref[...]
    y = y_ref[...]
    inv_n = 1.0 / RWKV_HEAD
    mu = _seg_sum(y, ones_blk) * inv_n
    d = y - mu
    var = _seg_sum(d * d, ones_blk) * inv_n
    yn = d * lax.rsqrt(var + GN_EPS) * lnw_ref[...] + lnb_ref[...]
    o_ref[...] = ((yn + bonus_ref[...].astype(f32)) * g_ref[...].astype(f32)).astype(bf16)


def _rwkv_post(y, bonus, g, lnw, lnb, ones_blk, tm):
    T = y.shape[0]
    row = lambda i: (i, 0)
    const = lambda i: (0, 0)
    tile = pl.BlockSpec((tm, D_MODEL), row)
    return pl.pallas_call(
        _rwkv_post_kernel,
        grid=(T // tm,),
        in_specs=[tile, tile, tile, pl.BlockSpec((1, D_MODEL), const), pl.BlockSpec((1, D_MODEL), const),
                  pl.BlockSpec((LANES, LANES), const)],
        out_specs=tile,
        out_shape=jax.ShapeDtypeStruct((T, D_MODEL), bf16),
        compiler_params=_cparams(("parallel",)),
        name="rwkv_post",
    )(y, bonus, g, lnw, lnb, ones_blk)


def _rope_tables(pos):
    half = QK_ROPE // 2
    inv = 1.0 / (ROPE_THETA ** (jnp.arange(0, QK_ROPE, 2, dtype=f32) / QK_ROPE))
    ang = pos[:, None] * inv[None, :]
    cos, sin = jnp.cos(ang), jnp.sin(ang)
    n = pos.shape[0]
    ones = jnp.ones((n, QK_NOPE), f32)
    z = lambda w: jnp.zeros((n, w), f32)
    tab_c = jnp.concatenate([ones, cos, cos, z(HEAD_TILE - QK_NOPE - QK_ROPE)], axis=1)
    tab_s1 = jnp.concatenate([z(QK_NOPE), -sin, z(HEAD_TILE - QK_NOPE - half)], axis=1)
    tab_s2 = jnp.concatenate([z(QK_NOPE + half), sin, z(HEAD_TILE - QK_NOPE - QK_ROPE)], axis=1)
    return tab_c, tab_s1, tab_s2


def _pick_tile(n, pref):
    t = min(n, pref)
    while n % t:
        t //= 2
    return t


def kernel(x_prompt, x_sample, cache_ckv, cache_kpe, state_wkv, state_shift, page_table,
           norm_mix, norm_ffn, norm_final,
           mla_w_a, mla_q_norm, mla_kv_norm, mla_w_uq, mla_w_ukv, mla_w_o,
           rw_mix, rw_w_rkv, rw_w0, rw_w1, rw_w2, rw_a0, rw_a1, rw_a2, rw_g1, rw_g2,
           rw_k_k, rw_k_a, rw_r_k, rw_ln_w, rw_ln_b, rw_w_o,
           ffn_w_in, ffn_w_out):
    B, S, D = x_prompt.shape
    DB, DS, _ = x_sample.shape
    assert D == D_MODEL and DS == 1 and S % WKV_CHUNK == 0
    n_pages = page_table.shape[1]
    past_len = n_pages * PAGE_SIZE
    H = MLA_HEADS
    row = lambda t: t.reshape(1, -1)
    xp = x_prompt.reshape(B * S, D)
    xs = x_sample.reshape(DB, D)

    w_a = mla_w_a[0]
    pad_pe = jnp.zeros((D, HEAD_TILE), f32).at[:, QK_NOPE:QK_NOPE + QK_ROPE].set(w_a[:, Q_LORA + KV_LORA:])
    wa_ext = jnp.concatenate([w_a[:, :Q_LORA + KV_LORA], pad_pe], axis=1).astype(bf16)
    wuq = (mla_w_uq[0] * (MLA_SCALE * LOG2_E)).reshape(Q_LORA, H, QK_NOPE + QK_ROPE)
    wuq = jnp.pad(wuq, ((0, 0), (0, 0), (0, HEAD_TILE - QK_NOPE - QK_ROPE))).reshape(Q_LORA, H * HEAD_TILE).astype(bf16)
    wukv = mla_w_ukv[0].reshape(KV_LORA, H * HEAD_TILE).astype(bf16)
    mla_wo = mla_w_o[0].astype(bf16)
    qn, kvn = row(mla_q_norm[0]), row(mla_kv_norm[0])
    ffn_in = ffn_w_in.astype(bf16)
    ffn_out = ffn_w_out.astype(bf16)
    hh = jnp.arange(LANES) // RWKV_HEAD
    ones_blk = (hh[:, None] == hh[None, :]).astype(bf16)
    rwp = dict(
        gmix=row(norm_mix[1]), mix=jnp.pad(rw_mix[0], ((0, 2), (0, 0))), w_rkv=rw_w_rkv[0].astype(bf16),
        w0=row(rw_w0[0]), w1=rw_w1[0].astype(bf16), w2=rw_w2[0].astype(bf16),
        a0=row(rw_a0[0]), a1=rw_a1[0].astype(bf16), a2=rw_a2[0].astype(bf16),
        g1=rw_g1[0].astype(bf16), g2=rw_g2[0].astype(bf16),
        k_k=row(rw_k_k[0]), k_a=row(rw_k_a[0]), r_k=row(rw_r_k[0]), ones_blk=ones_blk)
    rw_wo = rw_w_o[0].astype(bf16)
    lnw, lnb = row(rw_ln_w[0]), row(rw_ln_b[0])
    gfin = row(norm_final)

    tm_p = _pick_tile(B * S, 512)
    tm_s = _pick_tile(DB, 128)
    th = FFN_HIDDEN // 2

    tabs_p = _rope_tables(jnp.arange(S, dtype=f32))
    tabs_s = _rope_tables(jnp.full((tm_s,), past_len, f32))
    q_p, k_p, kv_p, ckv_p, kpe_p = _mla_proj(xp, row(norm_mix[0]), wa_ext, qn, kvn, wuq, wukv, tabs_p,
                                             _pick_tile(S, 512))
    o_p = _flash_attention(q_p, k_p, kv_p, B, S)
    xp = _proj_ffn(xp, o_p, mla_wo, row(norm_ffn[0]), ffn_in[0], ffn_out[0], gfin, False, tm_p, th)

    q_s, _, _, ckv_s, kpe_s = _mla_proj(xs, row(norm_mix[0]), wa_ext, qn, kvn, wuq, wukv, tabs_s, tm_s)
    qlat = jnp.swapaxes(_absorb_q(q_s, wukv), 0, 1)
    qpe = q_s.reshape(DB, H, HEAD_TILE)[:, :, QK_NOPE:QK_NOPE + QK_ROPE]
    o_lat = _decode_attention(page_table, qlat, qpe, ckv_s.reshape(DB, 1, KV_LORA), kpe_s.reshape(DB, 1, QK_ROPE),
                              cache_ckv.reshape(-1, PAGE_SIZE, KV_LORA),
                              jnp.swapaxes(cache_kpe.reshape(-1, PAGE_SIZE, QK_ROPE), 1, 2))
    o_s = _unabsorb_o(jnp.swapaxes(o_lat, 0, 1), wukv)
    o_s = o_s.reshape(DB, H, HEAD_TILE)[:, :, QK_NOPE:].reshape(DB, H * V_HEAD).astype(bf16)
    xs = _proj_ffn(xs, o_s, mla_wo, row(norm_ffn[0]), ffn_in[0], ffn_out[0], gfin, False, tm_s, th)

    tm_r = _pick_tile(S, 256)
    r, k, v, kn, b, ld, g, bonus, hl = _rwkv_proj(xp, xp, rwp, S, tm_r, bf16)
    shift_p = hl.reshape(B, S // tm_r, 8, D)[:, -1, 7, :]
    y, st = _wkv_chunked(r, k, v, kn, b, ld, B, S)
    wkv_p = jnp.transpose(st.reshape(B, RWKV_HEAD, RWKV_HEADS, RWKV_HEAD), (0, 2, 3, 1))
    yo = _rwkv_post(y, bonus, g, lnw, lnb, ones_blk, tm_p)
    y_prompt = _proj_ffn(xp, yo, rw_wo, row(norm_ffn[1]), ffn_in[1], ffn_out[1], gfin, True, tm_p, th)

    r, k, v, kn, b, ld, g, bonus, shift_s = _rwkv_proj(xs, state_shift[0], rwp, 1, tm_s, f32)
    wkv_s, y = _wkv_step(state_wkv[0], r, k, v, kn, b, ld)
    yo = _rwkv_post(y, bonus, g, lnw, lnb, ones_blk, tm_s)
    y_sample = _proj_ffn(xs, yo, rw_wo, row(norm_ffn[1]), ffn_in[1], ffn_out[1], gfin, True, tm_s, th)

    return (y_prompt.reshape(B, S, D), y_sample.reshape(DB, DS, D),
            ckv_p.reshape(1, B, S, KV_LORA), kpe_p.reshape(1, B, S, QK_ROPE),
            ckv_s.reshape(1, DB, DS, KV_LORA), kpe_s.reshape(1, DB, DS, QK_ROPE),
            wkv_p[None].astype(x_prompt.dtype), shift_p[None],
            wkv_s[None], shift_s[None])
```

```python
import functools
import math

import jax
import jax.numpy as jnp
from jax import lax
from jax.experimental import pallas as pl
from jax.experimental.pallas import tpu as pltpu

f32 = jnp.float32
bf16 = jnp.bfloat16

D_MODEL = 1024
MLA_HEADS = 16
QK_NOPE = 64
QK_ROPE = 32
V_HEAD = 64
Q_LORA = 512
KV_LORA = 256
ROPE_THETA = 10000.0
MLA_SCALE = 1.0 / math.sqrt(QK_NOPE + QK_ROPE)
LOG2_E = math.log2(math.e)
RWKV_HEAD = 64
RWKV_HEADS = D_MODEL // RWKV_HEAD
FFN_HIDDEN = 2816
NORM_EPS = 1e-6
GN_EPS = 64e-5
PAGE_SIZE = 128

LANES = 128
HEAD_TILE = 128
WKV_CHUNK = 64
WKV_GROUP = 4
WKV_CHUNKS_PER_STEP = 2
GROUP_W = WKV_GROUP * RWKV_HEAD
VMEM_LIMIT = 56 * 1024 * 1024


def _cparams(sem):
    return pltpu.CompilerParams(dimension_semantics=sem, vmem_limit_bytes=VMEM_LIMIT)


def _rms(x, g):
    return x * lax.rsqrt(jnp.mean(x * x, axis=-1, keepdims=True) + NORM_EPS) * g


def _dot(a, b):
    return jnp.dot(a, b, preferred_element_type=f32)


def _dot_nt(a, b):
    return lax.dot_general(a, b, (((1,), (1,)), ((), ())), preferred_element_type=f32)


def _dot_tn(a, b):
    return lax.dot_general(a, b, (((0,), (0,)), ((), ())), preferred_element_type=f32)


def _mla_proj_kernel(x_ref, gmix_ref, wa_ref, qn_ref, kvn_ref, wuq_ref, wukv_ref,
                     c_ref, s1_ref, s2_ref, q_ref, k_ref, kv_ref, ckv_ref, kpe_ref):
    tm = x_ref.shape[0]
    h = _rms(x_ref[...], gmix_ref[...]).astype(bf16)
    a = _dot(h, wa_ref[...])
    cq = _rms(a[:, :Q_LORA], qn_ref[...]).astype(bf16)
    ckv = _rms(a[:, Q_LORA:Q_LORA + KV_LORA], kvn_ref[...])
    ckv_ref[...] = ckv
    cos = c_ref[...]
    sin_lo = s1_ref[...]
    sin_hi = s2_ref[...]

    def rope(t):
        return t * cos + pltpu.roll(t, LANES - QK_ROPE // 2, 1) * sin_lo + pltpu.roll(t, QK_ROPE // 2, 1) * sin_hi

    kpe_t = rope(a[:, Q_LORA + KV_LORA:])
    kpe_ref[...] = kpe_t[:, QK_NOPE:QK_NOPE + QK_ROPE]
    kv = _dot(ckv.astype(bf16), wukv_ref[...])
    kv_ref[...] = kv.astype(bf16)
    q = _dot(cq, wuq_ref[...])
    lane = lax.broadcasted_iota(jnp.int32, (tm, HEAD_TILE), 1)
    for hh in range(MLA_HEADS):
        sl = slice(hh * HEAD_TILE, (hh + 1) * HEAD_TILE)
        q_ref[:, sl] = rope(q[:, sl]).astype(bf16)
        k_ref[:, sl] = jnp.where(lane < QK_NOPE, kv[:, sl], kpe_t).astype(bf16)


def _mla_proj(x, gmix, wa, qn, kvn, wuq, wukv, tabs, tm):
    T = x.shape[0]
    tab_c, tab_s1, tab_s2 = tabs
    nt = tab_c.shape[0] // tm
    HW = MLA_HEADS * HEAD_TILE
    const = lambda i: (0, 0)
    row = lambda i: (i, 0)
    tabmap = lambda i: (i % nt, 0)
    return pl.pallas_call(
        _mla_proj_kernel,
        grid=(T // tm,),
        in_specs=[
            pl.BlockSpec((tm, D_MODEL), row),
            pl.BlockSpec((1, D_MODEL), const),
            pl.BlockSpec(wa.shape, const),
            pl.BlockSpec((1, Q_LORA), const),
            pl.BlockSpec((1, KV_LORA), const),
            pl.BlockSpec(wuq.shape, const),
            pl.BlockSpec(wukv.shape, const),
            pl.BlockSpec((tm, HEAD_TILE), tabmap),
            pl.BlockSpec((tm, HEAD_TILE), tabmap),
            pl.BlockSpec((tm, HEAD_TILE), tabmap),
        ],
        out_specs=[
            pl.BlockSpec((tm, HW), row),
            pl.BlockSpec((tm, HW), row),
            pl.BlockSpec((tm, HW), row),
            pl.BlockSpec((tm, KV_LORA), row),
            pl.BlockSpec((tm, QK_ROPE), row),
        ],
        out_shape=[
            jax.ShapeDtypeStruct((T, HW), bf16),
            jax.ShapeDtypeStruct((T, HW), bf16),
            jax.ShapeDtypeStruct((T, HW), bf16),
            jax.ShapeDtypeStruct((T, KV_LORA), f32),
            jax.ShapeDtypeStruct((T, QK_ROPE), f32),
        ],
        compiler_params=_cparams(("parallel",)),
        name="mla_proj",
    )(x, gmix, wa, qn, kvn, wuq, wukv, tab_c, tab_s1, tab_s2)


def _flash_kernel(q_ref, k_ref, kv_ref, o_ref, vt_ref, m_ref, l_ref, acc_ref, *, tile):
    S = q_ref.shape[0]
    nt = S // tile
    krow = lax.broadcasted_iota(jnp.int32, (tile, tile), 0)
    qcol = lax.broadcasted_iota(jnp.int32, (tile, tile), 1)
    lane = lax.broadcasted_iota(jnp.int32, (tile, HEAD_TILE), 1)
    heads = range(q_ref.shape[1] // HEAD_TILE)
    hs = [slice(hh * HEAD_TILE, (hh + 1) * HEAD_TILE) for hh in heads]

    for hh in heads:
        for j in range(nt):
            vt_ref[hh, j] = kv_ref[j * tile:(j + 1) * tile, hs[hh]].astype(f32).T.astype(bf16)

    def q_body(qi, c):
        q0 = pl.multiple_of(qi * tile, tile)
        qs = [q_ref[pl.ds(q0, tile), hs[hh]] for hh in heads]
        for hh in heads:
            m_ref[hh] = jnp.full((1, tile), -jnp.inf, f32)
            l_ref[hh] = jnp.zeros((1, tile), f32)
            acc_ref[hh] = jnp.zeros((HEAD_TILE, tile), f32)

        def block(ki, diag):
            k0 = pl.multiple_of(ki * tile, tile)
            sts = [_dot_nt(k_ref[pl.ds(k0, tile), hs[hh]], qs[hh]) for hh in heads]
            if diag:
                sts = [jnp.where(krow <= qcol, st, -jnp.inf) for st in sts]
            m_prevs = [m_ref[hh] for hh in heads]
            m_news = [jnp.maximum(m_prevs[hh], jnp.max(sts[hh], axis=0, keepdims=True)) for hh in heads]
            ps = [jnp.exp2(sts[hh] - m_news[hh]) for hh in heads]
            alphas = [jnp.exp2(m_prevs[hh] - m_news[hh]) for hh in heads]
            pvs = [_dot(vt_ref[hh, ki], ps[hh].astype(bf16)) for hh in heads]
            for hh in heads:
                l_ref[hh] = alphas[hh] * l_ref[hh] + jnp.sum(ps[hh], axis=0, keepdims=True)
                m_ref[hh] = m_news[hh]
            for hh in heads:
                acc_ref[hh] = alphas[hh] * acc_ref[hh] + pvs[hh]

        def k_body(ki, c2):
            block(ki, False)
            return c2

        lax.fori_loop(0, qi, k_body, 0)
        block(qi, True)
        outs = [(acc_ref[hh] / l_ref[hh]).T for hh in heads]
        for pr in range(len(outs) // 2):
            o = jnp.where(lane < V_HEAD, pltpu.roll(outs[2 * pr], V_HEAD, 1), outs[2 * pr + 1])
            o_ref[pl.ds(q0, tile), pr * 2 * V_HEAD:(pr + 1) * 2 * V_HEAD] = o.astype(bf16)
        return c

    lax.fori_loop(0, nt, q_body, 0)


def _flash_attention(q, k, kv, B, S, tile=512, heads_per_step=4):
    T = B * S
    tile = min(tile, S)
    nh = heads_per_step
    blk = pl.BlockSpec((S, nh * HEAD_TILE), lambda b, hp: (b, hp))
    return pl.pallas_call(
        functools.partial(_flash_kernel, tile=tile),
        grid=(B, MLA_HEADS // nh),
        in_specs=[blk, blk, blk],
        out_specs=pl.BlockSpec((S, nh * V_HEAD), lambda b, hp: (b, hp)),
        out_shape=jax.ShapeDtypeStruct((T, MLA_HEADS * V_HEAD), bf16),
        scratch_shapes=[
            pltpu.VMEM((nh, S // tile, HEAD_TILE, tile), bf16),
            pltpu.VMEM((nh, 1, tile), f32),
            pltpu.VMEM((nh, 1, tile), f32),
            pltpu.VMEM((nh, HEAD_TILE, tile), f32),
        ],
        compiler_params=_cparams(("parallel", "parallel")),
        name="mla_flash",
    )(q, k, kv)


def _absorb_q_kernel(q_ref, w_ref, o_ref):
    lane = lax.broadcasted_iota(jnp.int32, q_ref.shape, 1)
    qn = jnp.where(lane < QK_NOPE, q_ref[...], jnp.zeros_like(q_ref[...]))
    o_ref[0] = _dot_nt(qn, w_ref[...]).astype(bf16)


def _absorb_q(q, wukv):
    DB = q.shape[0]
    return pl.pallas_call(
        _absorb_q_kernel,
        grid=(MLA_HEADS,),
        in_specs=[
            pl.BlockSpec((DB, HEAD_TILE), lambda h: (0, h)),
            pl.BlockSpec((KV_LORA, HEAD_TILE), lambda h: (0, h)),
        ],
        out_specs=pl.BlockSpec((1, DB, KV_LORA), lambda h: (h, 0, 0)),
        out_shape=jax.ShapeDtypeStruct((MLA_HEADS, DB, KV_LORA), bf16),
        compiler_params=_cparams(("parallel",)),
        name="mla_absorb_q",
    )(q, wukv)


def _unabsorb_o_kernel(o_ref, w_ref, out_ref):
    out_ref[...] = _dot(o_ref[0].astype(bf16), w_ref[...])


def _unabsorb_o(o_lat, wukv):
    DB = o_lat.shape[1]
    return pl.pallas_call(
        _unabsorb_o_kernel,
        grid=(MLA_HEADS,),
        in_specs=[
            pl.BlockSpec((1, DB, KV_LORA), lambda h: (h, 0, 0)),
            pl.BlockSpec((KV_LORA, HEAD_TILE), lambda h: (0, h)),
        ],
        out_specs=pl.BlockSpec((DB, HEAD_TILE), lambda h: (0, h)),
        out_shape=jax.ShapeDtypeStruct((DB, MLA_HEADS * HEAD_TILE), f32),
        compiler_params=_cparams(("parallel",)),
        name="mla_unabsorb_o",
    )(o_lat, wukv)


def _decode_kernel(pt_ref, qlat_ref, qpe_ref, cnew_ref, pnew_ref, ckv_hbm, kpe_hbm, o_ref,
                   ckv_buf, kpe_buf, sem, m_ref, l_ref, acc_ref, *, pages_per_step, chains):
    PP = pages_per_step
    NC = chains
    DB, n_pages = pt_ref.shape
    G = n_pages // PP
    per_chain = DB // NC
    total = per_chain * G

    def page_copy(page, slot, idx, which):
        if which == 0:
            return pltpu.make_async_copy(ckv_hbm.at[page], ckv_buf.at[slot, idx], sem.at[0, slot])
        return pltpu.make_async_copy(kpe_hbm.at[page], kpe_buf.at[slot, idx], sem.at[1, slot])

    def start_group(it, slot):
        bb = it // G
        g0 = (it % G) * PP
        for c in range(NC):
            for p in range(PP):
                page = pt_ref[c * per_chain + bb, g0 + p]
                page_copy(page, slot, c * PP + p, 0).start()
                page_copy(page, slot, c * PP + p, 1).start()

    def wait_group(slot):
        for idx in range(NC * PP):
            page_copy(0, slot, idx, 0).wait()
            page_copy(0, slot, idx, 1).wait()

    it = pl.program_id(0)
    slot = it % 2

    @pl.when(it == 0)
    def _():
        start_group(0, 0)

    @pl.when(it + 1 < total)
    def _():
        start_group(it + 1, 1 - slot)

    wait_group(slot)
    _decode_group(it // G, it % G, G, per_chain, slot, qlat_ref, qpe_ref, cnew_ref, pnew_ref, o_ref,
                  ckv_buf, kpe_buf, m_ref, l_ref, acc_ref, PP, NC)


def _decode_group(bb, g, G, per_chain, slot, qlat_ref, qpe_ref, cnew_ref, pnew_ref, o_ref,
                  ckv_buf, kpe_buf, m_ref, l_ref, acc_ref, PP, NC):
    chains = range(NC)
    bs = [c * per_chain + bb for c in chains]

    @pl.when(g == 0)
    def _():
        m_ref[...] = jnp.full(m_ref.shape, -jnp.inf, f32)
        l_ref[...] = jnp.zeros(l_ref.shape, f32)
        acc_ref[...] = jnp.zeros(acc_ref.shape, f32)

    qlat = [qlat_ref[bs[c]] for c in chains]
    qpe = [qpe_ref[bs[c]] for c in chains]
    cks = [[ckv_buf[slot, c * PP + p].astype(bf16) for p in range(PP)] for c in chains]
    s = [jnp.concatenate(
        [_dot_nt(qlat[c], cks[c][p]) + _dot(qpe[c], kpe_buf[slot, c * PP + p].astype(bf16)) for p in range(PP)],
        axis=1) for c in chains]
    m_prev = [m_ref[c] for c in chains]
    m_new = [jnp.maximum(m_prev[c], jnp.max(s[c], axis=-1, keepdims=True)) for c in chains]
    p_all = [jnp.exp2(s[c] - m_new[c]) for c in chains]
    alpha = [jnp.exp2(m_prev[c] - m_new[c]) for c in chains]
    for c in chains:
        l_ref[c] = alpha[c] * l_ref[c] + jnp.sum(p_all[c], axis=-1, keepdims=True)
        m_ref[c] = m_new[c]
    for c in chains:
        pv = _dot(p_all[c][:, :PAGE_SIZE].astype(bf16), cks[c][0])
        for p in range(1, PP):
            pv = pv + _dot(p_all[c][:, p * PAGE_SIZE:(p + 1) * PAGE_SIZE].astype(bf16), cks[c][p])
        acc_ref[c] = alpha[c] * acc_ref[c] + pv

    @pl.when(g == G - 1)
    def _():
        for c in chains:
            cnew = cnew_ref[bs[c]].astype(bf16).astype(f32)
            pnew = pnew_ref[bs[c]].astype(bf16).astype(f32)
            s_self = (jnp.sum(qlat[c].astype(f32) * cnew, axis=-1, keepdims=True)
                      + jnp.sum(qpe[c].astype(f32) * pnew, axis=-1, keepdims=True))
            m_last = m_ref[c]
            m_fin = jnp.maximum(m_last, s_self)
            p_self = jnp.exp2(s_self - m_fin)
            a_fin = jnp.exp2(m_last - m_fin)
            l_fin = a_fin * l_ref[c] + p_self
            o_ref[bs[c]] = (a_fin * acc_ref[c] + p_self * cnew) / l_fin


def _decode_attention(page_table, qlat, qpe, ckv_new, kpe_new, pool_ckv, pool_kpe_t, pages_per_step=16):
    DB, n_pages = page_table.shape
    PP = math.gcd(n_pages, pages_per_step)
    NC = 2 if DB % 2 == 0 else 1
    H = MLA_HEADS
    vmem = pl.BlockSpec(memory_space=pltpu.VMEM)
    hbm = pl.BlockSpec(memory_space=pl.ANY)
    grid_spec = pltpu.PrefetchScalarGridSpec(
        num_scalar_prefetch=1,
        grid=((DB // NC) * (n_pages // PP),),
        in_specs=[vmem, vmem, vmem, vmem, hbm, hbm],
        out_specs=vmem,
        scratch_shapes=[
            pltpu.VMEM((2, NC * PP, PAGE_SIZE, KV_LORA), pool_ckv.dtype),
            pltpu.VMEM((2, NC * PP, QK_ROPE, PAGE_SIZE), pool_kpe_t.dtype),
            pltpu.SemaphoreType.DMA((2, 2)),
            pltpu.VMEM((NC, H, 1), f32),
            pltpu.VMEM((NC, H, 1), f32),
            pltpu.VMEM((NC, H, KV_LORA), f32),
        ],
    )
    return pl.pallas_call(
        functools.partial(_decode_kernel, pages_per_step=PP, chains=NC),
        grid_spec=grid_spec,
        out_shape=jax.ShapeDtypeStruct((DB, H, KV_LORA), f32),
        compiler_params=_cparams(("arbitrary",)),
        name="mla_decode",
    )(page_table, qlat, qpe, ckv_new, kpe_new, pool_ckv, pool_kpe_t)


def _proj_ffn_kernel(x_ref, o_ref, wo_ref, gffn_ref, wg_ref, wu_ref, wout_ref, gfin_ref,
                     out_ref, acc_ref, h_ref, *, final_norm):
    j = pl.program_id(1)

    @pl.when(j == 0)
    def _():
        x1 = x_ref[...] + _dot(o_ref[...], wo_ref[...])
        acc_ref[...] = x1
        h_ref[...] = _rms(x1, gffn_ref[...]).astype(bf16)

    h = h_ref[...]
    gate = _dot(h, wg_ref[...])
    up = _dot(h, wu_ref[...])
    act = (gate * jax.nn.sigmoid(gate) * up).astype(bf16)
    acc_ref[...] += _dot(act, wout_ref[...])

    @pl.when(j == pl.num_programs(1) - 1)
    def _():
        y = acc_ref[...]
        if final_norm:
            y = _rms(y, gfin_ref[...])
        out_ref[...] = y


def _proj_ffn(x, o, wo, gffn, w_in, w_out, gfin, final_norm, tm, th):
    T = x.shape[0]
    nh = FFN_HIDDEN // th
    row = lambda i, j: (i, 0)
    const = lambda i, j: (0, 0)
    return pl.pallas_call(
        functools.partial(_proj_ffn_kernel, final_norm=final_norm),
        grid=(T // tm, nh),
        in_specs=[
            pl.BlockSpec((tm, D_MODEL), row),
            pl.BlockSpec((tm, D_MODEL), row),
            pl.BlockSpec((D_MODEL, D_MODEL), const),
            pl.BlockSpec((1, D_MODEL), const),
            pl.BlockSpec((D_MODEL, th), lambda i, j: (0, j)),
            pl.BlockSpec((D_MODEL, th), lambda i, j: (0, j + nh)),
            pl.BlockSpec((th, D_MODEL), lambda i, j: (j, 0)),
            pl.BlockSpec((1, D_MODEL), const),
        ],
        out_specs=pl.BlockSpec((tm, D_MODEL), row),
        out_shape=jax.ShapeDtypeStruct((T, D_MODEL), f32),
        scratch_shapes=[pltpu.VMEM((tm, D_MODEL), f32), pltpu.VMEM((tm, D_MODEL), bf16)],
        compiler_params=_cparams(("parallel", "arbitrary")),
        name="proj_ffn",
    )(x, o, wo, gffn, w_in, w_in, w_out, gfin)


def _seg_sum(x, ones_blk):
    parts = [_dot(x[:, gidx * LANES:(gidx + 1) * LANES].astype(bf16), ones_blk)
             for gidx in range(x.shape[1] // LANES)]
    return jnp.concatenate(parts, axis=1)


def _rwkv_proj_kernel(x_ref, prev_ref, gmix_ref, mix_ref, wrkv_ref, w0_ref, w1_ref, w2_ref,
                      a0_ref, a1_ref, a2_ref, g1_ref, g2_ref, kk_ref, ka_ref, rk_ref, ones_ref,
                      r_out, k_out, v_out, kn_out, b_out, ld_out, g_out, bonus_out, h_out,
                      *, seq_mode, tiles_per_seq):
    tm = x_ref.shape[0]
    gm = gmix_ref[...]
    h = _rms(x_ref[...], gm)
    if seq_mode:
        hp_row = _rms(prev_ref[...], gm)[7:8, :]
        is_start = (pl.program_id(0) % tiles_per_seq) == 0
        hp_row = jnp.where(is_start, jnp.zeros_like(hp_row), hp_row)
        rowid = lax.broadcasted_iota(jnp.int32, (tm, 1), 0)
        hprev = jnp.where(rowid == 0, hp_row, pltpu.roll(h, 1, 0))
        h_out[0] = h[tm - 8:, :]
    else:
        hprev = prev_ref[...]
        h_out[...] = h
    xx = hprev - h
    mix = mix_ref[...]
    xr, xw, xk, xv, xa, xg = ((h + xx * mix[n:n + 1]).astype(bf16) for n in range(6))
    r = _dot(xr, wrkv_ref[0])
    k = _dot(xk, wrkv_ref[1])
    v = _dot(xv, wrkv_ref[2])
    wl = w0_ref[...] + _dot(jnp.tanh(_dot(xw, w1_ref[...])).astype(bf16), w2_ref[...])
    z = -wl
    w_log = -(jnp.maximum(z, 0.0) + jnp.log(1.0 + jnp.exp(-jnp.abs(z)))) - 0.5
    ld_out[...] = -jnp.exp(w_log)
    a = jax.nn.sigmoid(a0_ref[...] + _dot(_dot(xa, a1_ref[...]).astype(bf16), a2_ref[...]))
    g = _dot(jax.nn.sigmoid(_dot(xg, g1_ref[...])).astype(bf16), g2_ref[...])
    ones_blk = ones_ref[...]
    kk = k * kk_ref[...]
    kk = kk / jnp.maximum(jnp.sqrt(_seg_sum(kk * kk, ones_blk)), 1e-12)
    k = k * (1.0 + (a - 1.0) * ka_ref[...])
    bonus = _seg_sum(r * k * rk_ref[...], ones_blk) * v
    r_out[...] = r.astype(r_out.dtype)
    k_out[...] = k.astype(k_out.dtype)
    v_out[...] = v.astype(v_out.dtype)
    kn_out[...] = kk.astype(kn_out.dtype)
    b_out[...] = (kk * a).astype(b_out.dtype)
    g_out[...] = g.astype(g_out.dtype)
    bonus_out[...] = bonus.astype(bonus_out.dtype)


def _rwkv_proj(x, prev, p, seq_len, tm, vec_dtype):
    T = x.shape[0]
    seq_mode = seq_len > 1
    tiles_per_seq = max(seq_len // tm, 1)
    row = lambda i: (i, 0)
    const = lambda i: (0, 0)
    const3 = lambda i: (0, 0, 0)
    if seq_mode:
        prev_spec = pl.BlockSpec((8, D_MODEL), lambda i: (jnp.maximum(i * (tm // 8) - 1, 0), 0))
        h_spec = pl.BlockSpec((1, 8, D_MODEL), lambda i: (i, 0, 0))
        h_shape = jax.ShapeDtypeStruct((T // tm, 8, D_MODEL), f32)
    else:
        prev_spec = pl.BlockSpec((tm, D_MODEL), row)
        h_spec = pl.BlockSpec((tm, D_MODEL), row)
        h_shape = jax.ShapeDtypeStruct((T, D_MODEL), f32)
    vec = lambda dt: jax.ShapeDtypeStruct((T, D_MODEL), dt)
    tile = pl.BlockSpec((tm, D_MODEL), row)
    full = lambda arr: pl.BlockSpec(arr.shape, const3 if arr.ndim == 3 else const)
    weights = [p["gmix"], p["mix"], p["w_rkv"], p["w0"], p["w1"], p["w2"], p["a0"], p["a1"], p["a2"],
               p["g1"], p["g2"], p["k_k"], p["k_a"], p["r_k"], p["ones_blk"]]
    return pl.pallas_call(
        functools.partial(_rwkv_proj_kernel, seq_mode=seq_mode, tiles_per_seq=tiles_per_seq),
        grid=(T // tm,),
        in_specs=[tile, prev_spec] + [full(w) for w in weights],
        out_specs=[tile] * 8 + [h_spec],
        out_shape=[vec(vec_dtype)] * 5 + [vec(f32), vec(bf16), vec(bf16), h_shape],
        compiler_params=_cparams(("parallel",)),
        name="rwkv_proj",
    )(x, prev, *weights)


def _wkv_chunk_kernel(r_ref, k_ref, v_ref, kn_ref, b_ref, ld_ref, y_ref, st_ref, m_ref, *, chunks):
    C = WKV_CHUNK
    G = WKV_GROUP
    W = GROUP_W
    c_idx = pl.program_id(1)

    @pl.when(c_idx == 0)
    def _():
        m_ref[...] = jnp.zeros(m_ref.shape, f32)

    rr = lax.broadcasted_iota(jnp.int32, (W, W), 0)
    cc = lax.broadcasted_iota(jnp.int32, (W, W), 1)
    bd_mask = (rr // RWKV_HEAD) == (cc // RWKV_HEAD)
    t_idx = lax.broadcasted_iota(jnp.int32, (C, W), 0)
    s_idx = lax.broadcasted_iota(jnp.int32, (C, W), 1) % C
    strict = s_idx < t_idx
    incl = s_idx <= t_idx
    eye_cat = (s_idx == t_idx).astype(f32)
    tri = (lax.broadcasted_iota(jnp.int32, (C, C), 1) <= lax.broadcasted_iota(jnp.int32, (C, C), 0)).astype(f32)
    lane_lo = lax.broadcasted_iota(jnp.int32, (C, LANES), 1) < RWKV_HEAD
    zero_tile = jnp.zeros((C, LANES), bf16)

    def bd(y):
        yb = y.astype(bf16)
        rows = []
        for hh in range(G):
            lt = hh // 2
            t = yb[:, lt * LANES:(lt + 1) * LANES]
            keep = jnp.where(lane_lo, t, zero_tile) if hh % 2 == 0 else jnp.where(lane_lo, zero_tile, t)
            tiles = [zero_tile] * (W // LANES)
            tiles[lt] = keep
            rows.append(jnp.concatenate(tiles, axis=1))
        return jnp.concatenate(rows, axis=0)

    def bdmm(x, y):
        return _dot(x.astype(bf16), bd(y))

    NG = r_ref.shape[1] // W
    streams = [(ci, gi) for ci in range(chunks) for gi in range(NG)]
    ns = range(len(streams))

    def load(ref, s):
        ci, gi = streams[s]
        return ref[ci * C:(ci + 1) * C, gi * W:(gi + 1) * W]

    hi = lax.Precision.HIGHEST
    ld = [load(ld_ref, s) for s in ns]
    cs = [jnp.dot(tri, ld[s], precision=hi, preferred_element_type=f32) for s in ns]
    tot_col = [jnp.broadcast_to(cs[s][C - 1:C, :], (LANES, W)).T for s in ns]
    r = [load(r_ref, s).astype(f32) for s in ns]
    k = [load(k_ref, s).astype(f32) for s in ns]
    v = [load(v_ref, s).astype(f32) for s in ns]
    kn = [load(kn_ref, s).astype(f32) for s in ns]
    b = [load(b_ref, s).astype(f32) for s in ns]
    p_inv = [jnp.exp(-cs[s]) for s in ns]
    p_rest = [jnp.exp(cs[s][C - 1:C, :] - cs[s]) for s in ns]
    a_t = [-kn[s] * jnp.exp(cs[s] - ld[s]) for s in ns]
    r_t = [r[s] * jnp.exp(cs[s]) for s in ns]
    lhs = [jnp.concatenate([a_t[s], r_t[s]], axis=0).astype(bf16) for s in ns]
    s_b = [_dot_nt(lhs[s], bd(b[s] * p_inv[s])) for s in ns]
    s_k = [_dot_nt(lhs[s], bd(k[s] * p_inv[s])) for s in ns]
    l_ab = [jnp.where(strict, s_b[s][:C], 0.0) for s in ns]
    l_ak = [jnp.where(strict, s_k[s][:C], 0.0) for s in ns]
    a_rb = [jnp.where(incl, s_b[s][C:], 0.0) for s in ns]
    a_rk = [jnp.where(incl, s_k[s][C:], 0.0) for s in ns]
    t_inv = [eye_cat + l_ab[s] for s in ns]
    pw = l_ab
    step = 1
    while 2 * step < C:
        pw = [bdmm(pw[s], pw[s]) for s in ns]
        t_inv = [t_inv[s] + bdmm(t_inv[s], pw[s]) for s in ns]
        step *= 2
    w1 = [bdmm(t_inv[s], a_t[s]) for s in ns]
    from_v = [bdmm(jnp.concatenate([l_ak[s], a_rk[s]], axis=0), v[s]) for s in ns]
    w2 = [bdmm(t_inv[s], from_v[s][:C]) for s in ns]
    y_v = [from_v[s][C:] for s in ns]
    st_lhs = [jnp.concatenate([w1[s], r_t[s]], axis=0).astype(bf16) for s in ns]
    bk_rest = [jnp.concatenate([b[s] * p_rest[s], k[s] * p_rest[s]], axis=0).astype(bf16) for s in ns]
    decay_col = [jnp.exp(jnp.concatenate([tot_col[s]] * (W // LANES), axis=1)) for s in ns]

    m = [m_ref[gi] for gi in range(NG)]
    for ci in range(chunks):
        ss = [ci * NG + gi for gi in range(NG)]
        from_state = [_dot(st_lhs[s], m[gi].astype(bf16)) for gi, s in enumerate(ss)]
        u = [from_state[gi][:C] + w2[s] for gi, s in enumerate(ss)]
        y_u = [bdmm(a_rb[s], u[gi]) for gi, s in enumerate(ss)]
        upd = [_dot_tn(bk_rest[s], jnp.concatenate([u[gi], v[s]], axis=0).astype(bf16)) for gi, s in enumerate(ss)]
        for gi, s in enumerate(ss):
            y_ref[ci * C:(ci + 1) * C, gi * W:(gi + 1) * W] = from_state[gi][C:] + y_u[gi] + y_v[s]
        m = [m[gi] * decay_col[s] + jnp.where(bd_mask, upd[gi], 0.0) for gi, s in enumerate(ss)]
    for gi in range(NG):
        m_ref[gi] = m[gi]
        acc = m[gi][:RWKV_HEAD]
        for hh in range(1, G):
            acc = acc + m[gi][hh * RWKV_HEAD:(hh + 1) * RWKV_HEAD]
        st_ref[0, :, gi * W:(gi + 1) * W] = acc


def _wkv_chunked(r, k, v, kn, b, ld, B, S):
    T = B * S
    chunks = WKV_CHUNKS_PER_STEP if S % (WKV_CHUNK * WKV_CHUNKS_PER_STEP) == 0 else 1
    rows = WKV_CHUNK * chunks
    nc = S // rows
    tile = pl.BlockSpec((rows, D_MODEL), lambda bi, ci: (bi * nc + ci, 0))
    return pl.pallas_call(
        functools.partial(_wkv_chunk_kernel, chunks=chunks),
        grid=(B, nc),
        in_specs=[tile] * 6,
        out_specs=[tile, pl.BlockSpec((1, RWKV_HEAD, D_MODEL), lambda bi, ci: (bi, 0, 0))],
        out_shape=[jax.ShapeDtypeStruct((T, D_MODEL), f32),
                   jax.ShapeDtypeStruct((B, RWKV_HEAD, D_MODEL), f32)],
        scratch_shapes=[pltpu.VMEM((D_MODEL // GROUP_W, GROUP_W, GROUP_W), f32)],
        compiler_params=_cparams(("parallel", "arbitrary")),
        name="wkv_chunked",
    )(r, k, v, kn, b, ld)


def _wkv_step_kernel(s_ref, vec_ref, snew_ref, y_ref):
    r, k, kn, b, ld = (vec_ref[i, 0] for i in (0, 1, 3, 4, 5))
    a = -kn
    w = jnp.exp(ld)

    def value_row(vi, c):
        st = s_ref[0, vi]
        sa = jnp.sum(st * a, axis=0, keepdims=True)
        v_row = vec_ref[2, 0, pl.ds(vi, 1), :]
        st = st * w + sa * b + v_row * k
        snew_ref[0, vi] = st
        y_ref[0, pl.ds(vi, 1), :] = jnp.sum(st * r, axis=0, keepdims=True)
        return c

    lax.fori_loop(0, RWKV_HEAD, value_row, 0, unroll=4)


def _wkv_step(state_t, vecs):
    H, N, _, DB = state_t.shape
    st = pl.BlockSpec((1, N, N, DB), lambda h: (h, 0, 0, 0))
    return pl.pallas_call(
        _wkv_step_kernel,
        grid=(H,),
        in_specs=[st, pl.BlockSpec((6, 1, N, DB), lambda h: (0, h, 0, 0))],
        out_specs=[st, pl.BlockSpec((1, N, DB), lambda h: (h, 0, 0))],
        out_shape=[jax.ShapeDtypeStruct(state_t.shape, f32),
                   jax.ShapeDtypeStruct((H, N, DB), f32)],
        compiler_params=_cparams(("parallel",)),
        name="wkv_step",
    )(state_t, vecs)


def _rwkv_post_kernel(y_ref, bonus_ref, g_ref, lnw_ref, lnb_ref, ones_ref, o_ref):
    ones_blk = ones_ref[...]
    y = y_ref[...]
    inv_n = 1.0 / RWKV_HEAD
    mu = _seg_sum(y, ones_blk) * inv_n
    d = y - mu
    var = _seg_sum(d * d, ones_blk) * inv_n
    yn = d * lax.rsqrt(var + GN_EPS) * lnw_ref[...] + lnb_ref[...]
    o_ref[...] = ((yn + bonus_ref[...].astype(f32)) * g_ref[...].astype(f32)).astype(bf16)


def _rwkv_post(y, bonus, g, lnw, lnb, ones_blk, tm):
    T = y.shape[0]
    row = lambda i: (i, 0)
    const = lambda i: (0, 0)
    tile = pl.BlockSpec((tm, D_MODEL), row)
    return pl.pallas_call(
        _rwkv_post_kernel,
        grid=(T // tm,),
        in_specs=[tile, tile, tile, pl.BlockSpec((1, D_MODEL), const), pl.BlockSpec((1, D_MODEL), const),
                  pl.BlockSpec((LANES, LANES), const)],
        out_specs=tile,
        out_shape=jax.ShapeDtypeStruct((T, D_MODEL), bf16),
        compiler_params=_cparams(("parallel",)),
        name="rwkv_post",
    )(y, bonus, g, lnw, lnb, ones_blk)


def _rope_tables(pos):
    half = QK_ROPE // 2
    inv = 1.0 / (ROPE_THETA ** (jnp.arange(0, QK_ROPE, 2, dtype=f32) / QK_ROPE))
    ang = pos[:, None] * inv[None, :]
    cos, sin = jnp.cos(ang), jnp.sin(ang)
    n = pos.shape[0]
    ones = jnp.ones((n, QK_NOPE), f32)
    z = lambda w: jnp.zeros((n, w), f32)
    tab_c = jnp.concatenate([ones, cos, cos, z(HEAD_TILE - QK_NOPE - QK_ROPE)], axis=1)
    tab_s1 = jnp.concatenate([z(QK_NOPE), -sin, z(HEAD_TILE - QK_NOPE - half)], axis=1)
    tab_s2 = jnp.concatenate([z(QK_NOPE + half), sin, z(HEAD_TILE - QK_NOPE - QK_ROPE)], axis=1)
    return tab_c, tab_s1, tab_s2


def _pick_tile(n, pref):
    t = min(n, pref)
    while n % t:
        t //= 2
    return t


def kernel(x_prompt, x_sample, cache_ckv, cache_kpe, state_wkv, state_shift, page_table,
           norm_mix, norm_ffn, norm_final,
           mla_w_a, mla_q_norm, mla_kv_norm, mla_w_uq, mla_w_ukv, mla_w_o,
           rw_mix, rw_w_rkv, rw_w0, rw_w1, rw_w2, rw_a0, rw_a1, rw_a2, rw_g1, rw_g2,
           rw_k_k, rw_k_a, rw_r_k, rw_ln_w, rw_ln_b, rw_w_o,
           ffn_w_in, ffn_w_out):
    B, S, D = x_prompt.shape
    DB, DS, _ = x_sample.shape
    assert D == D_MODEL and DS == 1 and S % WKV_CHUNK == 0
    n_pages = page_table.shape[1]
    past_len = n_pages * PAGE_SIZE
    H = MLA_HEADS
    row = lambda t: t.reshape(1, -1)
    xp = x_prompt.reshape(B * S, D)
    xs = x_sample.reshape(DB, D)

    w_a = mla_w_a[0]
    pad_pe = jnp.zeros((D, HEAD_TILE), f32).at[:, QK_NOPE:QK_NOPE + QK_ROPE].set(w_a[:, Q_LORA + KV_LORA:])
    wa_ext = jnp.concatenate([w_a[:, :Q_LORA + KV_LORA], pad_pe], axis=1).astype(bf16)
    wuq = (mla_w_uq[0] * (MLA_SCALE * LOG2_E)).reshape(Q_LORA, H, QK_NOPE + QK_ROPE)
    wuq = jnp.pad(wuq, ((0, 0), (0, 0), (0, HEAD_TILE - QK_NOPE - QK_ROPE))).reshape(Q_LORA, H * HEAD_TILE).astype(bf16)
    wukv = mla_w_ukv[0].reshape(KV_LORA, H * HEAD_TILE).astype(bf16)
    mla_wo = mla_w_o[0].astype(bf16)
    qn, kvn = row(mla_q_norm[0]), row(mla_kv_norm[0])
    ffn_in = ffn_w_in.astype(bf16)
    ffn_out = ffn_w_out.astype(bf16)
    hh = jnp.arange(LANES) // RWKV_HEAD
    ones_blk = (hh[:, None] == hh[None, :]).astype(bf16)
    rwp = dict(
        gmix=row(norm_mix[1]), mix=jnp.pad(rw_mix[0], ((0, 2), (0, 0))), w_rkv=rw_w_rkv[0].astype(bf16),
        w0=row(rw_w0[0]), w1=rw_w1[0].astype(bf16), w2=rw_w2[0].astype(bf16),
        a0=row(rw_a0[0]), a1=rw_a1[0].astype(bf16), a2=rw_a2[0].astype(bf16),
        g1=rw_g1[0].astype(bf16), g2=rw_g2[0].astype(bf16),
        k_k=row(rw_k_k[0]), k_a=row(rw_k_a[0]), r_k=row(rw_r_k[0]), ones_blk=ones_blk)
    rw_wo = rw_w_o[0].astype(bf16)
    lnw, lnb = row(rw_ln_w[0]), row(rw_ln_b[0])
    gfin = row(norm_final)

    tm_p = _pick_tile(B * S, 512)
    tm_s = _pick_tile(DB, 128)
    th = FFN_HIDDEN // 2

    tabs_p = _rope_tables(jnp.arange(S, dtype=f32))
    tabs_s = _rope_tables(jnp.full((tm_s,), past_len, f32))
    q_s, _, _, ckv_s, kpe_s = _mla_proj(xs, row(norm_mix[0]), wa_ext, qn, kvn, wuq, wukv, tabs_s, tm_s)
    qlat = jnp.swapaxes(_absorb_q(q_s, wukv), 0, 1)
    qpe = q_s.reshape(DB, H, HEAD_TILE)[:, :, QK_NOPE:QK_NOPE + QK_ROPE]
    o_lat = _decode_attention(page_table, qlat, qpe, ckv_s.reshape(DB, 1, KV_LORA), kpe_s.reshape(DB, 1, QK_ROPE),
                              cache_ckv.reshape(-1, PAGE_SIZE, KV_LORA),
                              jnp.swapaxes(cache_kpe.reshape(-1, PAGE_SIZE, QK_ROPE), 1, 2))
    o_s = _unabsorb_o(jnp.swapaxes(o_lat, 0, 1), wukv)
    o_s = o_s.reshape(DB, H, HEAD_TILE)[:, :, QK_NOPE:].reshape(DB, H * V_HEAD).astype(bf16)
    xs = _proj_ffn(xs, o_s, mla_wo, row(norm_ffn[0]), ffn_in[0], ffn_out[0], gfin, False, tm_s, th)

    q_p, k_p, kv_p, ckv_p, kpe_p = _mla_proj(xp, row(norm_mix[0]), wa_ext, qn, kvn, wuq, wukv, tabs_p,
                                             _pick_tile(S, 512))
    o_p = _flash_attention(q_p, k_p, kv_p, B, S)
    xp = _proj_ffn(xp, o_p, mla_wo, row(norm_ffn[0]), ffn_in[0], ffn_out[0], gfin, False, tm_p, th)

    tm_r = _pick_tile(S, 256)
    r, k, v, kn, b, ld, g, bonus, hl = _rwkv_proj(xp, xp, rwp, S, tm_r, bf16)
    shift_p = hl.reshape(B, S // tm_r, 8, D)[:, -1, 7, :]
    y, st = _wkv_chunked(r, k, v, kn, b, ld, B, S)
    wkv_p = jnp.transpose(st.reshape(B, RWKV_HEAD, RWKV_HEADS, RWKV_HEAD), (0, 2, 3, 1))
    yo = _rwkv_post(y, bonus, g, lnw, lnb, ones_blk, tm_p)
    y_prompt = _proj_ffn(xp, yo, rw_wo, row(norm_ffn[1]), ffn_in[1], ffn_out[1], gfin, True, tm_p, th)

    r, k, v, kn, b, ld, g, bonus, shift_s = _rwkv_proj(xs, state_shift[0], rwp, 1, tm_s, f32)
    vecs = jnp.transpose(jnp.stack([r, k, v, kn, b, ld]), (0, 2, 1)).reshape(6, RWKV_HEADS, RWKV_HEAD, DB)
    wkv_t, y_t = _wkv_step(jnp.transpose(state_wkv[0].astype(f32), (1, 2, 3, 0)), vecs)
    wkv_s = jnp.transpose(wkv_t, (3, 0, 1, 2)).astype(state_wkv.dtype)
    y = jnp.transpose(y_t.reshape(D, DB))
    yo = _rwkv_post(y, bonus, g, lnw, lnb, ones_blk, tm_s)
    y_sample = _proj_ffn(xs, yo, rw_wo, row(norm_ffn[1]), ffn_in[1], ffn_out[1], gfin, True, tm_s, th)

    return (y_prompt.reshape(B, S, D), y_sample.reshape(DB, DS, D),
            ckv_p.reshape(1, B, S, KV_LORA), kpe_p.reshape(1, B, S, QK_ROPE),
            ckv_s.reshape(1, DB, DS, KV_LORA), kpe_s.reshape(1, DB, DS, QK_ROPE),
            wkv_p[None].astype(x_prompt.dtype), shift_p[None],
            wkv_s[None], shift_s[None])
```

```python
import functools
import math

import jax
import jax.numpy as jnp
from jax import lax
from jax.experimental import pallas as pl
from jax.experimental.pallas import tpu as pltpu

f32 = jnp.float32
bf16 = jnp.bfloat16

D_MODEL = 1024
MLA_HEADS = 16
QK_NOPE = 64
QK_ROPE = 32
V_HEAD = 64
Q_LORA = 512
KV_LORA = 256
ROPE_THETA = 10000.0
MLA_SCALE = 1.0 / math.sqrt(QK_NOPE + QK_ROPE)
LOG2_E = math.log2(math.e)
RWKV_HEAD = 64
RWKV_HEADS = D_MODEL // RWKV_HEAD
FFN_HIDDEN = 2816
NORM_EPS = 1e-6
GN_EPS = 64e-5
PAGE_SIZE = 128

LANES = 128
HEAD_TILE = 128
WKV_CHUNK = 64
WKV_GROUP = 4
WKV_CHUNKS_PER_STEP = 2
GROUP_W = WKV_GROUP * RWKV_HEAD
VMEM_LIMIT = 56 * 1024 * 1024


def _cparams(sem):
    return pltpu.CompilerParams(dimension_semantics=sem, vmem_limit_bytes=VMEM_LIMIT)


def _rms(x, g):
    return x * lax.rsqrt(jnp.mean(x * x, axis=-1, keepdims=True) + NORM_EPS) * g


def _dot(a, b):
    return jnp.dot(a, b, preferred_element_type=f32)


def _dot_nt(a, b):
    return lax.dot_general(a, b, (((1,), (1,)), ((), ())), preferred_element_type=f32)


def _dot_tn(a, b):
    return lax.dot_general(a, b, (((0,), (0,)), ((), ())), preferred_element_type=f32)


def _mla_proj_kernel(x_ref, gmix_ref, wa_ref, qn_ref, kvn_ref, wuq_ref, wukv_ref,
                     c_ref, s1_ref, s2_ref, q_ref, k_ref, kv_ref, ckv_ref, kpe_ref):
    tm = x_ref.shape[0]
    h = _rms(x_ref[...], gmix_ref[...]).astype(bf16)
    a = _dot(h, wa_ref[...])
    cq = _rms(a[:, :Q_LORA], qn_ref[...]).astype(bf16)
    ckv = _rms(a[:, Q_LORA:Q_LORA + KV_LORA], kvn_ref[...])
    ckv_ref[...] = ckv
    cos = c_ref[...]
    sin_lo = s1_ref[...]
    sin_hi = s2_ref[...]

    def rope(t):
        return t * cos + pltpu.roll(t, LANES - QK_ROPE // 2, 1) * sin_lo + pltpu.roll(t, QK_ROPE // 2, 1) * sin_hi

    kpe_t = rope(a[:, Q_LORA + KV_LORA:])
    kpe_ref[...] = kpe_t[:, QK_NOPE:QK_NOPE + QK_ROPE]
    kv = _dot(ckv.astype(bf16), wukv_ref[...])
    kv_ref[...] = kv.astype(bf16)
    q = _dot(cq, wuq_ref[...])
    lane = lax.broadcasted_iota(jnp.int32, (tm, HEAD_TILE), 1)
    for hh in range(MLA_HEADS):
        sl = slice(hh * HEAD_TILE, (hh + 1) * HEAD_TILE)
        q_ref[:, sl] = rope(q[:, sl]).astype(bf16)
        k_ref[:, sl] = jnp.where(lane < QK_NOPE, kv[:, sl], kpe_t).astype(bf16)


def _mla_proj(x, gmix, wa, qn, kvn, wuq, wukv, tabs, tm):
    T = x.shape[0]
    tab_c, tab_s1, tab_s2 = tabs
    nt = tab_c.shape[0] // tm
    HW = MLA_HEADS * HEAD_TILE
    const = lambda i: (0, 0)
    row = lambda i: (i, 0)
    tabmap = lambda i: (i % nt, 0)
    return pl.pallas_call(
        _mla_proj_kernel,
        grid=(T // tm,),
        in_specs=[
            pl.BlockSpec((tm, D_MODEL), row),
            pl.BlockSpec((1, D_MODEL), const),
            pl.BlockSpec(wa.shape, const),
            pl.BlockSpec((1, Q_LORA), const),
            pl.BlockSpec((1, KV_LORA), const),
            pl.BlockSpec(wuq.shape, const),
            pl.BlockSpec(wukv.shape, const),
            pl.BlockSpec((tm, HEAD_TILE), tabmap),
            pl.BlockSpec((tm, HEAD_TILE), tabmap),
            pl.BlockSpec((tm, HEAD_TILE), tabmap),
        ],
        out_specs=[
            pl.BlockSpec((tm, HW), row),
            pl.BlockSpec((tm, HW), row),
            pl.BlockSpec((tm, HW), row),
            pl.BlockSpec((tm, KV_LORA), row),
            pl.BlockSpec((tm, QK_ROPE), row),
        ],
        out_shape=[
            jax.ShapeDtypeStruct((T, HW), bf16),
            jax.ShapeDtypeStruct((T, HW), bf16),
            jax.ShapeDtypeStruct((T, HW), bf16),
            jax.ShapeDtypeStruct((T, KV_LORA), f32),
            jax.ShapeDtypeStruct((T, QK_ROPE), f32),
        ],
        compiler_params=_cparams(("parallel",)),
        name="mla_proj",
    )(x, gmix, wa, qn, kvn, wuq, wukv, tab_c, tab_s1, tab_s2)


def _flash_kernel(q_ref, k_ref, kv_ref, o_ref, vt_ref, m_ref, acc_ref, *, tile):
    S = q_ref.shape[0]
    nt = S // tile
    lane = lax.broadcasted_iota(jnp.int32, (tile, HEAD_TILE), 1)
    ones_row = lax.broadcasted_iota(jnp.int32, (HEAD_TILE, tile), 0) == 0
    heads = range(q_ref.shape[1] // HEAD_TILE)
    hs = [slice(hh * HEAD_TILE, (hh + 1) * HEAD_TILE) for hh in heads]

    for hh in heads:
        for j in range(nt):
            vt = kv_ref[j * tile:(j + 1) * tile, hs[hh]].astype(f32).T
            vt_ref[hh, j] = jnp.where(ones_row, 1.0, vt).astype(bf16)

    def q_body(qi, c):
        q0 = pl.multiple_of(qi * tile, tile)
        qs = [q_ref[pl.ds(q0, tile), hs[hh]] for hh in heads]
        for hh in heads:
            m_ref[hh] = jnp.full((1, tile), -jnp.inf, f32)
            acc_ref[hh] = jnp.zeros((HEAD_TILE, tile), f32)

        def block(ki, kr, qr, diag):
            k0 = pl.multiple_of(ki * tile + kr.start, kr.stop - kr.start)
            nk = kr.stop - kr.start
            sts = [_dot_nt(k_ref[pl.ds(k0, nk), hs[hh]], qs[hh][qr]) for hh in heads]
            if diag:
                shape = (nk, qr.stop - qr.start)
                visible = lax.broadcasted_iota(jnp.int32, shape, 0) <= lax.broadcasted_iota(jnp.int32, shape, 1)
                sts = [jnp.where(visible, st, -jnp.inf) for st in sts]
            m_prevs = [m_ref[hh, :, qr] for hh in heads]
            m_news = [jnp.maximum(m_prevs[hh], jnp.max(sts[hh], axis=0, keepdims=True)) for hh in heads]
            ps = [jnp.exp2(sts[hh] - m_news[hh]) for hh in heads]
            alphas = [jnp.exp2(m_prevs[hh] - m_news[hh]) for hh in heads]
            pvs = [_dot(vt_ref[hh, ki, :, kr], ps[hh].astype(bf16)) for hh in heads]
            for hh in heads:
                m_ref[hh, :, qr] = m_news[hh]
                acc_ref[hh, :, qr] = alphas[hh] * acc_ref[hh, :, qr] + pvs[hh]

        def k_body(ki, c2):
            block(ki, slice(0, tile), slice(0, tile), False)
            return c2

        lax.fori_loop(0, qi, k_body, 0)
        half = tile // 2
        block(qi, slice(0, half), slice(0, tile), True)
        block(qi, slice(half, tile), slice(half, tile), True)
        accs = [acc_ref[hh] for hh in heads]
        outs = [(a / a[0:1, :]).T for a in accs]
        for pr in range(len(outs) // 2):
            o = jnp.where(lane < V_HEAD, pltpu.roll(outs[2 * pr], V_HEAD, 1), outs[2 * pr + 1])
            o_ref[pl.ds(q0, tile), pr * 2 * V_HEAD:(pr + 1) * 2 * V_HEAD] = o.astype(bf16)
        return c

    lax.fori_loop(0, nt, q_body, 0)


def _flash_attention(q, k, kv, B, S, tile=512, heads_per_step=4):
    T = B * S
    tile = min(tile, S)
    nh = heads_per_step
    blk = pl.BlockSpec((S, nh * HEAD_TILE), lambda b, hp: (b, hp))
    return pl.pallas_call(
        functools.partial(_flash_kernel, tile=tile),
        grid=(B, MLA_HEADS // nh),
        in_specs=[blk, blk, blk],
        out_specs=pl.BlockSpec((S, nh * V_HEAD), lambda b, hp: (b, hp)),
        out_shape=jax.ShapeDtypeStruct((T, MLA_HEADS * V_HEAD), bf16),
        scratch_shapes=[
            pltpu.VMEM((nh, S // tile, HEAD_TILE, tile), bf16),
            pltpu.VMEM((nh, 1, tile), f32),
            pltpu.VMEM((nh, HEAD_TILE, tile), f32),
        ],
        compiler_params=_cparams(("parallel", "parallel")),
        name="mla_flash",
    )(q, k, kv)


def _absorb_q_kernel(q_ref, w_ref, o_ref):
    lane = lax.broadcasted_iota(jnp.int32, q_ref.shape, 1)
    qn = jnp.where(lane < QK_NOPE, q_ref[...], jnp.zeros_like(q_ref[...]))
    o_ref[0] = _dot_nt(qn, w_ref[...]).astype(bf16)


def _absorb_q(q, wukv):
    DB = q.shape[0]
    return pl.pallas_call(
        _absorb_q_kernel,
        grid=(MLA_HEADS,),
        in_specs=[
            pl.BlockSpec((DB, HEAD_TILE), lambda h: (0, h)),
            pl.BlockSpec((KV_LORA, HEAD_TILE), lambda h: (0, h)),
        ],
        out_specs=pl.BlockSpec((1, DB, KV_LORA), lambda h: (h, 0, 0)),
        out_shape=jax.ShapeDtypeStruct((MLA_HEADS, DB, KV_LORA), bf16),
        compiler_params=_cparams(("parallel",)),
        name="mla_absorb_q",
    )(q, wukv)


def _unabsorb_o_kernel(o_ref, w_ref, out_ref):
    out_ref[...] = _dot(o_ref[0].astype(bf16), w_ref[...])


def _unabsorb_o(o_lat, wukv):
    DB = o_lat.shape[1]
    return pl.pallas_call(
        _unabsorb_o_kernel,
        grid=(MLA_HEADS,),
        in_specs=[
            pl.BlockSpec((1, DB, KV_LORA), lambda h: (h, 0, 0)),
            pl.BlockSpec((KV_LORA, HEAD_TILE), lambda h: (0, h)),
        ],
        out_specs=pl.BlockSpec((DB, HEAD_TILE), lambda h: (0, h)),
        out_shape=jax.ShapeDtypeStruct((DB, MLA_HEADS * HEAD_TILE), f32),
        compiler_params=_cparams(("parallel",)),
        name="mla_unabsorb_o",
    )(o_lat, wukv)


def _decode_kernel(pt_ref, qlat_ref, qpe_ref, cnew_ref, pnew_ref, ckv_hbm, kpe_hbm, o_ref,
                   ckv_buf, kpe_buf, sem, m_ref, l_ref, acc_ref, *, pages_per_step, chains):
    PP = pages_per_step
    NC = chains
    DB, n_pages = pt_ref.shape
    G = n_pages // PP
    per_chain = DB // NC
    total = per_chain * G

    def page_copy(page, slot, idx, which):
        if which == 0:
            return pltpu.make_async_copy(ckv_hbm.at[page], ckv_buf.at[slot, idx], sem.at[0, slot])
        return pltpu.make_async_copy(kpe_hbm.at[page], kpe_buf.at[slot, idx], sem.at[1, slot])

    def start_group(it, slot):
        bb = it // G
        g0 = (it % G) * PP
        for c in range(NC):
            for p in range(PP):
                page = pt_ref[c * per_chain + bb, g0 + p]
                page_copy(page, slot, c * PP + p, 0).start()
                page_copy(page, slot, c * PP + p, 1).start()

    def wait_group(slot):
        for idx in range(NC * PP):
            page_copy(0, slot, idx, 0).wait()
            page_copy(0, slot, idx, 1).wait()

    it = pl.program_id(0)
    slot = it % 2

    @pl.when(it == 0)
    def _():
        start_group(0, 0)

    @pl.when(it + 1 < total)
    def _():
        start_group(it + 1, 1 - slot)

    wait_group(slot)
    _decode_group(it // G, it % G, G, per_chain, slot, qlat_ref, qpe_ref, cnew_ref, pnew_ref, o_ref,
                  ckv_buf, kpe_buf, m_ref, l_ref, acc_ref, PP, NC)


def _decode_group(bb, g, G, per_chain, slot, qlat_ref, qpe_ref, cnew_ref, pnew_ref, o_ref,
                  ckv_buf, kpe_buf, m_ref, l_ref, acc_ref, PP, NC):
    chains = range(NC)
    bs = [c * per_chain + bb for c in chains]

    @pl.when(g == 0)
    def _():
        m_ref[...] = jnp.full(m_ref.shape, -jnp.inf, f32)
        l_ref[...] = jnp.zeros(l_ref.shape, f32)
        acc_ref[...] = jnp.zeros(acc_ref.shape, f32)

    qlat = [qlat_ref[bs[c]] for c in chains]
    qpe = [qpe_ref[bs[c]] for c in chains]
    cks = [[ckv_buf[slot, c * PP + p].astype(bf16) for p in range(PP)] for c in chains]
    s = [jnp.concatenate(
        [_dot_nt(qlat[c], cks[c][p]) + _dot(qpe[c], kpe_buf[slot, c * PP + p].astype(bf16)) for p in range(PP)],
        axis=1) for c in chains]
    m_prev = [m_ref[c] for c in chains]
    m_new = [jnp.maximum(m_prev[c], jnp.max(s[c], axis=-1, keepdims=True)) for c in chains]
    p_all = [jnp.exp2(s[c] - m_new[c]) for c in chains]
    alpha = [jnp.exp2(m_prev[c] - m_new[c]) for c in chains]
    for c in chains:
        l_ref[c] = alpha[c] * l_ref[c] + jnp.sum(p_all[c], axis=-1, keepdims=True)
        m_ref[c] = m_new[c]
    for c in chains:
        pv = _dot(p_all[c][:, :PAGE_SIZE].astype(bf16), cks[c][0])
        for p in range(1, PP):
            pv = pv + _dot(p_all[c][:, p * PAGE_SIZE:(p + 1) * PAGE_SIZE].astype(bf16), cks[c][p])
        acc_ref[c] = alpha[c] * acc_ref[c] + pv

    @pl.when(g == G - 1)
    def _():
        for c in chains:
            cnew = cnew_ref[bs[c]].astype(bf16).astype(f32)
            pnew = pnew_ref[bs[c]].astype(bf16).astype(f32)
            s_self = (jnp.sum(qlat[c].astype(f32) * cnew, axis=-1, keepdims=True)
                      + jnp.sum(qpe[c].astype(f32) * pnew, axis=-1, keepdims=True))
            m_last = m_ref[c]
            m_fin = jnp.maximum(m_last, s_self)
            p_self = jnp.exp2(s_self - m_fin)
            a_fin = jnp.exp2(m_last - m_fin)
            l_fin = a_fin * l_ref[c] + p_self
            o_ref[bs[c]] = (a_fin * acc_ref[c] + p_self * cnew) / l_fin


def _decode_attention(page_table, qlat, qpe, ckv_new, kpe_new, pool_ckv, pool_kpe_t, pages_per_step=16):
    DB, n_pages = page_table.shape
    PP = math.gcd(n_pages, pages_per_step)
    NC = 2 if DB % 2 == 0 else 1
    H = MLA_HEADS
    vmem = pl.BlockSpec(memory_space=pltpu.VMEM)
    hbm = pl.BlockSpec(memory_space=pl.ANY)
    grid_spec = pltpu.PrefetchScalarGridSpec(
        num_scalar_prefetch=1,
        grid=((DB // NC) * (n_pages // PP),),
        in_specs=[vmem, vmem, vmem, vmem, hbm, hbm],
        out_specs=vmem,
        scratch_shapes=[
            pltpu.VMEM((2, NC * PP, PAGE_SIZE, KV_LORA), pool_ckv.dtype),
            pltpu.VMEM((2, NC * PP, QK_ROPE, PAGE_SIZE), pool_kpe_t.dtype),
            pltpu.SemaphoreType.DMA((2, 2)),
            pltpu.VMEM((NC, H, 1), f32),
            pltpu.VMEM((NC, H, 1), f32),
            pltpu.VMEM((NC, H, KV_LORA), f32),
        ],
    )
    return pl.pallas_call(
        functools.partial(_decode_kernel, pages_per_step=PP, chains=NC),
        grid_spec=grid_spec,
        out_shape=jax.ShapeDtypeStruct((DB, H, KV_LORA), f32),
        compiler_params=_cparams(("arbitrary",)),
        name="mla_decode",
    )(page_table, qlat, qpe, ckv_new, kpe_new, pool_ckv, pool_kpe_t)


def _proj_ffn_kernel(x_ref, o_ref, wo_ref, gffn_ref, wg_ref, wu_ref, wout_ref, gfin_ref,
                     out_ref, acc_ref, h_ref, *, final_norm):
    j = pl.program_id(1)

    @pl.when(j == 0)
    def _():
        x1 = x_ref[...] + _dot(o_ref[...], wo_ref[...])
        acc_ref[...] = x1
        h_ref[...] = _rms(x1, gffn_ref[...]).astype(bf16)

    h = h_ref[...]
    gate = _dot(h, wg_ref[...])
    up = _dot(h, wu_ref[...])
    act = (gate * jax.nn.sigmoid(gate) * up).astype(bf16)
    acc_ref[...] += _dot(act, wout_ref[...])

    @pl.when(j == pl.num_programs(1) - 1)
    def _():
        y = acc_ref[...]
        if final_norm:
            y = _rms(y, gfin_ref[...])
        out_ref[...] = y


def _proj_ffn(x, o, wo, gffn, w_in, w_out, gfin, final_norm, tm, th):
    T = x.shape[0]
    nh = FFN_HIDDEN // th
    row = lambda i, j: (i, 0)
    const = lambda i, j: (0, 0)
    return pl.pallas_call(
        functools.partial(_proj_ffn_kernel, final_norm=final_norm),
        grid=(T // tm, nh),
        in_specs=[
            pl.BlockSpec((tm, D_MODEL), row),
            pl.BlockSpec((tm, D_MODEL), row),
            pl.BlockSpec((D_MODEL, D_MODEL), const),
            pl.BlockSpec((1, D_MODEL), const),
            pl.BlockSpec((D_MODEL, th), lambda i, j: (0, j)),
            pl.BlockSpec((D_MODEL, th), lambda i, j: (0, j + nh)),
            pl.BlockSpec((th, D_MODEL), lambda i, j: (j, 0)),
            pl.BlockSpec((1, D_MODEL), const),
        ],
        out_specs=pl.BlockSpec((tm, D_MODEL), row),
        out_shape=jax.ShapeDtypeStruct((T, D_MODEL), f32),
        scratch_shapes=[pltpu.VMEM((tm, D_MODEL), f32), pltpu.VMEM((tm, D_MODEL), bf16)],
        compiler_params=_cparams(("parallel", "arbitrary")),
        name="proj_ffn",
    )(x, o, wo, gffn, w_in, w_in, w_out, gfin)


def _seg_sum(x, ones_blk):
    parts = [_dot(x[:, gidx * LANES:(gidx + 1) * LANES].astype(bf16), ones_blk)
             for gidx in range(x.shape[1] // LANES)]
    return jnp.concatenate(parts, axis=1)


def _rwkv_proj_kernel(x_ref, prev_ref, gmix_ref, mix_ref, wrkv_ref, w0_ref, w1_ref, w2_ref,
                      a0_ref, a1_ref, a2_ref, g1_ref, g2_ref, kk_ref, ka_ref, rk_ref, ones_ref,
                      r_out, k_out, v_out, kn_out, b_out, ld_out, g_out, bonus_out, h_out,
                      *, seq_mode, tiles_per_seq):
    tm = x_ref.shape[0]
    gm = gmix_ref[...]
    h = _rms(x_ref[...], gm)
    if seq_mode:
        hp_row = _rms(prev_ref[...], gm)[7:8, :]
        is_start = (pl.program_id(0) % tiles_per_seq) == 0
        hp_row = jnp.where(is_start, jnp.zeros_like(hp_row), hp_row)
        rowid = lax.broadcasted_iota(jnp.int32, (tm, 1), 0)
        hprev = jnp.where(rowid == 0, hp_row, pltpu.roll(h, 1, 0))
        h_out[0] = h[tm - 8:, :]
    else:
        hprev = prev_ref[...]
        h_out[...] = h
    xx = hprev - h
    mix = mix_ref[...]
    xr, xw, xk, xv, xa, xg = ((h + xx * mix[n:n + 1]).astype(bf16) for n in range(6))
    r = _dot(xr, wrkv_ref[0])
    k = _dot(xk, wrkv_ref[1])
    v = _dot(xv, wrkv_ref[2])
    wl = w0_ref[...] + _dot(jnp.tanh(_dot(xw, w1_ref[...])).astype(bf16), w2_ref[...])
    z = -wl
    w_log = -(jnp.maximum(z, 0.0) + jnp.log(1.0 + jnp.exp(-jnp.abs(z)))) - 0.5
    ld_out[...] = -jnp.exp(w_log)
    a = jax.nn.sigmoid(a0_ref[...] + _dot(_dot(xa, a1_ref[...]).astype(bf16), a2_ref[...]))
    g = _dot(jax.nn.sigmoid(_dot(xg, g1_ref[...])).astype(bf16), g2_ref[...])
    ones_blk = ones_ref[...]
    kk = k * kk_ref[...]
    kk = kk / jnp.maximum(jnp.sqrt(_seg_sum(kk * kk, ones_blk)), 1e-12)
    k = k * (1.0 + (a - 1.0) * ka_ref[...])
    bonus = _seg_sum(r * k * rk_ref[...], ones_blk) * v
    r_out[...] = r.astype(r_out.dtype)
    k_out[...] = k.astype(k_out.dtype)
    v_out[...] = v.astype(v_out.dtype)
    kn_out[...] = kk.astype(kn_out.dtype)
    b_out[...] = (kk * a).astype(b_out.dtype)
    g_out[...] = g.astype(g_out.dtype)
    bonus_out[...] = bonus.astype(bonus_out.dtype)


def _rwkv_proj(x, prev, p, seq_len, tm, vec_dtype):
    T = x.shape[0]
    seq_mode = seq_len > 1
    tiles_per_seq = max(seq_len // tm, 1)
    row = lambda i: (i, 0)
    const = lambda i: (0, 0)
    const3 = lambda i: (0, 0, 0)
    if seq_mode:
        prev_spec = pl.BlockSpec((8, D_MODEL), lambda i: (jnp.maximum(i * (tm // 8) - 1, 0), 0))
        h_spec = pl.BlockSpec((1, 8, D_MODEL), lambda i: (i, 0, 0))
        h_shape = jax.ShapeDtypeStruct((T // tm, 8, D_MODEL), f32)
    else:
        prev_spec = pl.BlockSpec((tm, D_MODEL), row)
        h_spec = pl.BlockSpec((tm, D_MODEL), row)
        h_shape = jax.ShapeDtypeStruct((T, D_MODEL), f32)
    vec = lambda dt: jax.ShapeDtypeStruct((T, D_MODEL), dt)
    tile = pl.BlockSpec((tm, D_MODEL), row)
    full = lambda arr: pl.BlockSpec(arr.shape, const3 if arr.ndim == 3 else const)
    weights = [p["gmix"], p["mix"], p["w_rkv"], p["w0"], p["w1"], p["w2"], p["a0"], p["a1"], p["a2"],
               p["g1"], p["g2"], p["k_k"], p["k_a"], p["r_k"], p["ones_blk"]]
    return pl.pallas_call(
        functools.partial(_rwkv_proj_kernel, seq_mode=seq_mode, tiles_per_seq=tiles_per_seq),
        grid=(T // tm,),
        in_specs=[tile, prev_spec] + [full(w) for w in weights],
        out_specs=[tile] * 8 + [h_spec],
        out_shape=[vec(vec_dtype)] * 5 + [vec(f32), vec(bf16), vec(bf16), h_shape],
        compiler_params=_cparams(("parallel",)),
        name="rwkv_proj",
    )(x, prev, *weights)


def _wkv_chunk_kernel(r_ref, k_ref, v_ref, kn_ref, b_ref, ld_ref, y_ref, st_ref, m_ref, *, chunks):
    C = WKV_CHUNK
    G = WKV_GROUP
    W = GROUP_W
    c_idx = pl.program_id(1)

    @pl.when(c_idx == 0)
    def _():
        m_ref[...] = jnp.zeros(m_ref.shape, f32)

    rr = lax.broadcasted_iota(jnp.int32, (W, W), 0)
    cc = lax.broadcasted_iota(jnp.int32, (W, W), 1)
    bd_mask = (rr // RWKV_HEAD) == (cc // RWKV_HEAD)
    t_idx = lax.broadcasted_iota(jnp.int32, (C, W), 0)
    s_idx = lax.broadcasted_iota(jnp.int32, (C, W), 1) % C
    strict = s_idx < t_idx
    incl = s_idx <= t_idx
    eye_cat = (s_idx == t_idx).astype(f32)
    tri = (lax.broadcasted_iota(jnp.int32, (C, C), 1) <= lax.broadcasted_iota(jnp.int32, (C, C), 0)).astype(f32)
    lane_lo = lax.broadcasted_iota(jnp.int32, (C, LANES), 1) < RWKV_HEAD
    zero_tile = jnp.zeros((C, LANES), bf16)

    def bd(y):
        yb = y.astype(bf16)
        rows = []
        for hh in range(G):
            lt = hh // 2
            t = yb[:, lt * LANES:(lt + 1) * LANES]
            keep = jnp.where(lane_lo, t, zero_tile) if hh % 2 == 0 else jnp.where(lane_lo, zero_tile, t)
            tiles = [zero_tile] * (W // LANES)
            tiles[lt] = keep
            rows.append(jnp.concatenate(tiles, axis=1))
        return jnp.concatenate(rows, axis=0)

    def bdmm(x, y):
        return _dot(x.astype(bf16), bd(y))

    NG = r_ref.shape[1] // W
    streams = [(ci, gi) for ci in range(chunks) for gi in range(NG)]
    ns = range(len(streams))

    def load(ref, s):
        ci, gi = streams[s]
        return ref[ci * C:(ci + 1) * C, gi * W:(gi + 1) * W]

    hi = lax.Precision.HIGHEST
    ld = [load(ld_ref, s) for s in ns]
    cs = [jnp.dot(tri, ld[s], precision=hi, preferred_element_type=f32) for s in ns]
    tot_col = [jnp.broadcast_to(cs[s][C - 1:C, :], (LANES, W)).T for s in ns]
    r = [load(r_ref, s).astype(f32) for s in ns]
    k = [load(k_ref, s).astype(f32) for s in ns]
    v = [load(v_ref, s).astype(f32) for s in ns]
    kn = [load(kn_ref, s).astype(f32) for s in ns]
    b = [load(b_ref, s).astype(f32) for s in ns]
    p_inv = [jnp.exp(-cs[s]) for s in ns]
    p_rest = [jnp.exp(cs[s][C - 1:C, :] - cs[s]) for s in ns]
    a_t = [-kn[s] * jnp.exp(cs[s] - ld[s]) for s in ns]
    r_t = [r[s] * jnp.exp(cs[s]) for s in ns]
    lhs = [jnp.concatenate([a_t[s], r_t[s]], axis=0).astype(bf16) for s in ns]
    s_b = [_dot_nt(lhs[s], bd(b[s] * p_inv[s])) for s in ns]
    s_k = [_dot_nt(lhs[s], bd(k[s] * p_inv[s])) for s in ns]
    l_ab = [jnp.where(strict, s_b[s][:C], 0.0) for s in ns]
    l_ak = [jnp.where(strict, s_k[s][:C], 0.0) for s in ns]
    a_rb = [jnp.where(incl, s_b[s][C:], 0.0) for s in ns]
    a_rk = [jnp.where(incl, s_k[s][C:], 0.0) for s in ns]
    t_inv = [eye_cat + l_ab[s] for s in ns]
    pw = [bdmm(l_ab[s], l_ab[s]) for s in ns]
    step = 2
    while step < C:
        rhs = [bd(pw[s]) for s in ns]
        if 2 * step < C:
            both = [_dot(jnp.concatenate([pw[s], t_inv[s]], axis=0).astype(bf16), rhs[s]) for s in ns]
            pw = [both[s][:C] for s in ns]
            t_inv = [t_inv[s] + both[s][C:] for s in ns]
        else:
            t_inv = [t_inv[s] + _dot(t_inv[s].astype(bf16), rhs[s]) for s in ns]
        step *= 2
    from_v = [bdmm(jnp.concatenate([l_ak[s], a_rk[s]], axis=0), v[s]) for s in ns]
    y_v = [from_v[s][C:] for s in ns]
    bk_rest = [jnp.concatenate([b[s] * p_rest[s], k[s] * p_rest[s]], axis=0).astype(bf16) for s in ns]
    decay_col = [jnp.exp(jnp.concatenate([tot_col[s]] * (W // LANES), axis=1)) for s in ns]

    m = [m_ref[gi] for gi in range(NG)]
    for ci in range(chunks):
        ss = [ci * NG + gi for gi in range(NG)]
        from_state = [_dot(lhs[s], m[gi].astype(bf16)) for gi, s in enumerate(ss)]
        u = [bdmm(t_inv[s], from_state[gi][:C] + from_v[s][:C]) for gi, s in enumerate(ss)]
        y_u = [bdmm(a_rb[s], u[gi]) for gi, s in enumerate(ss)]
        upd = [_dot_tn(bk_rest[s], jnp.concatenate([u[gi], v[s]], axis=0).astype(bf16)) for gi, s in enumerate(ss)]
        for gi, s in enumerate(ss):
            y_ref[ci * C:(ci + 1) * C, gi * W:(gi + 1) * W] = from_state[gi][C:] + y_u[gi] + y_v[s]
        m = [m[gi] * decay_col[s] + jnp.where(bd_mask, upd[gi], 0.0) for gi, s in enumerate(ss)]
    for gi in range(NG):
        m_ref[gi] = m[gi]
        acc = m[gi][:RWKV_HEAD]
        for hh in range(1, G):
            acc = acc + m[gi][hh * RWKV_HEAD:(hh + 1) * RWKV_HEAD]
        st_ref[0, :, gi * W:(gi + 1) * W] = acc


def _wkv_chunked(r, k, v, kn, b, ld, B, S):
    T = B * S
    chunks = WKV_CHUNKS_PER_STEP if S % (WKV_CHUNK * WKV_CHUNKS_PER_STEP) == 0 else 1
    rows = WKV_CHUNK * chunks
    nc = S // rows
    tile = pl.BlockSpec((rows, D_MODEL), lambda bi, ci: (bi * nc + ci, 0))
    return pl.pallas_call(
        functools.partial(_wkv_chunk_kernel, chunks=chunks),
        grid=(B, nc),
        in_specs=[tile] * 6,
        out_specs=[tile, pl.BlockSpec((1, RWKV_HEAD, D_MODEL), lambda bi, ci: (bi, 0, 0))],
        out_shape=[jax.ShapeDtypeStruct((T, D_MODEL), f32),
                   jax.ShapeDtypeStruct((B, RWKV_HEAD, D_MODEL), f32)],
        scratch_shapes=[pltpu.VMEM((D_MODEL // GROUP_W, GROUP_W, GROUP_W), f32)],
        compiler_params=_cparams(("parallel", "arbitrary")),
        name="wkv_chunked",
    )(r, k, v, kn, b, ld)


def _wkv_step_kernel(s_ref, vec_ref, snew_ref, y_ref):
    r, k, kn, b, ld = (vec_ref[i, 0] for i in (0, 1, 3, 4, 5))
    a = -kn
    w = jnp.exp(ld)

    def value_row(vi, c):
        st = s_ref[0, vi]
        sa = jnp.sum(st * a, axis=0, keepdims=True)
        v_row = vec_ref[2, 0, pl.ds(vi, 1), :]
        st = st * w + sa * b + v_row * k
        snew_ref[0, vi] = st
        y_ref[0, pl.ds(vi, 1), :] = jnp.sum(st * r, axis=0, keepdims=True)
        return c

    lax.fori_loop(0, RWKV_HEAD, value_row, 0, unroll=4)


def _wkv_step(state_t, vecs):
    H, N, _, DB = state_t.shape
    st = pl.BlockSpec((1, N, N, DB), lambda h: (h, 0, 0, 0))
    return pl.pallas_call(
        _wkv_step_kernel,
        grid=(H,),
        in_specs=[st, pl.BlockSpec((6, 1, N, DB), lambda h: (0, h, 0, 0))],
        out_specs=[st, pl.BlockSpec((1, N, DB), lambda h: (h, 0, 0))],
        out_shape=[jax.ShapeDtypeStruct(state_t.shape, f32),
                   jax.ShapeDtypeStruct((H, N, DB), f32)],
        compiler_params=_cparams(("parallel",)),
        name="wkv_step",
    )(state_t, vecs)


def _rwkv_post_kernel(y_ref, bonus_ref, g_ref, lnw_ref, lnb_ref, ones_ref, o_ref):
    ones_blk = ones_ref[...]
    y = y_ref[...]
    inv_n = 1.0 / RWKV_HEAD
    mu = _seg_sum(y, ones_blk) * inv_n
    d = y - mu
    var = _seg_sum(d * d, ones_blk) * inv_n
    yn = d * lax.rsqrt(var + GN_EPS) * lnw_ref[...] + lnb_ref[...]
    o_ref[...] = ((yn + bonus_ref[...].astype(f32)) * g_ref[...].astype(f32)).astype(bf16)


def _rwkv_post(y, bonus, g, lnw, lnb, ones_blk, tm):
    T = y.shape[0]
    row = lambda i: (i, 0)
    const = lambda i: (0, 0)
    tile = pl.BlockSpec((tm, D_MODEL), row)
    return pl.pallas_call(
        _rwkv_post_kernel,
        grid=(T // tm,),
        in_specs=[tile, tile, tile, pl.BlockSpec((1, D_MODEL), const), pl.BlockSpec((1, D_MODEL), const),
                  pl.BlockSpec((LANES, LANES), const)],
        out_specs=tile,
        out_shape=jax.ShapeDtypeStruct((T, D_MODEL), bf16),
        compiler_params=_cparams(("parallel",)),
        name="rwkv_post",
    )(y, bonus, g, lnw, lnb, ones_blk)


def _rope_tables(pos):
    half = QK_ROPE // 2
    inv = 1.0 / (ROPE_THETA ** (jnp.arange(0, QK_ROPE, 2, dtype=f32) / QK_ROPE))
    ang = pos[:, None] * inv[None, :]
    cos, sin = jnp.cos(ang), jnp.sin(ang)
    n = pos.shape[0]
    ones = jnp.ones((n, QK_NOPE), f32)
    z = lambda w: jnp.zeros((n, w), f32)
    tab_c = jnp.concatenate([ones, cos, cos, z(HEAD_TILE - QK_NOPE - QK_ROPE)], axis=1)
    tab_s1 = jnp.concatenate([z(QK_NOPE), -sin, z(HEAD_TILE - QK_NOPE - half)], axis=1)
    tab_s2 = jnp.concatenate([z(QK_NOPE + half), sin, z(HEAD_TILE - QK_NOPE - QK_ROPE)], axis=1)
    return tab_c, tab_s1, tab_s2


def _pick_tile(n, pref):
    t = min(n, pref)
    while n % t:
        t //= 2
    return t


def kernel(x_prompt, x_sample, cache_ckv, cache_kpe, state_wkv, state_shift, page_table,
           norm_mix, norm_ffn, norm_final,
           mla_w_a, mla_q_norm, mla_kv_norm, mla_w_uq, mla_w_ukv, mla_w_o,
           rw_mix, rw_w_rkv, rw_w0, rw_w1, rw_w2, rw_a0, rw_a1, rw_a2, rw_g1, rw_g2,
           rw_k_k, rw_k_a, rw_r_k, rw_ln_w, rw_ln_b, rw_w_o,
           ffn_w_in, ffn_w_out):
    B, S, D = x_prompt.shape
    DB, DS, _ = x_sample.shape
    assert D == D_MODEL and DS == 1 and S % WKV_CHUNK == 0
    n_pages = page_table.shape[1]
    past_len = n_pages * PAGE_SIZE
    H = MLA_HEADS
    row = lambda t: t.reshape(1, -1)
    xp = x_prompt.reshape(B * S, D)
    xs = x_sample.reshape(DB, D)

    w_a = mla_w_a[0]
    pad_pe = jnp.zeros((D, HEAD_TILE), f32).at[:, QK_NOPE:QK_NOPE + QK_ROPE].set(w_a[:, Q_LORA + KV_LORA:])
    wa_ext = jnp.concatenate([w_a[:, :Q_LORA + KV_LORA], pad_pe], axis=1).astype(bf16)
    wuq = (mla_w_uq[0] * (MLA_SCALE * LOG2_E)).reshape(Q_LORA, H, QK_NOPE + QK_ROPE)
    wuq = jnp.pad(wuq, ((0, 0), (0, 0), (0, HEAD_TILE - QK_NOPE - QK_ROPE))).reshape(Q_LORA, H * HEAD_TILE).astype(bf16)
    wukv = mla_w_ukv[0].reshape(KV_LORA, H * HEAD_TILE).astype(bf16)
    mla_wo = mla_w_o[0].astype(bf16)
    qn, kvn = row(mla_q_norm[0]), row(mla_kv_norm[0])
    ffn_in = ffn_w_in.astype(bf16)
    ffn_out = ffn_w_out.astype(bf16)
    hh = jnp.arange(LANES) // RWKV_HEAD
    ones_blk = (hh[:, None] == hh[None, :]).astype(bf16)
    rwp = dict(
        gmix=row(norm_mix[1]), mix=jnp.pad(rw_mix[0], ((0, 2), (0, 0))), w_rkv=rw_w_rkv[0].astype(bf16),
        w0=row(rw_w0[0]), w1=rw_w1[0].astype(bf16), w2=rw_w2[0].astype(bf16),
        a0=row(rw_a0[0]), a1=rw_a1[0].astype(bf16), a2=rw_a2[0].astype(bf16),
        g1=rw_g1[0].astype(bf16), g2=rw_g2[0].astype(bf16),
        k_k=row(rw_k_k[0]), k_a=row(rw_k_a[0]), r_k=row(rw_r_k[0]), ones_blk=ones_blk)
    rw_wo = rw_w_o[0].astype(bf16)
    lnw, lnb = row(rw_ln_w[0]), row(rw_ln_b[0])
    gfin = row(norm_final)

    tm_p = _pick_tile(B * S, 512)
    tm_s = _pick_tile(DB, 128)
    th = FFN_HIDDEN // 2

    tabs_p = _rope_tables(jnp.arange(S, dtype=f32))
    tabs_s = _rope_tables(jnp.full((tm_s,), past_len, f32))
    q_s, _, _, ckv_s, kpe_s = _mla_proj(xs, row(norm_mix[0]), wa_ext, qn, kvn, wuq, wukv, tabs_s, tm_s)
    qlat = jnp.swapaxes(_absorb_q(q_s, wukv), 0, 1)
    qpe = q_s.reshape(DB, H, HEAD_TILE)[:, :, QK_NOPE:QK_NOPE + QK_ROPE]
    o_lat = _decode_attention(page_table, qlat, qpe, ckv_s.reshape(DB, 1, KV_LORA), kpe_s.reshape(DB, 1, QK_ROPE),
                              cache_ckv.reshape(-1, PAGE_SIZE, KV_LORA),
                              jnp.swapaxes(cache_kpe.reshape(-1, PAGE_SIZE, QK_ROPE), 1, 2))
    o_s = _unabsorb_o(jnp.swapaxes(o_lat, 0, 1), wukv)
    o_s = o_s.reshape(DB, H, HEAD_TILE)[:, :, QK_NOPE:].reshape(DB, H * V_HEAD).astype(bf16)
    xs = _proj_ffn(xs, o_s, mla_wo, row(norm_ffn[0]), ffn_in[0], ffn_out[0], gfin, False, tm_s, th)

    q_p, k_p, kv_p, ckv_p, kpe_p = _mla_proj(xp, row(norm_mix[0]), wa_ext, qn, kvn, wuq, wukv, tabs_p,
                                             _pick_tile(S, 512))
    o_p = _flash_attention(q_p, k_p, kv_p, B, S)
    xp = _proj_ffn(xp, o_p, mla_wo, row(norm_ffn[0]), ffn_in[0], ffn_out[0], gfin, False, tm_p, th)

    tm_r = _pick_tile(S, 512)
    r, k, v, kn, b, ld, g, bonus, hl = _rwkv_proj(xp, xp, rwp, S, tm_r, bf16)
    shift_p = hl.reshape(B, S // tm_r, 8, D)[:, -1, 7, :]
    y, st = _wkv_chunked(r, k, v, kn, b, ld, B, S)
    wkv_p = jnp.transpose(st.reshape(B, RWKV_HEAD, RWKV_HEADS, RWKV_HEAD), (0, 2, 3, 1))
    yo = _rwkv_post(y, bonus, g, lnw, lnb, ones_blk, tm_p)
    y_prompt = _proj_ffn(xp, yo, rw_wo, row(norm_ffn[1]), ffn_in[1], ffn_out[1], gfin, True, tm_p, th)

    r, k, v, kn, b, ld, g, bonus, shift_s = _rwkv_proj(xs, state_shift[0], rwp, 1, tm_s, f32)
    vecs = jnp.transpose(jnp.stack([r, k, v, kn, b, ld]), (0, 2, 1)).reshape(6, RWKV_HEADS, RWKV_HEAD, DB)
    wkv_t, y_t = _wkv_step(jnp.transpose(state_wkv[0].astype(f32), (1, 2, 3, 0)), vecs)
    wkv_s = jnp.transpose(wkv_t, (3, 0, 1, 2)).astype(state_wkv.dtype)
    y = jnp.transpose(y_t.reshape(D, DB))
    yo = _rwkv_post(y, bonus, g, lnw, lnb, ones_blk, tm_s)
    y_sample = _proj_ffn(xs, yo, rw_wo, row(norm_ffn[1]), ffn_in[1], ffn_out[1], gfin, True, tm_s, th)

    return (y_prompt.reshape(B, S, D), y_sample.reshape(DB, DS, D),
            ckv_p.reshape(1, B, S, KV_LORA), kpe_p.reshape(1, B, S, QK_ROPE),
            ckv_s.reshape(1, DB, DS, KV_LORA), kpe_s.reshape(1, DB, DS, QK_ROPE),
            wkv_p[None].astype(x_prompt.dtype), shift_p[None],
            wkv_s[None], shift_s[None])
```

```python
import functools
import math

import jax
import jax.numpy as jnp
from jax import lax
from jax.experimental import pallas as pl
from jax.experimental.pallas import tpu as pltpu

f32 = jnp.float32
bf16 = jnp.bfloat16

D_MODEL = 1024
MLA_HEADS = 16
QK_NOPE = 64
QK_ROPE = 32
V_HEAD = 64
Q_LORA = 512
KV_LORA = 256
ROPE_THETA = 10000.0
MLA_SCALE = 1.0 / math.sqrt(QK_NOPE + QK_ROPE)
LOG2_E = math.log2(math.e)
RWKV_HEAD = 64
RWKV_HEADS = D_MODEL // RWKV_HEAD
FFN_HIDDEN = 2816
NORM_EPS = 1e-6
GN_EPS = 64e-5
PAGE_SIZE = 128

LANES = 128
HEAD_TILE = 128
WKV_CHUNK = 64
WKV_GROUP = 4
WKV_CHUNKS_PER_STEP = 2
GROUP_W = WKV_GROUP * RWKV_HEAD
VMEM_LIMIT = 56 * 1024 * 1024


def _cparams(sem):
    return pltpu.CompilerParams(dimension_semantics=sem, vmem_limit_bytes=VMEM_LIMIT)


def _rms(x, g):
    return x * lax.rsqrt(jnp.mean(x * x, axis=-1, keepdims=True) + NORM_EPS) * g


def _dot(a, b):
    return jnp.dot(a, b, preferred_element_type=f32)


def _dot_nt(a, b):
    return lax.dot_general(a, b, (((1,), (1,)), ((), ())), preferred_element_type=f32)


def _dot_tn(a, b):
    return lax.dot_general(a, b, (((0,), (0,)), ((), ())), preferred_element_type=f32)


def _mla_proj_kernel(x_ref, gmix_ref, wa_ref, qn_ref, kvn_ref, wuq_ref, wukv_ref,
                     c_ref, s1_ref, s2_ref, q_ref, k_ref, kv_ref, ckv_ref, kpe_ref):
    tm = x_ref.shape[0]
    h = _rms(x_ref[...], gmix_ref[...]).astype(bf16)
    a = _dot(h, wa_ref[...])
    cq = _rms(a[:, :Q_LORA], qn_ref[...]).astype(bf16)
    ckv = _rms(a[:, Q_LORA:Q_LORA + KV_LORA], kvn_ref[...])
    ckv_ref[...] = ckv
    cos = c_ref[...]
    sin_lo = s1_ref[...]
    sin_hi = s2_ref[...]

    def rope(t):
        return t * cos + pltpu.roll(t, LANES - QK_ROPE // 2, 1) * sin_lo + pltpu.roll(t, QK_ROPE // 2, 1) * sin_hi

    kpe_t = rope(a[:, Q_LORA + KV_LORA:])
    kpe_ref[...] = kpe_t[:, QK_NOPE:QK_NOPE + QK_ROPE]
    ckv_b = ckv.astype(bf16)
    lane = lax.broadcasted_iota(jnp.int32, (tm, HEAD_TILE), 1)
    for pr in range(MLA_HEADS // 2):
        cols = slice(pr * 2 * HEAD_TILE, (pr + 1) * 2 * HEAD_TILE)
        kv = _dot(ckv_b, wukv_ref[:, cols])
        q = _dot(cq, wuq_ref[:, cols])
        kv_ref[:, cols] = kv.astype(bf16)
        for hh in range(2):
            sl = slice(hh * HEAD_TILE, (hh + 1) * HEAD_TILE)
            out = slice((2 * pr + hh) * HEAD_TILE, (2 * pr + hh + 1) * HEAD_TILE)
            q_ref[:, out] = rope(q[:, sl]).astype(bf16)
            k_ref[:, out] = jnp.where(lane < QK_NOPE, kv[:, sl], kpe_t).astype(bf16)


def _mla_proj(x, gmix, wa, qn, kvn, wuq, wukv, tabs, tm):
    T = x.shape[0]
    tab_c, tab_s1, tab_s2 = tabs
    nt = tab_c.shape[0] // tm
    HW = MLA_HEADS * HEAD_TILE
    const = lambda i: (0, 0)
    row = lambda i: (i, 0)
    tabmap = lambda i: (i % nt, 0)
    return pl.pallas_call(
        _mla_proj_kernel,
        grid=(T // tm,),
        in_specs=[
            pl.BlockSpec((tm, D_MODEL), row),
            pl.BlockSpec((1, D_MODEL), const),
            pl.BlockSpec(wa.shape, const),
            pl.BlockSpec((1, Q_LORA), const),
            pl.BlockSpec((1, KV_LORA), const),
            pl.BlockSpec(wuq.shape, const),
            pl.BlockSpec(wukv.shape, const),
            pl.BlockSpec((tm, HEAD_TILE), tabmap),
            pl.BlockSpec((tm, HEAD_TILE), tabmap),
            pl.BlockSpec((tm, HEAD_TILE), tabmap),
        ],
        out_specs=[
            pl.BlockSpec((tm, HW), row),
            pl.BlockSpec((tm, HW), row),
            pl.BlockSpec((tm, HW), row),
            pl.BlockSpec((tm, KV_LORA), row),
            pl.BlockSpec((tm, QK_ROPE), row),
        ],
        out_shape=[
            jax.ShapeDtypeStruct((T, HW), bf16),
            jax.ShapeDtypeStruct((T, HW), bf16),
            jax.ShapeDtypeStruct((T, HW), bf16),
            jax.ShapeDtypeStruct((T, KV_LORA), f32),
            jax.ShapeDtypeStruct((T, QK_ROPE), f32),
        ],
        compiler_params=_cparams(("parallel",)),
        name="mla_proj",
    )(x, gmix, wa, qn, kvn, wuq, wukv, tab_c, tab_s1, tab_s2)


def _flash_kernel(q_ref, k_ref, kv_ref, o_ref, vt_ref, m_ref, acc_ref, *, tile):
    S = q_ref.shape[0]
    nt = S // tile
    lane = lax.broadcasted_iota(jnp.int32, (tile, HEAD_TILE), 1)
    ones_row = lax.broadcasted_iota(jnp.int32, (HEAD_TILE, tile), 0) == 0
    heads = range(q_ref.shape[1] // HEAD_TILE)
    hs = [slice(hh * HEAD_TILE, (hh + 1) * HEAD_TILE) for hh in heads]

    for hh in heads:
        for j in range(nt):
            vt = kv_ref[j * tile:(j + 1) * tile, hs[hh]].astype(f32).T
            vt_ref[hh, j] = jnp.where(ones_row, 1.0, vt).astype(bf16)

    def q_body(qi, c):
        q0 = qi * tile
        qs = [q_ref[pl.ds(q0, tile), hs[hh]] for hh in heads]
        for hh in heads:
            m_ref[hh] = jnp.full((1, tile), -jnp.inf, f32)
            acc_ref[hh] = jnp.zeros((HEAD_TILE, tile), f32)

        def block(ki, kr, qr, diag):
            k0 = ki * tile + kr.start
            nk = kr.stop - kr.start
            sts = [_dot_nt(k_ref[pl.ds(k0, nk), hs[hh]], qs[hh][qr]) for hh in heads]
            if diag:
                shape = (nk, qr.stop - qr.start)
                visible = lax.broadcasted_iota(jnp.int32, shape, 0) <= lax.broadcasted_iota(jnp.int32, shape, 1)
                sts = [jnp.where(visible, st, -jnp.inf) for st in sts]
            m_prevs = [m_ref[hh, :, qr] for hh in heads]
            m_news = [jnp.maximum(m_prevs[hh], jnp.max(sts[hh], axis=0, keepdims=True)) for hh in heads]
            ps = [jnp.exp2(sts[hh] - m_news[hh]) for hh in heads]
            alphas = [jnp.exp2(m_prevs[hh] - m_news[hh]) for hh in heads]
            pvs = [_dot(vt_ref[hh, ki, :, kr], ps[hh].astype(bf16)) for hh in heads]
            for hh in heads:
                m_ref[hh, :, qr] = m_news[hh]
                acc_ref[hh, :, qr] = alphas[hh] * acc_ref[hh, :, qr] + pvs[hh]

        for ki in range(qi):
            block(ki, slice(0, tile), slice(0, tile), False)
        half = tile // 2
        block(qi, slice(0, half), slice(0, tile), True)
        block(qi, slice(half, tile), slice(half, tile), True)
        accs = [acc_ref[hh] for hh in heads]
        outs = [(a / a[0:1, :]).T for a in accs]
        for pr in range(len(outs) // 2):
            o = jnp.where(lane < V_HEAD, pltpu.roll(outs[2 * pr], V_HEAD, 1), outs[2 * pr + 1])
            o_ref[pl.ds(q0, tile), pr * 2 * V_HEAD:(pr + 1) * 2 * V_HEAD] = o.astype(bf16)
        return c

    for qi in range(nt):
        q_body(qi, 0)


def _flash_attention(q, k, kv, B, S, tile=512, heads_per_step=4):
    T = B * S
    tile = min(tile, S)
    nh = heads_per_step
    blk = pl.BlockSpec((S, nh * HEAD_TILE), lambda b, hp: (b, hp))
    return pl.pallas_call(
        functools.partial(_flash_kernel, tile=tile),
        grid=(B, MLA_HEADS // nh),
        in_specs=[blk, blk, blk],
        out_specs=pl.BlockSpec((S, nh * V_HEAD), lambda b, hp: (b, hp)),
        out_shape=jax.ShapeDtypeStruct((T, MLA_HEADS * V_HEAD), bf16),
        scratch_shapes=[
            pltpu.VMEM((nh, S // tile, HEAD_TILE, tile), bf16),
            pltpu.VMEM((nh, 1, tile), f32),
            pltpu.VMEM((nh, HEAD_TILE, tile), f32),
        ],
        compiler_params=_cparams(("parallel", "parallel")),
        name="mla_flash",
    )(q, k, kv)


def _absorb_q_kernel(q_ref, w_ref, o_ref):
    lane = lax.broadcasted_iota(jnp.int32, q_ref.shape, 1)
    qn = jnp.where(lane < QK_NOPE, q_ref[...], jnp.zeros_like(q_ref[...]))
    o_ref[0] = _dot_nt(qn, w_ref[...]).astype(bf16)


def _absorb_q(q, wukv):
    DB = q.shape[0]
    return pl.pallas_call(
        _absorb_q_kernel,
        grid=(MLA_HEADS,),
        in_specs=[
            pl.BlockSpec((DB, HEAD_TILE), lambda h: (0, h)),
            pl.BlockSpec((KV_LORA, HEAD_TILE), lambda h: (0, h)),
        ],
        out_specs=pl.BlockSpec((1, DB, KV_LORA), lambda h: (h, 0, 0)),
        out_shape=jax.ShapeDtypeStruct((MLA_HEADS, DB, KV_LORA), bf16),
        compiler_params=_cparams(("parallel",)),
        name="mla_absorb_q",
    )(q, wukv)


def _unabsorb_o_kernel(o_ref, w_ref, out_ref):
    out_ref[...] = _dot(o_ref[0].astype(bf16), w_ref[...])


def _unabsorb_o(o_lat, wukv):
    DB = o_lat.shape[1]
    return pl.pallas_call(
        _unabsorb_o_kernel,
        grid=(MLA_HEADS,),
        in_specs=[
            pl.BlockSpec((1, DB, KV_LORA), lambda h: (h, 0, 0)),
            pl.BlockSpec((KV_LORA, HEAD_TILE), lambda h: (0, h)),
        ],
        out_specs=pl.BlockSpec((DB, HEAD_TILE), lambda h: (0, h)),
        out_shape=jax.ShapeDtypeStruct((DB, MLA_HEADS * HEAD_TILE), f32),
        compiler_params=_cparams(("parallel",)),
        name="mla_unabsorb_o",
    )(o_lat, wukv)


def _decode_kernel(pt_ref, qlat_ref, qpe_ref, cnew_ref, pnew_ref, ckv_hbm, kpe_hbm, o_ref,
                   ckv_buf, kpe_buf, sem, m_ref, l_ref, acc_ref, *, pages_per_step, chains):
    PP = pages_per_step
    NC = chains
    DB, n_pages = pt_ref.shape
    G = n_pages // PP
    per_chain = DB // NC
    total = per_chain * G

    def page_copy(page, slot, idx, which):
        if which == 0:
            return pltpu.make_async_copy(ckv_hbm.at[page], ckv_buf.at[slot, idx], sem.at[0, slot])
        return pltpu.make_async_copy(kpe_hbm.at[page], kpe_buf.at[slot, idx], sem.at[1, slot])

    def start_group(it, slot):
        bb = it // G
        g0 = (it % G) * PP
        for c in range(NC):
            for p in range(PP):
                page = pt_ref[c * per_chain + bb, g0 + p]
                page_copy(page, slot, c * PP + p, 0).start()
                page_copy(page, slot, c * PP + p, 1).start()

    def wait_group(slot):
        for idx in range(NC * PP):
            page_copy(0, slot, idx, 0).wait()
            page_copy(0, slot, idx, 1).wait()

    it = pl.program_id(0)
    slot = it % 2

    @pl.when(it == 0)
    def _():
        start_group(0, 0)

    @pl.when(it + 1 < total)
    def _():
        start_group(it + 1, 1 - slot)

    wait_group(slot)
    _decode_group(it // G, it % G, G, per_chain, slot, qlat_ref, qpe_ref, cnew_ref, pnew_ref, o_ref,
                  ckv_buf, kpe_buf, m_ref, l_ref, acc_ref, PP, NC)


def _decode_group(bb, g, G, per_chain, slot, qlat_ref, qpe_ref, cnew_ref, pnew_ref, o_ref,
                  ckv_buf, kpe_buf, m_ref, l_ref, acc_ref, PP, NC):
    chains = range(NC)
    bs = [c * per_chain + bb for c in chains]

    @pl.when(g == 0)
    def _():
        m_ref[...] = jnp.full(m_ref.shape, -jnp.inf, f32)
        l_ref[...] = jnp.zeros(l_ref.shape, f32)
        acc_ref[...] = jnp.zeros(acc_ref.shape, f32)

    qlat = [qlat_ref[bs[c]] for c in chains]
    qpe = [qpe_ref[bs[c]] for c in chains]
    cks = [[ckv_buf[slot, c * PP + p].astype(bf16) for p in range(PP)] for c in chains]
    s = [jnp.concatenate(
        [_dot_nt(qlat[c], cks[c][p]) + _dot(qpe[c], kpe_buf[slot, c * PP + p].astype(bf16)) for p in range(PP)],
        axis=1) for c in chains]
    m_prev = [m_ref[c] for c in chains]
    m_new = [jnp.maximum(m_prev[c], jnp.max(s[c], axis=-1, keepdims=True)) for c in chains]
    p_all = [jnp.exp2(s[c] - m_new[c]) for c in chains]
    alpha = [jnp.exp2(m_prev[c] - m_new[c]) for c in chains]
    for c in chains:
        l_ref[c] = alpha[c] * l_ref[c] + jnp.sum(p_all[c], axis=-1, keepdims=True)
        m_ref[c] = m_new[c]
    for c in chains:
        pv = _dot(p_all[c][:, :PAGE_SIZE].astype(bf16), cks[c][0])
        for p in range(1, PP):
            pv = pv + _dot(p_all[c][:, p * PAGE_SIZE:(p + 1) * PAGE_SIZE].astype(bf16), cks[c][p])
        acc_ref[c] = alpha[c] * acc_ref[c] + pv

    @pl.when(g == G - 1)
    def _():
        for c in chains:
            cnew = cnew_ref[bs[c]].astype(bf16).astype(f32)
            pnew = pnew_ref[bs[c]].astype(bf16).astype(f32)
            s_self = (jnp.sum(qlat[c].astype(f32) * cnew, axis=-1, keepdims=True)
                      + jnp.sum(qpe[c].astype(f32) * pnew, axis=-1, keepdims=True))
            m_last = m_ref[c]
            m_fin = jnp.maximum(m_last, s_self)
            p_self = jnp.exp2(s_self - m_fin)
            a_fin = jnp.exp2(m_last - m_fin)
            l_fin = a_fin * l_ref[c] + p_self
            o_ref[bs[c]] = (a_fin * acc_ref[c] + p_self * cnew) / l_fin


def _decode_attention(page_table, qlat, qpe, ckv_new, kpe_new, pool_ckv, pool_kpe_t, pages_per_step=16):
    DB, n_pages = page_table.shape
    PP = math.gcd(n_pages, pages_per_step)
    NC = 2 if DB % 2 == 0 else 1
    H = MLA_HEADS
    vmem = pl.BlockSpec(memory_space=pltpu.VMEM)
    hbm = pl.BlockSpec(memory_space=pl.ANY)
    grid_spec = pltpu.PrefetchScalarGridSpec(
        num_scalar_prefetch=1,
        grid=((DB // NC) * (n_pages // PP),),
        in_specs=[vmem, vmem, vmem, vmem, hbm, hbm],
        out_specs=vmem,
        scratch_shapes=[
            pltpu.VMEM((2, NC * PP, PAGE_SIZE, KV_LORA), pool_ckv.dtype),
            pltpu.VMEM((2, NC * PP, QK_ROPE, PAGE_SIZE), pool_kpe_t.dtype),
            pltpu.SemaphoreType.DMA((2, 2)),
            pltpu.VMEM((NC, H, 1), f32),
            pltpu.VMEM((NC, H, 1), f32),
            pltpu.VMEM((NC, H, KV_LORA), f32),
        ],
    )
    return pl.pallas_call(
        functools.partial(_decode_kernel, pages_per_step=PP, chains=NC),
        grid_spec=grid_spec,
        out_shape=jax.ShapeDtypeStruct((DB, H, KV_LORA), f32),
        compiler_params=_cparams(("arbitrary",)),
        name="mla_decode",
    )(page_table, qlat, qpe, ckv_new, kpe_new, pool_ckv, pool_kpe_t)


def _proj_ffn_kernel(x_ref, o_ref, wo_ref, gffn_ref, wg_ref, wu_ref, wout_ref, gfin_ref,
                     out_ref, acc_ref, h_ref, *, final_norm):
    j = pl.program_id(1)

    @pl.when(j == 0)
    def _():
        x1 = x_ref[...] + _dot(o_ref[...], wo_ref[...])
        acc_ref[...] = x1
        h_ref[...] = _rms(x1, gffn_ref[...]).astype(bf16)

    h = h_ref[...]
    gate = _dot(h, wg_ref[...])
    up = _dot(h, wu_ref[...])
    act = (gate * jax.nn.sigmoid(gate) * up).astype(bf16)
    acc_ref[...] += _dot(act, wout_ref[...])

    @pl.when(j == pl.num_programs(1) - 1)
    def _():
        y = acc_ref[...]
        if final_norm:
            y = _rms(y, gfin_ref[...])
        out_ref[...] = y


def _proj_ffn(x, o, wo, gffn, w_in, w_out, gfin, final_norm, tm, th):
    T = x.shape[0]
    nh = FFN_HIDDEN // th
    row = lambda i, j: (i, 0)
    const = lambda i, j: (0, 0)
    return pl.pallas_call(
        functools.partial(_proj_ffn_kernel, final_norm=final_norm),
        grid=(T // tm, nh),
        in_specs=[
            pl.BlockSpec((tm, D_MODEL), row),
            pl.BlockSpec((tm, D_MODEL), row),
            pl.BlockSpec((D_MODEL, D_MODEL), const),
            pl.BlockSpec((1, D_MODEL), const),
            pl.BlockSpec((D_MODEL, th), lambda i, j: (0, j)),
            pl.BlockSpec((D_MODEL, th), lambda i, j: (0, j + nh)),
            pl.BlockSpec((th, D_MODEL), lambda i, j: (j, 0)),
            pl.BlockSpec((1, D_MODEL), const),
        ],
        out_specs=pl.BlockSpec((tm, D_MODEL), row),
        out_shape=jax.ShapeDtypeStruct((T, D_MODEL), f32),
        scratch_shapes=[pltpu.VMEM((tm, D_MODEL), f32), pltpu.VMEM((tm, D_MODEL), bf16)],
        compiler_params=_cparams(("parallel", "arbitrary")),
        name="proj_ffn",
    )(x, o, wo, gffn, w_in, w_in, w_out, gfin)


def _seg_sum(x, ones_blk):
    parts = [_dot(x[:, gidx * LANES:(gidx + 1) * LANES].astype(bf16), ones_blk)
             for gidx in range(x.shape[1] // LANES)]
    return jnp.concatenate(parts, axis=1)


def _rwkv_proj_kernel(x_ref, prev_ref, gmix_ref, mix_ref, wrkv_ref, w0_ref, w1_ref, w2_ref,
                      a0_ref, a1_ref, a2_ref, g1_ref, g2_ref, kk_ref, ka_ref, rk_ref, ones_ref,
                      r_out, k_out, v_out, kn_out, b_out, ld_out, g_out, bonus_out, h_out,
                      *, seq_mode, tiles_per_seq):
    tm = x_ref.shape[0]
    gm = gmix_ref[...]
    h = _rms(x_ref[...], gm)
    if seq_mode:
        hp_row = _rms(prev_ref[...], gm)[7:8, :]
        is_start = (pl.program_id(0) % tiles_per_seq) == 0
        hp_row = jnp.where(is_start, jnp.zeros_like(hp_row), hp_row)
        rowid = lax.broadcasted_iota(jnp.int32, (tm, 1), 0)
        hprev = jnp.where(rowid == 0, hp_row, pltpu.roll(h, 1, 0))
        h_out[0] = h[tm - 8:, :]
    else:
        hprev = prev_ref[...]
        h_out[...] = h
    xx = hprev - h
    mix = mix_ref[...]
    xr, xw, xk, xv, xa, xg = ((h + xx * mix[n:n + 1]).astype(bf16) for n in range(6))
    w_hid = jnp.tanh(_dot(xw, w1_ref[...])).astype(bf16)
    a_hid = _dot(xa, a1_ref[...]).astype(bf16)
    g_hid = jax.nn.sigmoid(_dot(xg, g1_ref[...])).astype(bf16)
    ones_blk = ones_ref[...]
    ct = 2 * LANES
    for j in range(D_MODEL // ct):
        cols = slice(j * ct, (j + 1) * ct)
        r = _dot(xr, wrkv_ref[0, :, cols])
        k = _dot(xk, wrkv_ref[1, :, cols])
        v = _dot(xv, wrkv_ref[2, :, cols])
        wl = w0_ref[:, cols] + _dot(w_hid, w2_ref[:, cols])
        z = -wl
        w_log = -(jnp.maximum(z, 0.0) + jnp.log(1.0 + jnp.exp(-jnp.abs(z)))) - 0.5
        ld_out[:, cols] = -jnp.exp(w_log)
        a = jax.nn.sigmoid(a0_ref[:, cols] + _dot(a_hid, a2_ref[:, cols]))
        g_out[:, cols] = _dot(g_hid, g2_ref[:, cols]).astype(g_out.dtype)
        kk = k * kk_ref[:, cols]
        kk = kk / jnp.maximum(jnp.sqrt(_seg_sum(kk * kk, ones_blk)), 1e-12)
        k = k * (1.0 + (a - 1.0) * ka_ref[:, cols])
        bonus_out[:, cols] = (_seg_sum(r * k * rk_ref[:, cols], ones_blk) * v).astype(bonus_out.dtype)
        r_out[:, cols] = r.astype(r_out.dtype)
        k_out[:, cols] = k.astype(k_out.dtype)
        v_out[:, cols] = v.astype(v_out.dtype)
        kn_out[:, cols] = kk.astype(kn_out.dtype)
        b_out[:, cols] = (kk * a).astype(b_out.dtype)


def _rwkv_proj(x, prev, p, seq_len, tm, vec_dtype):
    T = x.shape[0]
    seq_mode = seq_len > 1
    tiles_per_seq = max(seq_len // tm, 1)
    row = lambda i: (i, 0)
    const = lambda i: (0, 0)
    const3 = lambda i: (0, 0, 0)
    if seq_mode:
        prev_spec = pl.BlockSpec((8, D_MODEL), lambda i: (jnp.maximum(i * (tm // 8) - 1, 0), 0))
        h_spec = pl.BlockSpec((1, 8, D_MODEL), lambda i: (i, 0, 0))
        h_shape = jax.ShapeDtypeStruct((T // tm, 8, D_MODEL), f32)
    else:
        prev_spec = pl.BlockSpec((tm, D_MODEL), row)
        h_spec = pl.BlockSpec((tm, D_MODEL), row)
        h_shape = jax.ShapeDtypeStruct((T, D_MODEL), f32)
    vec = lambda dt: jax.ShapeDtypeStruct((T, D_MODEL), dt)
    tile = pl.BlockSpec((tm, D_MODEL), row)
    full = lambda arr: pl.BlockSpec(arr.shape, const3 if arr.ndim == 3 else const)
    weights = [p["gmix"], p["mix"], p["w_rkv"], p["w0"], p["w1"], p["w2"], p["a0"], p["a1"], p["a2"],
               p["g1"], p["g2"], p["k_k"], p["k_a"], p["r_k"], p["ones_blk"]]
    return pl.pallas_call(
        functools.partial(_rwkv_proj_kernel, seq_mode=seq_mode, tiles_per_seq=tiles_per_seq),
        grid=(T // tm,),
        in_specs=[tile, prev_spec] + [full(w) for w in weights],
        out_specs=[tile] * 8 + [h_spec],
        out_shape=[vec(vec_dtype)] * 5 + [vec(f32), vec(bf16), vec(bf16), h_shape],
        compiler_params=_cparams(("parallel",)),
        name="rwkv_proj",
    )(x, prev, *weights)


def _wkv_chunk_kernel(r_ref, k_ref, v_ref, kn_ref, b_ref, ld_ref, y_ref, st_ref, m_ref, *, chunks):
    C = WKV_CHUNK
    G = WKV_GROUP
    W = GROUP_W
    c_idx = pl.program_id(1)

    @pl.when(c_idx == 0)
    def _():
        m_ref[...] = jnp.zeros(m_ref.shape, f32)

    rr = lax.broadcasted_iota(jnp.int32, (W, W), 0)
    cc = lax.broadcasted_iota(jnp.int32, (W, W), 1)
    bd_mask = (rr // RWKV_HEAD) == (cc // RWKV_HEAD)
    t_idx = lax.broadcasted_iota(jnp.int32, (C, W), 0)
    s_idx = lax.broadcasted_iota(jnp.int32, (C, W), 1) % C
    strict = s_idx < t_idx
    incl = s_idx <= t_idx
    eye_cat = (s_idx == t_idx).astype(f32)
    lane_lo = lax.broadcasted_iota(jnp.int32, (C, LANES), 1) < RWKV_HEAD
    zero_tile = jnp.zeros((C, LANES), bf16)

    def bd(y):
        yb = y.astype(bf16)
        rows = []
        for hh in range(G):
            lt = hh // 2
            t = yb[:, lt * LANES:(lt + 1) * LANES]
            keep = jnp.where(lane_lo, t, zero_tile) if hh % 2 == 0 else jnp.where(lane_lo, zero_tile, t)
            tiles = [zero_tile] * (W // LANES)
            tiles[lt] = keep
            rows.append(jnp.concatenate(tiles, axis=1))
        return jnp.concatenate(rows, axis=0)

    def bdmm(x, y):
        return _dot(x.astype(bf16), bd(y))

    NG = r_ref.shape[1] // W
    streams = [(ci, gi) for ci in range(chunks) for gi in range(NG)]
    ns = range(len(streams))

    def load(ref, s):
        ci, gi = streams[s]
        return ref[ci * C:(ci + 1) * C, gi * W:(gi + 1) * W]

    def cumsum_rows(x):
        sh = 1
        while sh < C:
            x = x + jnp.where(t_idx >= sh, pltpu.roll(x, sh, 0), 0.0)
            sh *= 2
        return x

    ld = [load(ld_ref, s) for s in ns]
    cs = [cumsum_rows(ld[s]) for s in ns]
    tot_col = [jnp.broadcast_to(cs[s][C - 1:C, :], (LANES, W)).T for s in ns]
    r = [load(r_ref, s).astype(f32) for s in ns]
    k = [load(k_ref, s).astype(f32) for s in ns]
    v = [load(v_ref, s).astype(f32) for s in ns]
    kn = [load(kn_ref, s).astype(f32) for s in ns]
    b = [load(b_ref, s).astype(f32) for s in ns]
    p_inv = [jnp.exp(-cs[s]) for s in ns]
    p_rest = [jnp.exp(cs[s][C - 1:C, :] - cs[s]) for s in ns]
    a_t = [-kn[s] * jnp.exp(cs[s] - ld[s]) for s in ns]
    r_t = [r[s] * jnp.exp(cs[s]) for s in ns]
    lhs = [jnp.concatenate([a_t[s], r_t[s]], axis=0).astype(bf16) for s in ns]
    s_b = [_dot_nt(lhs[s], bd(b[s] * p_inv[s])) for s in ns]
    s_k = [_dot_nt(lhs[s], bd(k[s] * p_inv[s])) for s in ns]
    l_ab = [jnp.where(strict, s_b[s][:C], 0.0) for s in ns]
    l_ak = [jnp.where(strict, s_k[s][:C], 0.0) for s in ns]
    a_rb = [jnp.where(incl, s_b[s][C:], 0.0) for s in ns]
    a_rk = [jnp.where(incl, s_k[s][C:], 0.0) for s in ns]
    t_inv = [eye_cat + l_ab[s] for s in ns]
    pw = [bdmm(l_ab[s], l_ab[s]) for s in ns]
    step = 2
    while step < C:
        rhs = [bd(pw[s]) for s in ns]
        if 2 * step < C:
            both = [_dot(jnp.concatenate([pw[s], t_inv[s]], axis=0).astype(bf16), rhs[s]) for s in ns]
            pw = [both[s][:C] for s in ns]
            t_inv = [t_inv[s] + both[s][C:] for s in ns]
        else:
            t_inv = [t_inv[s] + _dot(t_inv[s].astype(bf16), rhs[s]) for s in ns]
        step *= 2
    from_v = [bdmm(jnp.concatenate([l_ak[s], a_rk[s]], axis=0), v[s]) for s in ns]
    y_v = [from_v[s][C:] for s in ns]
    bk_rest = [jnp.concatenate([b[s] * p_rest[s], k[s] * p_rest[s]], axis=0).astype(bf16) for s in ns]
    decay_col = [jnp.exp(jnp.concatenate([tot_col[s]] * (W // LANES), axis=1)) for s in ns]

    m = [m_ref[gi] for gi in range(NG)]
    for ci in range(chunks):
        ss = [ci * NG + gi for gi in range(NG)]
        from_state = [_dot(lhs[s], m[gi].astype(bf16)) for gi, s in enumerate(ss)]
        u = [bdmm(t_inv[s], from_state[gi][:C] + from_v[s][:C]) for gi, s in enumerate(ss)]
        y_u = [bdmm(a_rb[s], u[gi]) for gi, s in enumerate(ss)]
        upd = [_dot_tn(bk_rest[s], jnp.concatenate([u[gi], v[s]], axis=0).astype(bf16)) for gi, s in enumerate(ss)]
        for gi, s in enumerate(ss):
            y_ref[ci * C:(ci + 1) * C, gi * W:(gi + 1) * W] = from_state[gi][C:] + y_u[gi] + y_v[s]
        m = [m[gi] * decay_col[s] + jnp.where(bd_mask, upd[gi], 0.0) for gi, s in enumerate(ss)]
    for gi in range(NG):
        m_ref[gi] = m[gi]
        acc = m[gi][:RWKV_HEAD]
        for hh in range(1, G):
            acc = acc + m[gi][hh * RWKV_HEAD:(hh + 1) * RWKV_HEAD]
        st_ref[0, :, gi * W:(gi + 1) * W] = acc


def _wkv_chunked(r, k, v, kn, b, ld, B, S):
    T = B * S
    chunks = WKV_CHUNKS_PER_STEP if S % (WKV_CHUNK * WKV_CHUNKS_PER_STEP) == 0 else 1
    rows = WKV_CHUNK * chunks
    nc = S // rows
    tile = pl.BlockSpec((rows, D_MODEL), lambda bi, ci: (bi * nc + ci, 0))
    return pl.pallas_call(
        functools.partial(_wkv_chunk_kernel, chunks=chunks),
        grid=(B, nc),
        in_specs=[tile] * 6,
        out_specs=[tile, pl.BlockSpec((1, RWKV_HEAD, D_MODEL), lambda bi, ci: (bi, 0, 0))],
        out_shape=[jax.ShapeDtypeStruct((T, D_MODEL), f32),
                   jax.ShapeDtypeStruct((B, RWKV_HEAD, D_MODEL), f32)],
        scratch_shapes=[pltpu.VMEM((D_MODEL // GROUP_W, GROUP_W, GROUP_W), f32)],
        compiler_params=_cparams(("parallel", "arbitrary")),
        name="wkv_chunked",
    )(r, k, v, kn, b, ld)


def _wkv_step_kernel(s_ref, vec_ref, snew_ref, y_ref):
    r, k, kn, b, ld = (vec_ref[i, 0] for i in (0, 1, 3, 4, 5))
    a = -kn
    w = jnp.exp(ld)

    def value_row(vi, c):
        st = s_ref[0, vi]
        sa = jnp.sum(st * a, axis=0, keepdims=True)
        v_row = vec_ref[2, 0, pl.ds(vi, 1), :]
        st = st * w + sa * b + v_row * k
        snew_ref[0, vi] = st
        y_ref[0, pl.ds(vi, 1), :] = jnp.sum(st * r, axis=0, keepdims=True)
        return c

    lax.fori_loop(0, RWKV_HEAD, value_row, 0, unroll=4)


def _wkv_step(state_t, vecs):
    H, N, _, DB = state_t.shape
    st = pl.BlockSpec((1, N, N, DB), lambda h: (h, 0, 0, 0))
    return pl.pallas_call(
        _wkv_step_kernel,
        grid=(H,),
        in_specs=[st, pl.BlockSpec((6, 1, N, DB), lambda h: (0, h, 0, 0))],
        out_specs=[st, pl.BlockSpec((1, N, DB), lambda h: (h, 0, 0))],
        out_shape=[jax.ShapeDtypeStruct(state_t.shape, f32),
                   jax.ShapeDtypeStruct((H, N, DB), f32)],
        compiler_params=_cparams(("parallel",)),
        name="wkv_step",
    )(state_t, vecs)


def _rwkv_post_kernel(y_ref, bonus_ref, g_ref, lnw_ref, lnb_ref, ones_ref, o_ref):
    ones_blk = ones_ref[...]
    y = y_ref[...]
    inv_n = 1.0 / RWKV_HEAD
    mu = _seg_sum(y, ones_blk) * inv_n
    d = y - mu
    var = _seg_sum(d * d, ones_blk) * inv_n
    yn = d * lax.rsqrt(var + GN_EPS) * lnw_ref[...] + lnb_ref[...]
    o_ref[...] = ((yn + bonus_ref[...].astype(f32)) * g_ref[...].astype(f32)).astype(bf16)


def _rwkv_post(y, bonus, g, lnw, lnb, ones_blk, tm):
    T = y.shape[0]
    row = lambda i: (i, 0)
    const = lambda i: (0, 0)
    tile = pl.BlockSpec((tm, D_MODEL), row)
    return pl.pallas_call(
        _rwkv_post_kernel,
        grid=(T // tm,),
        in_specs=[tile, tile, tile, pl.BlockSpec((1, D_MODEL), const), pl.BlockSpec((1, D_MODEL), const),
                  pl.BlockSpec((LANES, LANES), const)],
        out_specs=tile,
        out_shape=jax.ShapeDtypeStruct((T, D_MODEL), bf16),
        compiler_params=_cparams(("parallel",)),
        name="rwkv_post",
    )(y, bonus, g, lnw, lnb, ones_blk)


def _rope_tables(pos):
    half = QK_ROPE // 2
    inv = 1.0 / (ROPE_THETA ** (jnp.arange(0, QK_ROPE, 2, dtype=f32) / QK_ROPE))
    ang = pos[:, None] * inv[None, :]
    cos, sin = jnp.cos(ang), jnp.sin(ang)
    n = pos.shape[0]
    ones = jnp.ones((n, QK_NOPE), f32)
    z = lambda w: jnp.zeros((n, w), f32)
    tab_c = jnp.concatenate([ones, cos, cos, z(HEAD_TILE - QK_NOPE - QK_ROPE)], axis=1)
    tab_s1 = jnp.concatenate([z(QK_NOPE), -sin, z(HEAD_TILE - QK_NOPE - half)], axis=1)
    tab_s2 = jnp.concatenate([z(QK_NOPE + half), sin, z(HEAD_TILE - QK_NOPE - QK_ROPE)], axis=1)
    return tab_c, tab_s1, tab_s2


def _pick_tile(n, pref):
    t = min(n, pref)
    while n % t:
        t //= 2
    return t


def kernel(x_prompt, x_sample, cache_ckv, cache_kpe, state_wkv, state_shift, page_table,
           norm_mix, norm_ffn, norm_final,
           mla_w_a, mla_q_norm, mla_kv_norm, mla_w_uq, mla_w_ukv, mla_w_o,
           rw_mix, rw_w_rkv, rw_w0, rw_w1, rw_w2, rw_a0, rw_a1, rw_a2, rw_g1, rw_g2,
           rw_k_k, rw_k_a, rw_r_k, rw_ln_w, rw_ln_b, rw_w_o,
           ffn_w_in, ffn_w_out):
    B, S, D = x_prompt.shape
    DB, DS, _ = x_sample.shape
    assert D == D_MODEL and DS == 1 and S % WKV_CHUNK == 0
    n_pages = page_table.shape[1]
    past_len = n_pages * PAGE_SIZE
    H = MLA_HEADS
    row = lambda t: t.reshape(1, -1)
    xp = x_prompt.reshape(B * S, D)
    xs = x_sample.reshape(DB, D)

    w_a = mla_w_a[0]
    pad_pe = jnp.zeros((D, HEAD_TILE), f32).at[:, QK_NOPE:QK_NOPE + QK_ROPE].set(w_a[:, Q_LORA + KV_LORA:])
    wa_ext = jnp.concatenate([w_a[:, :Q_LORA + KV_LORA], pad_pe], axis=1).astype(bf16)
    wuq = (mla_w_uq[0] * (MLA_SCALE * LOG2_E)).reshape(Q_LORA, H, QK_NOPE + QK_ROPE)
    wuq = jnp.pad(wuq, ((0, 0), (0, 0), (0, HEAD_TILE - QK_NOPE - QK_ROPE))).reshape(Q_LORA, H * HEAD_TILE).astype(bf16)
    wukv = mla_w_ukv[0].reshape(KV_LORA, H * HEAD_TILE).astype(bf16)
    mla_wo = mla_w_o[0].astype(bf16)
    qn, kvn = row(mla_q_norm[0]), row(mla_kv_norm[0])
    ffn_in = ffn_w_in.astype(bf16)
    ffn_out = ffn_w_out.astype(bf16)
    hh = jnp.arange(LANES) // RWKV_HEAD
    ones_blk = (hh[:, None] == hh[None, :]).astype(bf16)
    rwp = dict(
        gmix=row(norm_mix[1]), mix=jnp.pad(rw_mix[0], ((0, 2), (0, 0))), w_rkv=rw_w_rkv[0].astype(bf16),
        w0=row(rw_w0[0]), w1=rw_w1[0].astype(bf16), w2=rw_w2[0].astype(bf16),
        a0=row(rw_a0[0]), a1=rw_a1[0].astype(bf16), a2=rw_a2[0].astype(bf16),
        g1=rw_g1[0].astype(bf16), g2=rw_g2[0].astype(bf16),
        k_k=row(rw_k_k[0]), k_a=row(rw_k_a[0]), r_k=row(rw_r_k[0]), ones_blk=ones_blk)
    rw_wo = rw_w_o[0].astype(bf16)
    lnw, lnb = row(rw_ln_w[0]), row(rw_ln_b[0])
    gfin = row(norm_final)

    tm_p = _pick_tile(B * S, 512)
    tm_s = _pick_tile(DB, 128)
    th = FFN_HIDDEN // 2

    tabs_p = _rope_tables(jnp.arange(S, dtype=f32))
    tabs_s = _rope_tables(jnp.full((tm_s,), past_len, f32))
    q_s, _, _, ckv_s, kpe_s = _mla_proj(xs, row(norm_mix[0]), wa_ext, qn, kvn, wuq, wukv, tabs_s, tm_s)
    qlat = jnp.swapaxes(_absorb_q(q_s, wukv), 0, 1)
    qpe = q_s.reshape(DB, H, HEAD_TILE)[:, :, QK_NOPE:QK_NOPE + QK_ROPE]
    o_lat = _decode_attention(page_table, qlat, qpe, ckv_s.reshape(DB, 1, KV_LORA), kpe_s.reshape(DB, 1, QK_ROPE),
                              cache_ckv.reshape(-1, PAGE_SIZE, KV_LORA),
                              jnp.swapaxes(cache_kpe.reshape(-1, PAGE_SIZE, QK_ROPE), 1, 2))
    o_s = _unabsorb_o(jnp.swapaxes(o_lat, 0, 1), wukv)
    o_s = o_s.reshape(DB, H, HEAD_TILE)[:, :, QK_NOPE:].reshape(DB, H * V_HEAD).astype(bf16)
    xs = _proj_ffn(xs, o_s, mla_wo, row(norm_ffn[0]), ffn_in[0], ffn_out[0], gfin, False, tm_s, th)

    q_p, k_p, kv_p, ckv_p, kpe_p = _mla_proj(xp, row(norm_mix[0]), wa_ext, qn, kvn, wuq, wukv, tabs_p,
                                             _pick_tile(S, 512))
    o_p = _flash_attention(q_p, k_p, kv_p, B, S)
    xp = _proj_ffn(xp, o_p, mla_wo, row(norm_ffn[0]), ffn_in[0], ffn_out[0], gfin, False, tm_p, th)

    tm_r = _pick_tile(S, 512)
    r, k, v, kn, b, ld, g, bonus, hl = _rwkv_proj(xp, xp, rwp, S, tm_r, bf16)
    shift_p = hl.reshape(B, S // tm_r, 8, D)[:, -1, 7, :]
    y, st = _wkv_chunked(r, k, v, kn, b, ld, B, S)
    wkv_p = jnp.transpose(st.reshape(B, RWKV_HEAD, RWKV_HEADS, RWKV_HEAD), (0, 2, 3, 1))
    yo = _rwkv_post(y, bonus, g, lnw, lnb, ones_blk, tm_p)
    y_prompt = _proj_ffn(xp, yo, rw_wo, row(norm_ffn[1]), ffn_in[1], ffn_out[1], gfin, True, tm_p, th)

    r, k, v, kn, b, ld, g, bonus, shift_s = _rwkv_proj(xs, state_shift[0], rwp, 1, tm_s, f32)
    vecs = jnp.transpose(jnp.stack([r, k, v, kn, b, ld]), (0, 2, 1)).reshape(6, RWKV_HEADS, RWKV_HEAD, DB)
    wkv_t, y_t = _wkv_step(jnp.transpose(state_wkv[0].astype(f32), (1, 2, 3, 0)), vecs)
    wkv_s = jnp.transpose(wkv_t, (3, 0, 1, 2)).astype(state_wkv.dtype)
    y = jnp.transpose(y_t.reshape(D, DB))
    yo = _rwkv_post(y, bonus, g, lnw, lnb, ones_blk, tm_s)
    y_sample = _proj_ffn(xs, yo, rw_wo, row(norm_ffn[1]), ffn_in[1], ffn_out[1], gfin, True, tm_s, th)

    return (y_prompt.reshape(B, S, D), y_sample.reshape(DB, DS, D),
            ckv_p.reshape(1, B, S, KV_LORA), kpe_p.reshape(1, B, S, QK_ROPE),
            ckv_s.reshape(1, DB, DS, KV_LORA), kpe_s.reshape(1, DB, DS, QK_ROPE),
            wkv_p[None].astype(x_prompt.dtype), shift_p[None],
            wkv_s[None], shift_s[None])
```

```python
import functools
import itertools
import math

import jax
import jax.numpy as jnp
from jax import lax
from jax.experimental import pallas as pl
from jax.experimental.pallas import tpu as pltpu

f32 = jnp.float32
bf16 = jnp.bfloat16

D_MODEL = 1024
MLA_HEADS = 16
QK_NOPE = 64
QK_ROPE = 32
V_HEAD = 64
Q_LORA = 512
KV_LORA = 256
ROPE_THETA = 10000.0
MLA_SCALE = 1.0 / math.sqrt(QK_NOPE + QK_ROPE)
LOG2_E = math.log2(math.e)
RWKV_HEAD = 64
RWKV_HEADS = D_MODEL // RWKV_HEAD
FFN_HIDDEN = 2816
FFN_TILE = 256
NORM_EPS = 1e-6
GN_EPS = 64e-5
PAGE_SIZE = 128

LANES = 128
HEAD_TILE = 128
WKV_CHUNK = 64
WKV_GROUP = 4
WKV_CHUNKS_PER_STEP = 4
WKV_WAVE = 2
GROUP_W = WKV_GROUP * RWKV_HEAD
VMEM_LIMIT = 56 * 1024 * 1024
_DONE = object()


def _cparams(sem):
    return pltpu.CompilerParams(dimension_semantics=sem, vmem_limit_bytes=VMEM_LIMIT)


def _rms(x, g):
    return x * lax.rsqrt(jnp.mean(x * x, axis=-1, keepdims=True) + NORM_EPS) * g


def _dot(a, b):
    return jnp.dot(a, b, preferred_element_type=f32)


def _dot_nt(a, b):
    return lax.dot_general(a, b, (((1,), (1,)), ((), ())), preferred_element_type=f32)


def _dot_tn(a, b):
    return lax.dot_general(a, b, (((0,), (0,)), ((), ())), preferred_element_type=f32)


def _mla_proj_kernel(x_ref, gmix_ref, wa_ref, qn_ref, kvn_ref, wuq_ref, wukv_ref,
                     c_ref, s1_ref, s2_ref, q_ref, k_ref, kv_ref, ckv_ref, kpe_ref):
    tm = x_ref.shape[0]
    h = _rms(x_ref[...], gmix_ref[...]).astype(bf16)
    a = _dot(h, wa_ref[...])
    cq = _rms(a[:, :Q_LORA], qn_ref[...]).astype(bf16)
    ckv = _rms(a[:, Q_LORA:Q_LORA + KV_LORA], kvn_ref[...])
    ckv_ref[...] = ckv
    cos = c_ref[...]
    sin_lo = s1_ref[...]
    sin_hi = s2_ref[...]

    def rope(t):
        return t * cos + pltpu.roll(t, LANES - QK_ROPE // 2, 1) * sin_lo + pltpu.roll(t, QK_ROPE // 2, 1) * sin_hi

    kpe_t = rope(a[:, Q_LORA + KV_LORA:])
    kpe_ref[...] = kpe_t[:, QK_NOPE:QK_NOPE + QK_ROPE]
    ckv_b = ckv.astype(bf16)
    lane = lax.broadcasted_iota(jnp.int32, (tm, HEAD_TILE), 1)
    for pr in range(MLA_HEADS // 2):
        cols = slice(pr * 2 * HEAD_TILE, (pr + 1) * 2 * HEAD_TILE)
        kv = _dot(ckv_b, wukv_ref[:, cols])
        q = _dot(cq, wuq_ref[:, cols])
        kv_ref[:, cols] = kv.astype(bf16)
        for hh in range(2):
            sl = slice(hh * HEAD_TILE, (hh + 1) * HEAD_TILE)
            out = slice((2 * pr + hh) * HEAD_TILE, (2 * pr + hh + 1) * HEAD_TILE)
            q_ref[:, out] = rope(q[:, sl]).astype(bf16)
            k_ref[:, out] = jnp.where(lane < QK_NOPE, kv[:, sl], kpe_t).astype(bf16)


def _mla_proj(x, gmix, wa, qn, kvn, wuq, wukv, tabs, tm):
    T = x.shape[0]
    tab_c, tab_s1, tab_s2 = tabs
    nt = tab_c.shape[0] // tm
    HW = MLA_HEADS * HEAD_TILE
    const = lambda i: (0, 0)
    row = lambda i: (i, 0)
    tabmap = lambda i: (i % nt, 0)
    return pl.pallas_call(
        _mla_proj_kernel,
        grid=(T // tm,),
        in_specs=[
            pl.BlockSpec((tm, D_MODEL), row),
            pl.BlockSpec((1, D_MODEL), const),
            pl.BlockSpec(wa.shape, const),
            pl.BlockSpec((1, Q_LORA), const),
            pl.BlockSpec((1, KV_LORA), const),
            pl.BlockSpec(wuq.shape, const),
            pl.BlockSpec(wukv.shape, const),
            pl.BlockSpec((tm, HEAD_TILE), tabmap),
            pl.BlockSpec((tm, HEAD_TILE), tabmap),
            pl.BlockSpec((tm, HEAD_TILE), tabmap),
        ],
        out_specs=[
            pl.BlockSpec((tm, HW), row),
            pl.BlockSpec((tm, HW), row),
            pl.BlockSpec((tm, HW), row),
            pl.BlockSpec((tm, KV_LORA), row),
            pl.BlockSpec((tm, QK_ROPE), row),
        ],
        out_shape=[
            jax.ShapeDtypeStruct((T, HW), bf16),
            jax.ShapeDtypeStruct((T, HW), bf16),
            jax.ShapeDtypeStruct((T, HW), bf16),
            jax.ShapeDtypeStruct((T, KV_LORA), f32),
            jax.ShapeDtypeStruct((T, QK_ROPE), f32),
        ],
        compiler_params=_cparams(("parallel",)),
        name="mla_proj",
    )(x, gmix, wa, qn, kvn, wuq, wukv, tab_c, tab_s1, tab_s2)


def _flash_kernel(q_ref, k_ref, kv_ref, o_ref, vt_ref, m_ref, acc_ref, *, tile):
    S = q_ref.shape[0]
    nt = S // tile
    lane = lax.broadcasted_iota(jnp.int32, (tile, HEAD_TILE), 1)
    ones_row = lax.broadcasted_iota(jnp.int32, (HEAD_TILE, tile), 0) == 0
    heads = range(q_ref.shape[1] // HEAD_TILE)
    hs = [slice(hh * HEAD_TILE, (hh + 1) * HEAD_TILE) for hh in heads]

    for hh in heads:
        for j in range(nt):
            vt = kv_ref[j * tile:(j + 1) * tile, hs[hh]].astype(f32).T
            vt_ref[hh, j] = jnp.where(ones_row, 1.0, vt).astype(bf16)

    def q_body(qi, c):
        q0 = qi * tile
        qs = [q_ref[pl.ds(q0, tile), hs[hh]] for hh in heads]
        for hh in heads:
            m_ref[hh] = jnp.full((1, tile), -jnp.inf, f32)
            acc_ref[hh] = jnp.zeros((HEAD_TILE, tile), f32)

        def block(ki, kr, qr, diag):
            k0 = ki * tile + kr.start
            nk = kr.stop - kr.start
            sts = [_dot_nt(k_ref[pl.ds(k0, nk), hs[hh]], qs[hh][qr]) for hh in heads]
            if diag:
                shape = (nk, qr.stop - qr.start)
                visible = lax.broadcasted_iota(jnp.int32, shape, 0) <= lax.broadcasted_iota(jnp.int32, shape, 1)
                sts = [jnp.where(visible, st, -jnp.inf) for st in sts]
            m_prevs = [m_ref[hh, :, qr] for hh in heads]
            m_news = [jnp.maximum(m_prevs[hh], jnp.max(sts[hh], axis=0, keepdims=True)) for hh in heads]
            ps = [jnp.exp2(sts[hh] - m_news[hh]) for hh in heads]
            alphas = [jnp.exp2(m_prevs[hh] - m_news[hh]) for hh in heads]
            pvs = [_dot(vt_ref[hh, ki, :, kr], ps[hh].astype(bf16)) for hh in heads]
            for hh in heads:
                m_ref[hh, :, qr] = m_news[hh]
                acc_ref[hh, :, qr] = alphas[hh] * acc_ref[hh, :, qr] + pvs[hh]

        for ki in range(qi):
            block(ki, slice(0, tile), slice(0, tile), False)
        half = tile // 2
        block(qi, slice(0, half), slice(0, tile), True)
        block(qi, slice(half, tile), slice(half, tile), True)
        accs = [acc_ref[hh] for hh in heads]
        outs = [(a / a[0:1, :]).T for a in accs]
        for pr in range(len(outs) // 2):
            o = jnp.where(lane < V_HEAD, pltpu.roll(outs[2 * pr], V_HEAD, 1), outs[2 * pr + 1])
            o_ref[pl.ds(q0, tile), pr * 2 * V_HEAD:(pr + 1) * 2 * V_HEAD] = o.astype(bf16)
        return c

    for qi in range(nt):
        q_body(qi, 0)


def _flash_attention(q, k, kv, B, S, tile=512, heads_per_step=4):
    T = B * S
    tile = min(tile, S)
    nh = heads_per_step
    blk = pl.BlockSpec((S, nh * HEAD_TILE), lambda b, hp: (b, hp))
    return pl.pallas_call(
        functools.partial(_flash_kernel, tile=tile),
        grid=(B, MLA_HEADS // nh),
        in_specs=[blk, blk, blk],
        out_specs=pl.BlockSpec((S, nh * V_HEAD), lambda b, hp: (b, hp)),
        out_shape=jax.ShapeDtypeStruct((T, MLA_HEADS * V_HEAD), bf16),
        scratch_shapes=[
            pltpu.VMEM((nh, S // tile, HEAD_TILE, tile), bf16),
            pltpu.VMEM((nh, 1, tile), f32),
            pltpu.VMEM((nh, HEAD_TILE, tile), f32),
        ],
        compiler_params=_cparams(("parallel", "parallel")),
        name="mla_flash",
    )(q, k, kv)


def _absorb_q_kernel(q_ref, w_ref, o_ref):
    lane = lax.broadcasted_iota(jnp.int32, q_ref.shape, 1)
    qn = jnp.where(lane < QK_NOPE, q_ref[...], jnp.zeros_like(q_ref[...]))
    o_ref[0] = _dot_nt(qn, w_ref[...]).astype(bf16)


def _absorb_q(q, wukv):
    DB = q.shape[0]
    return pl.pallas_call(
        _absorb_q_kernel,
        grid=(MLA_HEADS,),
        in_specs=[
            pl.BlockSpec((DB, HEAD_TILE), lambda h: (0, h)),
            pl.BlockSpec((KV_LORA, HEAD_TILE), lambda h: (0, h)),
        ],
        out_specs=pl.BlockSpec((1, DB, KV_LORA), lambda h: (h, 0, 0)),
        out_shape=jax.ShapeDtypeStruct((MLA_HEADS, DB, KV_LORA), bf16),
        compiler_params=_cparams(("parallel",)),
        name="mla_absorb_q",
    )(q, wukv)


def _unabsorb_o_kernel(o_ref, w_ref, out_ref):
    out_ref[...] = _dot(o_ref[0].astype(bf16), w_ref[...])


def _unabsorb_o(o_lat, wukv):
    DB = o_lat.shape[1]
    return pl.pallas_call(
        _unabsorb_o_kernel,
        grid=(MLA_HEADS,),
        in_specs=[
            pl.BlockSpec((1, DB, KV_LORA), lambda h: (h, 0, 0)),
            pl.BlockSpec((KV_LORA, HEAD_TILE), lambda h: (0, h)),
        ],
        out_specs=pl.BlockSpec((DB, HEAD_TILE), lambda h: (0, h)),
        out_shape=jax.ShapeDtypeStruct((DB, MLA_HEADS * HEAD_TILE), f32),
        compiler_params=_cparams(("parallel",)),
        name="mla_unabsorb_o",
    )(o_lat, wukv)


def _decode_kernel(pt_ref, qlat_ref, qpe_ref, cnew_ref, pnew_ref, ckv_hbm, kpe_hbm, o_ref,
                   ckv_buf, kpe_buf, sem, m_ref, l_ref, acc_ref, *, pages_per_step, chains):
    PP = pages_per_step
    NC = chains
    DB, n_pages = pt_ref.shape
    G = n_pages // PP
    per_chain = DB // NC
    total = per_chain * G

    def page_copy(page, slot, idx, which):
        if which == 0:
            return pltpu.make_async_copy(ckv_hbm.at[page], ckv_buf.at[slot, idx], sem.at[0, slot])
        return pltpu.make_async_copy(kpe_hbm.at[page], kpe_buf.at[slot, idx], sem.at[1, slot])

    def start_group(it, slot):
        bb = it // G
        g0 = (it % G) * PP
        for c in range(NC):
            for p in range(PP):
                page = pt_ref[c * per_chain + bb, g0 + p]
                page_copy(page, slot, c * PP + p, 0).start()
                page_copy(page, slot, c * PP + p, 1).start()

    def wait_group(slot):
        for idx in range(NC * PP):
            page_copy(0, slot, idx, 0).wait()
            page_copy(0, slot, idx, 1).wait()

    it = pl.program_id(0)
    slot = it % 2

    @pl.when(it == 0)
    def _():
        start_group(0, 0)

    @pl.when(it + 1 < total)
    def _():
        start_group(it + 1, 1 - slot)

    wait_group(slot)
    _decode_group(it // G, it % G, G, per_chain, slot, qlat_ref, qpe_ref, cnew_ref, pnew_ref, o_ref,
                  ckv_buf, kpe_buf, m_ref, l_ref, acc_ref, PP, NC)


def _decode_group(bb, g, G, per_chain, slot, qlat_ref, qpe_ref, cnew_ref, pnew_ref, o_ref,
                  ckv_buf, kpe_buf, m_ref, l_ref, acc_ref, PP, NC):
    chains = range(NC)
    bs = [c * per_chain + bb for c in chains]

    @pl.when(g == 0)
    def _():
        m_ref[...] = jnp.full(m_ref.shape, -jnp.inf, f32)
        l_ref[...] = jnp.zeros(l_ref.shape, f32)
        acc_ref[...] = jnp.zeros(acc_ref.shape, f32)

    qlat = [qlat_ref[bs[c]] for c in chains]
    qpe = [qpe_ref[bs[c]] for c in chains]
    cks = [[ckv_buf[slot, c * PP + p].astype(bf16) for p in range(PP)] for c in chains]
    s = [jnp.concatenate(
        [_dot_nt(qlat[c], cks[c][p]) + _dot(qpe[c], kpe_buf[slot, c * PP + p].astype(bf16)) for p in range(PP)],
        axis=1) for c in chains]
    m_prev = [m_ref[c] for c in chains]
    m_new = [jnp.maximum(m_prev[c], jnp.max(s[c], axis=-1, keepdims=True)) for c in chains]
    p_all = [jnp.exp2(s[c] - m_new[c]) for c in chains]
    alpha = [jnp.exp2(m_prev[c] - m_new[c]) for c in chains]
    for c in chains:
        l_ref[c] = alpha[c] * l_ref[c] + jnp.sum(p_all[c], axis=-1, keepdims=True)
        m_ref[c] = m_new[c]
    for c in chains:
        pv = _dot(p_all[c][:, :PAGE_SIZE].astype(bf16), cks[c][0])
        for p in range(1, PP):
            pv = pv + _dot(p_all[c][:, p * PAGE_SIZE:(p + 1) * PAGE_SIZE].astype(bf16), cks[c][p])
        acc_ref[c] = alpha[c] * acc_ref[c] + pv

    @pl.when(g == G - 1)
    def _():
        for c in chains:
            cnew = cnew_ref[bs[c]].astype(bf16).astype(f32)
            pnew = pnew_ref[bs[c]].astype(bf16).astype(f32)
            s_self = (jnp.sum(qlat[c].astype(f32) * cnew, axis=-1, keepdims=True)
                      + jnp.sum(qpe[c].astype(f32) * pnew, axis=-1, keepdims=True))
            m_last = m_ref[c]
            m_fin = jnp.maximum(m_last, s_self)
            p_self = jnp.exp2(s_self - m_fin)
            a_fin = jnp.exp2(m_last - m_fin)
            l_fin = a_fin * l_ref[c] + p_self
            o_ref[bs[c]] = (a_fin * acc_ref[c] + p_self * cnew) / l_fin


def _decode_attention(page_table, qlat, qpe, ckv_new, kpe_new, pool_ckv, pool_kpe_t, pages_per_step=16):
    DB, n_pages = page_table.shape
    PP = math.gcd(n_pages, pages_per_step)
    NC = 2 if DB % 2 == 0 else 1
    H = MLA_HEADS
    vmem = pl.BlockSpec(memory_space=pltpu.VMEM)
    hbm = pl.BlockSpec(memory_space=pl.ANY)
    grid_spec = pltpu.PrefetchScalarGridSpec(
        num_scalar_prefetch=1,
        grid=((DB // NC) * (n_pages // PP),),
        in_specs=[vmem, vmem, vmem, vmem, hbm, hbm],
        out_specs=vmem,
        scratch_shapes=[
            pltpu.VMEM((2, NC * PP, PAGE_SIZE, KV_LORA), pool_ckv.dtype),
            pltpu.VMEM((2, NC * PP, QK_ROPE, PAGE_SIZE), pool_kpe_t.dtype),
            pltpu.SemaphoreType.DMA((2, 2)),
            pltpu.VMEM((NC, H, 1), f32),
            pltpu.VMEM((NC, H, 1), f32),
            pltpu.VMEM((NC, H, KV_LORA), f32),
        ],
    )
    return pl.pallas_call(
        functools.partial(_decode_kernel, pages_per_step=PP, chains=NC),
        grid_spec=grid_spec,
        out_shape=jax.ShapeDtypeStruct((DB, H, KV_LORA), f32),
        compiler_params=_cparams(("arbitrary",)),
        name="mla_decode",
    )(page_table, qlat, qpe, ckv_new, kpe_new, pool_ckv, pool_kpe_t)


def _proj_ffn_kernel(x_ref, o_ref, wo_ref, gffn_ref, win_ref, wout_ref, gfin_ref, out_ref, *, final_norm):
    x1 = x_ref[...] + _dot(o_ref[...], wo_ref[...])
    h = _rms(x1, gffn_ref[...]).astype(bf16)
    acc = x1
    for c in range(FFN_HIDDEN // FFN_TILE):
        lo = c * FFN_TILE
        gate = _dot(h, win_ref[:, lo:lo + FFN_TILE])
        up = _dot(h, win_ref[:, FFN_HIDDEN + lo:FFN_HIDDEN + lo + FFN_TILE])
        act = (gate * jax.nn.sigmoid(gate) * up).astype(bf16)
        acc = acc + _dot(act, wout_ref[lo:lo + FFN_TILE, :])
    if final_norm:
        acc = _rms(acc, gfin_ref[...])
    out_ref[...] = acc


def _proj_ffn(x, o, wo, gffn, w_in, w_out, gfin, final_norm, tm):
    T = x.shape[0]
    row = lambda i: (i, 0)
    held = lambda arr: pl.BlockSpec(arr.shape, lambda i: (0, 0), pipeline_mode=pl.Buffered(1))
    return pl.pallas_call(
        functools.partial(_proj_ffn_kernel, final_norm=final_norm),
        grid=(T // tm,),
        in_specs=[
            pl.BlockSpec((tm, D_MODEL), row),
            pl.BlockSpec((tm, D_MODEL), row),
            held(wo), held(gffn), held(w_in), held(w_out), held(gfin),
        ],
        out_specs=pl.BlockSpec((tm, D_MODEL), row),
        out_shape=jax.ShapeDtypeStruct((T, D_MODEL), f32),
        compiler_params=_cparams(("parallel",)),
        name="proj_ffn",
    )(x, o, wo, gffn, w_in, w_out, gfin)


def _seg_sum(x, ones_blk):
    parts = [_dot(x[:, gidx * LANES:(gidx + 1) * LANES].astype(bf16), ones_blk)
             for gidx in range(x.shape[1] // LANES)]
    return jnp.concatenate(parts, axis=1)


def _rwkv_proj_kernel(x_ref, prev_ref, gmix_ref, mix_ref, wrkv_ref, w0_ref, w1_ref, w2_ref,
                      a0_ref, a1_ref, a2_ref, g1_ref, g2_ref, kk_ref, ka_ref, rk_ref, ones_ref,
                      r_out, k_out, v_out, kn_out, b_out, ld_out, g_out, bonus_out, h_out,
                      *, seq_mode, tiles_per_seq):
    tm = x_ref.shape[0]
    gm = gmix_ref[...]
    h = _rms(x_ref[...], gm)
    if seq_mode:
        hp_row = _rms(prev_ref[...], gm)[7:8, :]
        is_start = (pl.program_id(0) % tiles_per_seq) == 0
        hp_row = jnp.where(is_start, jnp.zeros_like(hp_row), hp_row)
        rowid = lax.broadcasted_iota(jnp.int32, (tm, 1), 0)
        hprev = jnp.where(rowid == 0, hp_row, pltpu.roll(h, 1, 0))
        h_out[0] = h[tm - 8:, :]
    else:
        hprev = prev_ref[...]
        h_out[...] = h
    xx = hprev - h
    mix = mix_ref[...]
    xr, xw, xk, xv, xa, xg = ((h + xx * mix[n:n + 1]).astype(bf16) for n in range(6))
    w_hid = jnp.tanh(_dot(xw, w1_ref[...])).astype(bf16)
    a_hid = _dot(xa, a1_ref[...]).astype(bf16)
    g_hid = jax.nn.sigmoid(_dot(xg, g1_ref[...])).astype(bf16)
    ones_blk = ones_ref[...]
    ct = 2 * LANES
    for j in range(D_MODEL // ct):
        cols = slice(j * ct, (j + 1) * ct)
        r = _dot(xr, wrkv_ref[0, :, cols])
        k = _dot(xk, wrkv_ref[1, :, cols])
        v = _dot(xv, wrkv_ref[2, :, cols])
        wl = w0_ref[:, cols] + _dot(w_hid, w2_ref[:, cols])
        z = -wl
        w_log = -(jnp.maximum(z, 0.0) + jnp.log(1.0 + jnp.exp(-jnp.abs(z)))) - 0.5
        ld_out[:, cols] = -jnp.exp(w_log)
        a = jax.nn.sigmoid(a0_ref[:, cols] + _dot(a_hid, a2_ref[:, cols]))
        g_out[:, cols] = _dot(g_hid, g2_ref[:, cols]).astype(g_out.dtype)
        kk = k * kk_ref[:, cols]
        kk = kk / jnp.maximum(jnp.sqrt(_seg_sum(kk * kk, ones_blk)), 1e-12)
        k = k * (1.0 + (a - 1.0) * ka_ref[:, cols])
        bonus_out[:, cols] = (_seg_sum(r * k * rk_ref[:, cols], ones_blk) * v).astype(bonus_out.dtype)
        r_out[:, cols] = r.astype(r_out.dtype)
        k_out[:, cols] = k.astype(k_out.dtype)
        v_out[:, cols] = v.astype(v_out.dtype)
        kn_out[:, cols] = kk.astype(kn_out.dtype)
        b_out[:, cols] = (kk * a).astype(b_out.dtype)


def _rwkv_proj(x, prev, p, seq_len, tm, vec_dtype):
    T = x.shape[0]
    seq_mode = seq_len > 1
    tiles_per_seq = max(seq_len // tm, 1)
    row = lambda i: (i, 0)
    const = lambda i: (0, 0)
    const3 = lambda i: (0, 0, 0)
    if seq_mode:
        prev_spec = pl.BlockSpec((8, D_MODEL), lambda i: (jnp.maximum(i * (tm // 8) - 1, 0), 0))
        h_spec = pl.BlockSpec((1, 8, D_MODEL), lambda i: (i, 0, 0))
        h_shape = jax.ShapeDtypeStruct((T // tm, 8, D_MODEL), f32)
    else:
        prev_spec = pl.BlockSpec((tm, D_MODEL), row)
        h_spec = pl.BlockSpec((tm, D_MODEL), row)
        h_shape = jax.ShapeDtypeStruct((T, D_MODEL), f32)
    vec = lambda dt: jax.ShapeDtypeStruct((T, D_MODEL), dt)
    tile = pl.BlockSpec((tm, D_MODEL), row)
    full = lambda arr: pl.BlockSpec(arr.shape, const3 if arr.ndim == 3 else const)
    weights = [p["gmix"], p["mix"], p["w_rkv"], p["w0"], p["w1"], p["w2"], p["a0"], p["a1"], p["a2"],
               p["g1"], p["g2"], p["k_k"], p["k_a"], p["r_k"], p["ones_blk"]]
    return pl.pallas_call(
        functools.partial(_rwkv_proj_kernel, seq_mode=seq_mode, tiles_per_seq=tiles_per_seq),
        grid=(T // tm,),
        in_specs=[tile, prev_spec] + [full(w) for w in weights],
        out_specs=[tile] * 8 + [h_spec],
        out_shape=[vec(vec_dtype)] * 5 + [vec(f32), vec(bf16), vec(bf16), h_shape],
        compiler_params=_cparams(("parallel",)),
        name="rwkv_proj",
    )(x, prev, *weights)


def _wkv_chunk_kernel(r_ref, k_ref, v_ref, kn_ref, b_ref, ld_ref, y_ref, st_ref, m_ref, *, chunks, wave):
    C = WKV_CHUNK
    G = WKV_GROUP
    W = GROUP_W
    c_idx = pl.program_id(1)

    @pl.when(c_idx == 0)
    def _():
        m_ref[...] = jnp.zeros(m_ref.shape, f32)

    rr = lax.broadcasted_iota(jnp.int32, (W, W), 0)
    cc = lax.broadcasted_iota(jnp.int32, (W, W), 1)
    bd_mask = (rr // RWKV_HEAD) == (cc // RWKV_HEAD)
    t_idx = lax.broadcasted_iota(jnp.int32, (C, W), 0)
    s_idx = lax.broadcasted_iota(jnp.int32, (C, W), 1) % C
    strict = s_idx < t_idx
    incl = s_idx <= t_idx
    eye_cat = (s_idx == t_idx).astype(f32)
    lane_lo = lax.broadcasted_iota(jnp.int32, (C, LANES), 1) < RWKV_HEAD
    zero_tile = jnp.zeros((C, LANES), bf16)

    def bd(y):
        yb = y.astype(bf16)
        rows = []
        for hh in range(G):
            lt = hh // 2
            t = yb[:, lt * LANES:(lt + 1) * LANES]
            keep = jnp.where(lane_lo, t, zero_tile) if hh % 2 == 0 else jnp.where(lane_lo, zero_tile, t)
            tiles = [zero_tile] * (W // LANES)
            tiles[lt] = keep
            rows.append(jnp.concatenate(tiles, axis=1))
        return jnp.concatenate(rows, axis=0)

    def bdmm(x, y):
        return _dot(x.astype(bf16), bd(y))

    def cumsum_rows(x):
        sh = 1
        while sh < C:
            x = x + jnp.where(t_idx >= sh, pltpu.roll(x, sh, 0), 0.0)
            sh *= 2
        return x

    NG = r_ref.shape[1] // W
    n_waves = chunks // wave

    def precompute(wv, out):
        streams = [(wv * wave + ci, gi) for ci in range(wave) for gi in range(NG)]
        ns = range(len(streams))

        def load(ref, s):
            ci, gi = streams[s]
            return ref[ci * C:(ci + 1) * C, gi * W:(gi + 1) * W]

        ld = [load(ld_ref, s) for s in ns]
        cs = [cumsum_rows(ld[s]) for s in ns]
        tot_col = [jnp.broadcast_to(cs[s][C - 1:C, :], (LANES, W)).T for s in ns]
        yield
        r = [load(r_ref, s).astype(f32) for s in ns]
        k = [load(k_ref, s).astype(f32) for s in ns]
        v = [load(v_ref, s).astype(f32) for s in ns]
        kn = [load(kn_ref, s).astype(f32) for s in ns]
        b = [load(b_ref, s).astype(f32) for s in ns]
        p_inv = [jnp.exp(-cs[s]) for s in ns]
        p_rest = [jnp.exp(cs[s][C - 1:C, :] - cs[s]) for s in ns]
        a_t = [-kn[s] * jnp.exp(cs[s] - ld[s]) for s in ns]
        r_t = [r[s] * jnp.exp(cs[s]) for s in ns]
        lhs = [jnp.concatenate([a_t[s], r_t[s]], axis=0).astype(bf16) for s in ns]
        yield
        s_b = [_dot_nt(lhs[s], bd(b[s] * p_inv[s])) for s in ns]
        yield
        s_k = [_dot_nt(lhs[s], bd(k[s] * p_inv[s])) for s in ns]
        yield
        l_ab = [jnp.where(strict, s_b[s][:C], 0.0) for s in ns]
        l_ak = [jnp.where(strict, s_k[s][:C], 0.0) for s in ns]
        a_rb = [jnp.where(incl, s_b[s][C:], 0.0) for s in ns]
        a_rk = [jnp.where(incl, s_k[s][C:], 0.0) for s in ns]
        t_inv = [eye_cat + l_ab[s] for s in ns]
        pw = [bdmm(l_ab[s], l_ab[s]) for s in ns]
        yield
        step = 2
        while step < C:
            rhs = [bd(pw[s]) for s in ns]
            if 2 * step < C:
                both = [_dot(jnp.concatenate([pw[s], t_inv[s]], axis=0).astype(bf16), rhs[s]) for s in ns]
                pw = [both[s][:C] for s in ns]
                t_inv = [t_inv[s] + both[s][C:] for s in ns]
            else:
                t_inv = [t_inv[s] + _dot(t_inv[s].astype(bf16), rhs[s]) for s in ns]
            step *= 2
            yield
        from_v = [bdmm(jnp.concatenate([l_ak[s], a_rk[s]], axis=0), v[s]) for s in ns]
        yield
        w1 = [bdmm(t_inv[s], a_t[s]) for s in ns]
        yield
        w2 = [bdmm(t_inv[s], from_v[s][:C]) for s in ns]
        out["st_lhs"] = [jnp.concatenate([w1[s], r_t[s]], axis=0).astype(bf16) for s in ns]
        out["w2"] = w2
        out["y_v"] = [from_v[s][C:] for s in ns]
        out["a_rb"] = a_rb
        out["v"] = v
        out["bk_rest"] = [jnp.concatenate([b[s] * p_rest[s], k[s] * p_rest[s]], axis=0).astype(bf16) for s in ns]
        out["decay_col"] = [jnp.exp(jnp.concatenate([tot_col[s]] * (W // LANES), axis=1)) for s in ns]
        yield

    def state_chain(wv, d, m):
        for ci in range(wave):
            c_glob = wv * wave + ci
            ss = [ci * NG + gi for gi in range(NG)]
            from_state = [_dot(d["st_lhs"][s], m[gi].astype(bf16)) for gi, s in enumerate(ss)]
            yield
            u = [from_state[gi][:C] + d["w2"][s] for gi, s in enumerate(ss)]
            y_u = [bdmm(d["a_rb"][s], u[gi]) for gi, s in enumerate(ss)]
            upd = [_dot_tn(d["bk_rest"][s], jnp.concatenate([u[gi], d["v"][s]], axis=0).astype(bf16))
                   for gi, s in enumerate(ss)]
            yield
            for gi, s in enumerate(ss):
                y_ref[c_glob * C:(c_glob + 1) * C, gi * W:(gi + 1) * W] = from_state[gi][C:] + y_u[gi] + d["y_v"][s]
                m[gi] = m[gi] * d["decay_col"][s] + jnp.where(bd_mask, upd[gi], 0.0)
            yield

    front_stages = 5
    m = [m_ref[gi] for gi in range(NG)]
    data = [{} for _ in range(n_waves)]
    pres = [precompute(wv, data[wv]) for wv in range(n_waves)]
    for t in range(n_waves + 2):
        lanes = []
        if 0 <= t - 1 < n_waves:
            lanes.append(pres[t - 1])
        if t < n_waves:
            lanes.append(itertools.islice(pres[t], front_stages))
        if 0 <= t - 2 < n_waves:
            lanes.append(state_chain(t - 2, data[t - 2], m))
        while lanes:
            lanes = [g for g in lanes if next(g, _DONE) is not _DONE]
    for gi in range(NG):
        m_ref[gi] = m[gi]
        acc = m[gi][:RWKV_HEAD]
        for hh in range(1, G):
            acc = acc + m[gi][hh * RWKV_HEAD:(hh + 1) * RWKV_HEAD]
        st_ref[0, :, gi * W:(gi + 1) * W] = acc


def _wkv_chunked(r, k, v, kn, b, ld, B, S):
    T = B * S
    chunks = WKV_CHUNKS_PER_STEP if S % (WKV_CHUNK * WKV_CHUNKS_PER_STEP) == 0 else 1
    wave = WKV_WAVE if chunks % WKV_WAVE == 0 else chunks
    rows = WKV_CHUNK * chunks
    nc = S // rows
    tile = pl.BlockSpec((rows, D_MODEL), lambda bi, ci: (bi * nc + ci, 0))
    return pl.pallas_call(
        functools.partial(_wkv_chunk_kernel, chunks=chunks, wave=wave),
        grid=(B, nc),
        in_specs=[tile] * 6,
        out_specs=[tile, pl.BlockSpec((1, RWKV_HEAD, D_MODEL), lambda bi, ci: (bi, 0, 0))],
        out_shape=[jax.ShapeDtypeStruct((T, D_MODEL), f32),
                   jax.ShapeDtypeStruct((B, RWKV_HEAD, D_MODEL), f32)],
        scratch_shapes=[pltpu.VMEM((D_MODEL // GROUP_W, GROUP_W, GROUP_W), f32)],
        compiler_params=_cparams(("parallel", "arbitrary")),
        name="wkv_chunked",
    )(r, k, v, kn, b, ld)


def _wkv_step_kernel(s_ref, vec_ref, snew_ref, y_ref):
    r, k, kn, b, ld = (vec_ref[i, 0] for i in (0, 1, 3, 4, 5))
    a = -kn
    w = jnp.exp(ld)

    def value_row(vi, c):
        st = s_ref[0, vi]
        sa = jnp.sum(st * a, axis=0, keepdims=True)
        v_row = vec_ref[2, 0, pl.ds(vi, 1), :]
        st = st * w + sa * b + v_row * k
        snew_ref[0, vi] = st
        y_ref[0, pl.ds(vi, 1), :] = jnp.sum(st * r, axis=0, keepdims=True)
        return c

    lax.fori_loop(0, RWKV_HEAD, value_row, 0, unroll=4)


def _wkv_step(state_t, vecs):
    H, N, _, DB = state_t.shape
    st = pl.BlockSpec((1, N, N, DB), lambda h: (h, 0, 0, 0))
    return pl.pallas_call(
        _wkv_step_kernel,
        grid=(H,),
        in_specs=[st, pl.BlockSpec((6, 1, N, DB), lambda h: (0, h, 0, 0))],
        out_specs=[st, pl.BlockSpec((1, N, DB), lambda h: (h, 0, 0))],
        out_shape=[jax.ShapeDtypeStruct(state_t.shape, f32),
                   jax.ShapeDtypeStruct((H, N, DB), f32)],
        compiler_params=_cparams(("parallel",)),
        name="wkv_step",
    )(state_t, vecs)


def _rwkv_post_kernel(y_ref, bonus_ref, g_ref, lnw_ref, lnb_ref, ones_ref, o_ref):
    ones_blk = ones_ref[...]
    y = y_ref[...]
    inv_n = 1.0 / RWKV_HEAD
    mu = _seg_sum(y, ones_blk) * inv_n
    d = y - mu
    var = _seg_sum(d * d, ones_blk) * inv_n
    yn = d * lax.rsqrt(var + GN_EPS) * lnw_ref[...] + lnb_ref[...]
    o_ref[...] = ((yn + bonus_ref[...].astype(f32)) * g_ref[...].astype(f32)).astype(bf16)


def _rwkv_post(y, bonus, g, lnw, lnb, ones_blk, tm):
    T = y.shape[0]
    row = lambda i: (i, 0)
    const = lambda i: (0, 0)
    tile = pl.BlockSpec((tm, D_MODEL), row)
    return pl.pallas_call(
        _rwkv_post_kernel,
        grid=(T // tm,),
        in_specs=[tile, tile, tile, pl.BlockSpec((1, D_MODEL), const), pl.BlockSpec((1, D_MODEL), const),
                  pl.BlockSpec((LANES, LANES), const)],
        out_specs=tile,
        out_shape=jax.ShapeDtypeStruct((T, D_MODEL), bf16),
        compiler_params=_cparams(("parallel",)),
        name="rwkv_post",
    )(y, bonus, g, lnw, lnb, ones_blk)


def _rope_tables(pos):
    half = QK_ROPE // 2
    inv = 1.0 / (ROPE_THETA ** (jnp.arange(0, QK_ROPE, 2, dtype=f32) / QK_ROPE))
    ang = pos[:, None] * inv[None, :]
    cos, sin = jnp.cos(ang), jnp.sin(ang)
    n = pos.shape[0]
    ones = jnp.ones((n, QK_NOPE), f32)
    z = lambda w: jnp.zeros((n, w), f32)
    tab_c = jnp.concatenate([ones, cos, cos, z(HEAD_TILE - QK_NOPE - QK_ROPE)], axis=1)
    tab_s1 = jnp.concatenate([z(QK_NOPE), -sin, z(HEAD_TILE - QK_NOPE - half)], axis=1)
    tab_s2 = jnp.concatenate([z(QK_NOPE + half), sin, z(HEAD_TILE - QK_NOPE - QK_ROPE)], axis=1)
    return tab_c, tab_s1, tab_s2


def _pick_tile(n, pref):
    t = min(n, pref)
    while n % t:
        t //= 2
    return t


def kernel(x_prompt, x_sample, cache_ckv, cache_kpe, state_wkv, state_shift, page_table,
           norm_mix, norm_ffn, norm_final,
           mla_w_a, mla_q_norm, mla_kv_norm, mla_w_uq, mla_w_ukv, mla_w_o,
           rw_mix, rw_w_rkv, rw_w0, rw_w1, rw_w2, rw_a0, rw_a1, rw_a2, rw_g1, rw_g2,
           rw_k_k, rw_k_a, rw_r_k, rw_ln_w, rw_ln_b, rw_w_o,
           ffn_w_in, ffn_w_out):
    B, S, D = x_prompt.shape
    DB, DS, _ = x_sample.shape
    assert D == D_MODEL and DS == 1 and S % WKV_CHUNK == 0
    n_pages = page_table.shape[1]
    past_len = n_pages * PAGE_SIZE
    H = MLA_HEADS
    row = lambda t: t.reshape(1, -1)
    xp = x_prompt.reshape(B * S, D)
    xs = x_sample.reshape(DB, D)

    w_a = mla_w_a[0]
    pad_pe = jnp.zeros((D, HEAD_TILE), f32).at[:, QK_NOPE:QK_NOPE + QK_ROPE].set(w_a[:, Q_LORA + KV_LORA:])
    wa_ext = jnp.concatenate([w_a[:, :Q_LORA + KV_LORA], pad_pe], axis=1).astype(bf16)
    wuq = (mla_w_uq[0] * (MLA_SCALE * LOG2_E)).reshape(Q_LORA, H, QK_NOPE + QK_ROPE)
    wuq = jnp.pad(wuq, ((0, 0), (0, 0), (0, HEAD_TILE - QK_NOPE - QK_ROPE))).reshape(Q_LORA, H * HEAD_TILE).astype(bf16)
    wukv = mla_w_ukv[0].reshape(KV_LORA, H * HEAD_TILE).astype(bf16)
    mla_wo = mla_w_o[0].astype(bf16)
    qn, kvn = row(mla_q_norm[0]), row(mla_kv_norm[0])
    ffn_in = ffn_w_in.astype(bf16)
    ffn_out = ffn_w_out.astype(bf16)
    hh = jnp.arange(LANES) // RWKV_HEAD
    ones_blk = (hh[:, None] == hh[None, :]).astype(bf16)
    rwp = dict(
        gmix=row(norm_mix[1]), mix=jnp.pad(rw_mix[0], ((0, 2), (0, 0))), w_rkv=rw_w_rkv[0].astype(bf16),
        w0=row(rw_w0[0]), w1=rw_w1[0].astype(bf16), w2=rw_w2[0].astype(bf16),
        a0=row(rw_a0[0]), a1=rw_a1[0].astype(bf16), a2=rw_a2[0].astype(bf16),
        g1=rw_g1[0].astype(bf16), g2=rw_g2[0].astype(bf16),
        k_k=row(rw_k_k[0]), k_a=row(rw_k_a[0]), r_k=row(rw_r_k[0]), ones_blk=ones_blk)
    rw_wo = rw_w_o[0].astype(bf16)
    lnw, lnb = row(rw_ln_w[0]), row(rw_ln_b[0])
    gfin = row(norm_final)

    tm_p = _pick_tile(B * S, 512)
    tm_s = _pick_tile(DB, 128)

    tabs_p = _rope_tables(jnp.arange(S, dtype=f32))
    tabs_s = _rope_tables(jnp.full((tm_s,), past_len, f32))
    q_s, _, _, ckv_s, kpe_s = _mla_proj(xs, row(norm_mix[0]), wa_ext, qn, kvn, wuq, wukv, tabs_s, tm_s)
    qlat = jnp.swapaxes(_absorb_q(q_s, wukv), 0, 1)
    qpe = q_s.reshape(DB, H, HEAD_TILE)[:, :, QK_NOPE:QK_NOPE + QK_ROPE]
    o_lat = _decode_attention(page_table, qlat, qpe, ckv_s.reshape(DB, 1, KV_LORA), kpe_s.reshape(DB, 1, QK_ROPE),
                              cache_ckv.reshape(-1, PAGE_SIZE, KV_LORA),
                              jnp.swapaxes(cache_kpe.reshape(-1, PAGE_SIZE, QK_ROPE), 1, 2))
    o_s = _unabsorb_o(jnp.swapaxes(o_lat, 0, 1), wukv)
    o_s = o_s.reshape(DB, H, HEAD_TILE)[:, :, QK_NOPE:].reshape(DB, H * V_HEAD).astype(bf16)
    xs = _proj_ffn(xs, o_s, mla_wo, row(norm_ffn[0]), ffn_in[0], ffn_out[0], gfin, False, tm_s)

    q_p, k_p, kv_p, ckv_p, kpe_p = _mla_proj(xp, row(norm_mix[0]), wa_ext, qn, kvn, wuq, wukv, tabs_p,
                                             _pick_tile(S, 512))
    o_p = _flash_attention(q_p, k_p, kv_p, B, S)
    xp = _proj_ffn(xp, o_p, mla_wo, row(norm_ffn[0]), ffn_in[0], ffn_out[0], gfin, False, tm_p)

    tm_r = _pick_tile(S, 512)
    r, k, v, kn, b, ld, g, bonus, hl = _rwkv_proj(xp, xp, rwp, S, tm_r, bf16)
    shift_p = hl.reshape(B, S // tm_r, 8, D)[:, -1, 7, :]
    y, st = _wkv_chunked(r, k, v, kn, b, ld, B, S)
    wkv_p = jnp.transpose(st.reshape(B, RWKV_HEAD, RWKV_HEADS, RWKV_HEAD), (0, 2, 3, 1))
    yo = _rwkv_post(y, bonus, g, lnw, lnb, ones_blk, tm_p)
    y_prompt = _proj_ffn(xp, yo, rw_wo, row(norm_ffn[1]), ffn_in[1], ffn_out[1], gfin, True, tm_p)

    r, k, v, kn, b, ld, g, bonus, shift_s = _rwkv_proj(xs, state_shift[0], rwp, 1, tm_s, f32)
    vecs = jnp.transpose(jnp.stack([r, k, v, kn, b, ld]), (0, 2, 1)).reshape(6, RWKV_HEADS, RWKV_HEAD, DB)
    wkv_t, y_t = _wkv_step(jnp.transpose(state_wkv[0].astype(f32), (1, 2, 3, 0)), vecs)
    wkv_s = jnp.transpose(wkv_t, (3, 0, 1, 2)).astype(state_wkv.dtype)
    y = jnp.transpose(y_t.reshape(D, DB))
    yo = _rwkv_post(y, bonus, g, lnw, lnb, ones_blk, tm_s)
    y_sample = _proj_ffn(xs, yo, rw_wo, row(norm_ffn[1]), ffn_in[1], ffn_out[1], gfin, True, tm_s)

    return (y_prompt.reshape(B, S, D), y_sample.reshape(DB, DS, D),
            ckv_p.reshape(1, B, S, KV_LORA), kpe_p.reshape(1, B, S, QK_ROPE),
            ckv_s.reshape(1, DB, DS, KV_LORA), kpe_s.reshape(1, DB, DS, QK_ROPE),
            wkv_p[None].astype(x_prompt.dtype), shift_p[None],
            wkv_s[None], shift_s[None])
```

```python
import functools
import itertools
import math

import jax
import jax.numpy as jnp
from jax import lax
from jax.experimental import pallas as pl
from jax.experimental.pallas import tpu as pltpu

f32 = jnp.float32
bf16 = jnp.bfloat16

D_MODEL = 1024
MLA_HEADS = 16
QK_NOPE = 64
QK_ROPE = 32
V_HEAD = 64
Q_LORA = 512
KV_LORA = 256
ROPE_THETA = 10000.0
MLA_SCALE = 1.0 / math.sqrt(QK_NOPE + QK_ROPE)
LOG2_E = math.log2(math.e)
RWKV_HEAD = 64
RWKV_HEADS = D_MODEL // RWKV_HEAD
FFN_HIDDEN = 2816
FFN_TILE = 256
NORM_EPS = 1e-6
GN_EPS = 64e-5
PAGE_SIZE = 128

LANES = 128
HEAD_TILE = 128
DECODE_CHAINS = 4
WKV_CHUNK = 64
WKV_GROUP = 4
WKV_CHUNKS_PER_STEP = 4
WKV_WAVE = 2
GROUP_W = WKV_GROUP * RWKV_HEAD
VMEM_LIMIT = 56 * 1024 * 1024
_DONE = object()


def _cparams(sem):
    return pltpu.CompilerParams(dimension_semantics=sem, vmem_limit_bytes=VMEM_LIMIT)


def _rms(x, g):
    return x * lax.rsqrt(jnp.mean(x * x, axis=-1, keepdims=True) + NORM_EPS) * g


def _dot(a, b):
    return jnp.dot(a, b, preferred_element_type=f32)


def _dot_nt(a, b):
    return lax.dot_general(a, b, (((1,), (1,)), ((), ())), preferred_element_type=f32)


def _dot_tn(a, b):
    return lax.dot_general(a, b, (((0,), (0,)), ((), ())), preferred_element_type=f32)


def _mla_proj_kernel(x_ref, gmix_ref, wa_ref, qn_ref, kvn_ref, wuq_ref, wukv_ref,
                     c_ref, s1_ref, s2_ref, q_ref, k_ref, kv_ref, ckv_ref, kpe_ref):
    tm = x_ref.shape[0]
    h = _rms(x_ref[...], gmix_ref[...]).astype(bf16)
    a = _dot(h, wa_ref[...])
    cq = _rms(a[:, :Q_LORA], qn_ref[...]).astype(bf16)
    ckv = _rms(a[:, Q_LORA:Q_LORA + KV_LORA], kvn_ref[...])
    ckv_ref[...] = ckv
    cos = c_ref[...]
    sin_lo = s1_ref[...]
    sin_hi = s2_ref[...]

    def rope(t):
        return t * cos + pltpu.roll(t, LANES - QK_ROPE // 2, 1) * sin_lo + pltpu.roll(t, QK_ROPE // 2, 1) * sin_hi

    kpe_t = rope(a[:, Q_LORA + KV_LORA:])
    kpe_ref[...] = kpe_t[:, QK_NOPE:QK_NOPE + QK_ROPE]
    ckv_b = ckv.astype(bf16)
    lane = lax.broadcasted_iota(jnp.int32, (tm, HEAD_TILE), 1)
    for pr in range(MLA_HEADS // 2):
        cols = slice(pr * 2 * HEAD_TILE, (pr + 1) * 2 * HEAD_TILE)
        kv = _dot(ckv_b, wukv_ref[:, cols])
        q = _dot(cq, wuq_ref[:, cols])
        kv_ref[:, cols] = kv.astype(bf16)
        for hh in range(2):
            sl = slice(hh * HEAD_TILE, (hh + 1) * HEAD_TILE)
            out = slice((2 * pr + hh) * HEAD_TILE, (2 * pr + hh + 1) * HEAD_TILE)
            q_ref[:, out] = rope(q[:, sl]).astype(bf16)
            k_ref[:, out] = jnp.where(lane < QK_NOPE, kv[:, sl], kpe_t).astype(bf16)


def _mla_proj(x, gmix, wa, qn, kvn, wuq, wukv, tabs, tm):
    T = x.shape[0]
    tab_c, tab_s1, tab_s2 = tabs
    nt = tab_c.shape[0] // tm
    HW = MLA_HEADS * HEAD_TILE
    const = lambda i: (0, 0)
    row = lambda i: (i, 0)
    tabmap = lambda i: (i % nt, 0)
    return pl.pallas_call(
        _mla_proj_kernel,
        grid=(T // tm,),
        in_specs=[
            pl.BlockSpec((tm, D_MODEL), row),
            pl.BlockSpec((1, D_MODEL), const),
            pl.BlockSpec(wa.shape, const),
            pl.BlockSpec((1, Q_LORA), const),
            pl.BlockSpec((1, KV_LORA), const),
            pl.BlockSpec(wuq.shape, const),
            pl.BlockSpec(wukv.shape, const),
            pl.BlockSpec((tm, HEAD_TILE), tabmap),
            pl.BlockSpec((tm, HEAD_TILE), tabmap),
            pl.BlockSpec((tm, HEAD_TILE), tabmap),
        ],
        out_specs=[
            pl.BlockSpec((tm, HW), row),
            pl.BlockSpec((tm, HW), row),
            pl.BlockSpec((tm, HW), row),
            pl.BlockSpec((tm, KV_LORA), row),
            pl.BlockSpec((tm, QK_ROPE), row),
        ],
        out_shape=[
            jax.ShapeDtypeStruct((T, HW), bf16),
            jax.ShapeDtypeStruct((T, HW), bf16),
            jax.ShapeDtypeStruct((T, HW), bf16),
            jax.ShapeDtypeStruct((T, KV_LORA), f32),
            jax.ShapeDtypeStruct((T, QK_ROPE), f32),
        ],
        compiler_params=_cparams(("parallel",)),
        name="mla_proj",
    )(x, gmix, wa, qn, kvn, wuq, wukv, tab_c, tab_s1, tab_s2)


def _flash_kernel(q_ref, k_ref, kv_ref, o_ref, vt_ref, m_ref, acc_ref, *, tile):
    S = q_ref.shape[0]
    nt = S // tile
    lane = lax.broadcasted_iota(jnp.int32, (tile, HEAD_TILE), 1)
    ones_row = lax.broadcasted_iota(jnp.int32, (HEAD_TILE, tile), 0) == 0
    heads = range(q_ref.shape[1] // HEAD_TILE)
    hs = [slice(hh * HEAD_TILE, (hh + 1) * HEAD_TILE) for hh in heads]

    for hh in heads:
        for j in range(nt):
            vt = kv_ref[j * tile:(j + 1) * tile, hs[hh]].astype(f32).T
            vt_ref[hh, j] = jnp.where(ones_row, 1.0, vt).astype(bf16)

    def q_body(qi, c):
        q0 = qi * tile
        qs = [q_ref[pl.ds(q0, tile), hs[hh]] for hh in heads]
        for hh in heads:
            m_ref[hh] = jnp.full((1, tile), -jnp.inf, f32)
            acc_ref[hh] = jnp.zeros((HEAD_TILE, tile), f32)

        def block(ki, kr, qr, diag):
            k0 = ki * tile + kr.start
            nk = kr.stop - kr.start
            sts = [_dot_nt(k_ref[pl.ds(k0, nk), hs[hh]], qs[hh][qr]) for hh in heads]
            if diag:
                shape = (nk, qr.stop - qr.start)
                visible = lax.broadcasted_iota(jnp.int32, shape, 0) <= lax.broadcasted_iota(jnp.int32, shape, 1)
                sts = [jnp.where(visible, st, -jnp.inf) for st in sts]
            m_prevs = [m_ref[hh, :, qr] for hh in heads]
            m_news = [jnp.maximum(m_prevs[hh], jnp.max(sts[hh], axis=0, keepdims=True)) for hh in heads]
            ps = [jnp.exp2(sts[hh] - m_news[hh]) for hh in heads]
            alphas = [jnp.exp2(m_prevs[hh] - m_news[hh]) for hh in heads]
            pvs = [_dot(vt_ref[hh, ki, :, kr], ps[hh].astype(bf16)) for hh in heads]
            for hh in heads:
                m_ref[hh, :, qr] = m_news[hh]
                acc_ref[hh, :, qr] = alphas[hh] * acc_ref[hh, :, qr] + pvs[hh]

        for ki in range(qi):
            block(ki, slice(0, tile), slice(0, tile), False)
        half = tile // 2
        block(qi, slice(0, half), slice(0, tile), True)
        block(qi, slice(half, tile), slice(half, tile), True)
        accs = [acc_ref[hh] for hh in heads]
        outs = [(a / a[0:1, :]).T for a in accs]
        for pr in range(len(outs) // 2):
            o = jnp.where(lane < V_HEAD, pltpu.roll(outs[2 * pr], V_HEAD, 1), outs[2 * pr + 1])
            o_ref[pl.ds(q0, tile), pr * 2 * V_HEAD:(pr + 1) * 2 * V_HEAD] = o.astype(bf16)
        return c

    for qi in range(nt):
        q_body(qi, 0)


def _flash_attention(q, k, kv, B, S, tile=512, heads_per_step=4):
    T = B * S
    tile = min(tile, S)
    nh = heads_per_step
    blk = pl.BlockSpec((S, nh * HEAD_TILE), lambda b, hp: (b, hp))
    return pl.pallas_call(
        functools.partial(_flash_kernel, tile=tile),
        grid=(B, MLA_HEADS // nh),
        in_specs=[blk, blk, blk],
        out_specs=pl.BlockSpec((S, nh * V_HEAD), lambda b, hp: (b, hp)),
        out_shape=jax.ShapeDtypeStruct((T, MLA_HEADS * V_HEAD), bf16),
        scratch_shapes=[
            pltpu.VMEM((nh, S // tile, HEAD_TILE, tile), bf16),
            pltpu.VMEM((nh, 1, tile), f32),
            pltpu.VMEM((nh, HEAD_TILE, tile), f32),
        ],
        compiler_params=_cparams(("parallel", "parallel")),
        name="mla_flash",
    )(q, k, kv)


def _absorb_q_kernel(q_ref, w_ref, o_ref):
    lane = lax.broadcasted_iota(jnp.int32, q_ref.shape, 1)
    qn = jnp.where(lane < QK_NOPE, q_ref[...], jnp.zeros_like(q_ref[...]))
    o_ref[0] = _dot_nt(qn, w_ref[...]).astype(bf16)


def _absorb_q(q, wukv):
    DB = q.shape[0]
    return pl.pallas_call(
        _absorb_q_kernel,
        grid=(MLA_HEADS,),
        in_specs=[
            pl.BlockSpec((DB, HEAD_TILE), lambda h: (0, h)),
            pl.BlockSpec((KV_LORA, HEAD_TILE), lambda h: (0, h)),
        ],
        out_specs=pl.BlockSpec((1, DB, KV_LORA), lambda h: (h, 0, 0)),
        out_shape=jax.ShapeDtypeStruct((MLA_HEADS, DB, KV_LORA), bf16),
        compiler_params=_cparams(("parallel",)),
        name="mla_absorb_q",
    )(q, wukv)


def _unabsorb_o_kernel(o_ref, w_ref, out_ref):
    out_ref[...] = _dot(o_ref[0].astype(bf16), w_ref[...])


def _unabsorb_o(o_lat, wukv):
    DB = o_lat.shape[1]
    return pl.pallas_call(
        _unabsorb_o_kernel,
        grid=(MLA_HEADS,),
        in_specs=[
            pl.BlockSpec((1, DB, KV_LORA), lambda h: (h, 0, 0)),
            pl.BlockSpec((KV_LORA, HEAD_TILE), lambda h: (0, h)),
        ],
        out_specs=pl.BlockSpec((DB, HEAD_TILE), lambda h: (0, h)),
        out_shape=jax.ShapeDtypeStruct((DB, MLA_HEADS * HEAD_TILE), f32),
        compiler_params=_cparams(("parallel",)),
        name="mla_unabsorb_o",
    )(o_lat, wukv)


def _decode_kernel(pt_ref, qlat_ref, qpe_ref, cnew_ref, pnew_ref, ckv_hbm, kpe_hbm, o_ref,
                   ckv_buf, kpe_buf, sem, m_ref, l_ref, acc_ref, *, pages_per_step, chains):
    PP = pages_per_step
    NC = chains
    DB, n_pages = pt_ref.shape
    G = n_pages // PP
    per_chain = DB // NC
    total = per_chain * G

    def page_copy(page, slot, idx, which):
        if which == 0:
            return pltpu.make_async_copy(ckv_hbm.at[page], ckv_buf.at[slot, idx], sem.at[0, slot])
        return pltpu.make_async_copy(kpe_hbm.at[page], kpe_buf.at[slot, idx], sem.at[1, slot])

    def start_group(it, slot):
        bb = it // G
        g0 = (it % G) * PP
        for c in range(NC):
            for p in range(PP):
                page = pt_ref[c * per_chain + bb, g0 + p]
                page_copy(page, slot, c * PP + p, 0).start()
                page_copy(page, slot, c * PP + p, 1).start()

    def wait_group(slot):
        for idx in range(NC * PP):
            page_copy(0, slot, idx, 0).wait()
            page_copy(0, slot, idx, 1).wait()

    it = pl.program_id(0)
    slot = it % 2

    @pl.when(it == 0)
    def _():
        start_group(0, 0)

    @pl.when(it + 1 < total)
    def _():
        start_group(it + 1, 1 - slot)

    wait_group(slot)
    _decode_group(it // G, it % G, G, per_chain, slot, qlat_ref, qpe_ref, cnew_ref, pnew_ref, o_ref,
                  ckv_buf, kpe_buf, m_ref, l_ref, acc_ref, PP, NC)


def _decode_group(bb, g, G, per_chain, slot, qlat_ref, qpe_ref, cnew_ref, pnew_ref, o_ref,
                  ckv_buf, kpe_buf, m_ref, l_ref, acc_ref, PP, NC):
    chains = range(NC)
    bs = [c * per_chain + bb for c in chains]

    @pl.when(g == 0)
    def _():
        m_ref[...] = jnp.full(m_ref.shape, -jnp.inf, f32)
        l_ref[...] = jnp.zeros(l_ref.shape, f32)
        acc_ref[...] = jnp.zeros(acc_ref.shape, f32)

    qlat = [qlat_ref[bs[c]] for c in chains]
    qpe = [qpe_ref[bs[c]] for c in chains]
    cks = [[ckv_buf[slot, c * PP + p].astype(bf16) for p in range(PP)] for c in chains]
    s = [jnp.concatenate(
        [_dot_nt(qlat[c], cks[c][p]) + _dot(qpe[c], kpe_buf[slot, c * PP + p].astype(bf16)) for p in range(PP)],
        axis=1) for c in chains]
    m_prev = [m_ref[c] for c in chains]
    m_new = [jnp.maximum(m_prev[c], jnp.max(s[c], axis=-1, keepdims=True)) for c in chains]
    p_all = [jnp.exp2(s[c] - m_new[c]) for c in chains]
    alpha = [jnp.exp2(m_prev[c] - m_new[c]) for c in chains]
    for c in chains:
        l_ref[c] = alpha[c] * l_ref[c] + jnp.sum(p_all[c], axis=-1, keepdims=True)
        m_ref[c] = m_new[c]
    for c in chains:
        pv = _dot(p_all[c][:, :PAGE_SIZE].astype(bf16), cks[c][0])
        for p in range(1, PP):
            pv = pv + _dot(p_all[c][:, p * PAGE_SIZE:(p + 1) * PAGE_SIZE].astype(bf16), cks[c][p])
        acc_ref[c] = alpha[c] * acc_ref[c] + pv

    @pl.when(g == G - 1)
    def _():
        for c in chains:
            cnew = cnew_ref[bs[c]].astype(bf16).astype(f32)
            pnew = pnew_ref[bs[c]].astype(bf16).astype(f32)
            s_self = (jnp.sum(qlat[c].astype(f32) * cnew, axis=-1, keepdims=True)
                      + jnp.sum(qpe[c].astype(f32) * pnew, axis=-1, keepdims=True))
            m_last = m_ref[c]
            m_fin = jnp.maximum(m_last, s_self)
            p_self = jnp.exp2(s_self - m_fin)
            a_fin = jnp.exp2(m_last - m_fin)
            l_fin = a_fin * l_ref[c] + p_self
            o_ref[bs[c]] = (a_fin * acc_ref[c] + p_self * cnew) / l_fin


def _decode_attention(page_table, qlat, qpe, ckv_new, kpe_new, pool_ckv, pool_kpe_t, pages_per_step=16):
    DB, n_pages = page_table.shape
    PP = math.gcd(n_pages, pages_per_step)
    NC = math.gcd(DB, DECODE_CHAINS)
    H = MLA_HEADS
    vmem = pl.BlockSpec(memory_space=pltpu.VMEM)
    hbm = pl.BlockSpec(memory_space=pl.ANY)
    grid_spec = pltpu.PrefetchScalarGridSpec(
        num_scalar_prefetch=1,
        grid=((DB // NC) * (n_pages // PP),),
        in_specs=[vmem, vmem, vmem, vmem, hbm, hbm],
        out_specs=vmem,
        scratch_shapes=[
            pltpu.VMEM((2, NC * PP, PAGE_SIZE, KV_LORA), pool_ckv.dtype),
            pltpu.VMEM((2, NC * PP, QK_ROPE, PAGE_SIZE), pool_kpe_t.dtype),
            pltpu.SemaphoreType.DMA((2, 2)),
            pltpu.VMEM((NC, H, 1), f32),
            pltpu.VMEM((NC, H, 1), f32),
            pltpu.VMEM((NC, H, KV_LORA), f32),
        ],
    )
    return pl.pallas_call(
        functools.partial(_decode_kernel, pages_per_step=PP, chains=NC),
        grid_spec=grid_spec,
        out_shape=jax.ShapeDtypeStruct((DB, H, KV_LORA), f32),
        compiler_params=_cparams(("arbitrary",)),
        name="mla_decode",
    )(page_table, qlat, qpe, ckv_new, kpe_new, pool_ckv, pool_kpe_t)


def _proj_ffn_kernel(x_ref, o_ref, wo_ref, gffn_ref, win_ref, wout_ref, gfin_ref, out_ref, *, final_norm):
    x1 = x_ref[...] + _dot(o_ref[...], wo_ref[...])
    h = _rms(x1, gffn_ref[...]).astype(bf16)
    acc = x1
    for c in range(FFN_HIDDEN // FFN_TILE):
        lo = c * FFN_TILE
        gate = _dot(h, win_ref[:, lo:lo + FFN_TILE])
        up = _dot(h, win_ref[:, FFN_HIDDEN + lo:FFN_HIDDEN + lo + FFN_TILE])
        act = (gate * jax.nn.sigmoid(gate) * up).astype(bf16)
        acc = acc + _dot(act, wout_ref[lo:lo + FFN_TILE, :])
    if final_norm:
        acc = _rms(acc, gfin_ref[...])
    out_ref[...] = acc


def _proj_ffn(x, o, wo, gffn, w_in, w_out, gfin, final_norm, tm):
    T = x.shape[0]
    row = lambda i: (i, 0)
    held = lambda arr: pl.BlockSpec(arr.shape, lambda i: (0, 0), pipeline_mode=pl.Buffered(1))
    return pl.pallas_call(
        functools.partial(_proj_ffn_kernel, final_norm=final_norm),
        grid=(T // tm,),
        in_specs=[
            pl.BlockSpec((tm, D_MODEL), row),
            pl.BlockSpec((tm, D_MODEL), row),
            held(wo), held(gffn), held(w_in), held(w_out), held(gfin),
        ],
        out_specs=pl.BlockSpec((tm, D_MODEL), row),
        out_shape=jax.ShapeDtypeStruct((T, D_MODEL), f32),
        compiler_params=_cparams(("parallel",)),
        name="proj_ffn",
    )(x, o, wo, gffn, w_in, w_out, gfin)


def _seg_sum(x, ones_blk):
    parts = [_dot(x[:, gidx * LANES:(gidx + 1) * LANES].astype(bf16), ones_blk)
             for gidx in range(x.shape[1] // LANES)]
    return jnp.concatenate(parts, axis=1)


def _rwkv_proj_kernel(x_ref, prev_ref, gmix_ref, mix_ref, wrkv_ref, w0_ref, w1_ref, w2_ref,
                      a0_ref, a1_ref, a2_ref, g1_ref, g2_ref, kk_ref, ka_ref, rk_ref, ones_ref,
                      r_out, k_out, v_out, kn_out, b_out, ld_out, g_out, bonus_out, h_out,
                      *, seq_mode, tiles_per_seq):
    tm = x_ref.shape[0]
    gm = gmix_ref[...]
    h = _rms(x_ref[...], gm)
    if seq_mode:
        hp_row = _rms(prev_ref[...], gm)[7:8, :]
        is_start = (pl.program_id(0) % tiles_per_seq) == 0
        hp_row = jnp.where(is_start, jnp.zeros_like(hp_row), hp_row)
        rowid = lax.broadcasted_iota(jnp.int32, (tm, 1), 0)
        hprev = jnp.where(rowid == 0, hp_row, pltpu.roll(h, 1, 0))
        h_out[0] = h[tm - 8:, :]
    else:
        hprev = prev_ref[...]
        h_out[...] = h
    xx = hprev - h
    mix = mix_ref[...]
    xr, xw, xk, xv, xa, xg = ((h + xx * mix[n:n + 1]).astype(bf16) for n in range(6))
    w_hid = jnp.tanh(_dot(xw, w1_ref[...])).astype(bf16)
    a_hid = _dot(xa, a1_ref[...]).astype(bf16)
    g_hid = jax.nn.sigmoid(_dot(xg, g1_ref[...])).astype(bf16)
    ones_blk = ones_ref[...]
    ct = 2 * LANES
    for j in range(D_MODEL // ct):
        cols = slice(j * ct, (j + 1) * ct)
        r = _dot(xr, wrkv_ref[0, :, cols])
        k = _dot(xk, wrkv_ref[1, :, cols])
        v = _dot(xv, wrkv_ref[2, :, cols])
        wl = w0_ref[:, cols] + _dot(w_hid, w2_ref[:, cols])
        z = -wl
        w_log = -(jnp.maximum(z, 0.0) + jnp.log(1.0 + jnp.exp(-jnp.abs(z)))) - 0.5
        ld_out[:, cols] = -jnp.exp(w_log)
        a = jax.nn.sigmoid(a0_ref[:, cols] + _dot(a_hid, a2_ref[:, cols]))
        g_out[:, cols] = _dot(g_hid, g2_ref[:, cols]).astype(g_out.dtype)
        kk = k * kk_ref[:, cols]
        kk = kk / jnp.maximum(jnp.sqrt(_seg_sum(kk * kk, ones_blk)), 1e-12)
        k = k * (1.0 + (a - 1.0) * ka_ref[:, cols])
        bonus_out[:, cols] = (_seg_sum(r * k * rk_ref[:, cols], ones_blk) * v).astype(bonus_out.dtype)
        r_out[:, cols] = r.astype(r_out.dtype)
        k_out[:, cols] = k.astype(k_out.dtype)
        v_out[:, cols] = v.astype(v_out.dtype)
        kn_out[:, cols] = kk.astype(kn_out.dtype)
        b_out[:, cols] = (kk * a).astype(b_out.dtype)


def _rwkv_proj(x, prev, p, seq_len, tm, vec_dtype):
    T = x.shape[0]
    seq_mode = seq_len > 1
    tiles_per_seq = max(seq_len // tm, 1)
    row = lambda i: (i, 0)
    const = lambda i: (0, 0)
    const3 = lambda i: (0, 0, 0)
    if seq_mode:
        prev_spec = pl.BlockSpec((8, D_MODEL), lambda i: (jnp.maximum(i * (tm // 8) - 1, 0), 0))
        h_spec = pl.BlockSpec((1, 8, D_MODEL), lambda i: (i, 0, 0))
        h_shape = jax.ShapeDtypeStruct((T // tm, 8, D_MODEL), f32)
    else:
        prev_spec = pl.BlockSpec((tm, D_MODEL), row)
        h_spec = pl.BlockSpec((tm, D_MODEL), row)
        h_shape = jax.ShapeDtypeStruct((T, D_MODEL), f32)
    vec = lambda dt: jax.ShapeDtypeStruct((T, D_MODEL), dt)
    tile = pl.BlockSpec((tm, D_MODEL), row)
    full = lambda arr: pl.BlockSpec(arr.shape, const3 if arr.ndim == 3 else const)
    weights = [p["gmix"], p["mix"], p["w_rkv"], p["w0"], p["w1"], p["w2"], p["a0"], p["a1"], p["a2"],
               p["g1"], p["g2"], p["k_k"], p["k_a"], p["r_k"], p["ones_blk"]]
    return pl.pallas_call(
        functools.partial(_rwkv_proj_kernel, seq_mode=seq_mode, tiles_per_seq=tiles_per_seq),
        grid=(T // tm,),
        in_specs=[tile, prev_spec] + [full(w) for w in weights],
        out_specs=[tile] * 8 + [h_spec],
        out_shape=[vec(vec_dtype)] * 5 + [vec(f32), vec(bf16), vec(bf16), h_shape],
        compiler_params=_cparams(("parallel",)),
        name="rwkv_proj",
    )(x, prev, *weights)


def _wkv_chunk_kernel(r_ref, k_ref, v_ref, kn_ref, b_ref, ld_ref, y_ref, st_ref, m_ref, *, chunks, wave):
    C = WKV_CHUNK
    G = WKV_GROUP
    W = GROUP_W
    c_idx = pl.program_id(1)

    @pl.when(c_idx == 0)
    def _():
        m_ref[...] = jnp.zeros(m_ref.shape, f32)

    rr = lax.broadcasted_iota(jnp.int32, (W, W), 0)
    cc = lax.broadcasted_iota(jnp.int32, (W, W), 1)
    bd_mask = (rr // RWKV_HEAD) == (cc // RWKV_HEAD)
    t_idx = lax.broadcasted_iota(jnp.int32, (C, W), 0)
    s_idx = lax.broadcasted_iota(jnp.int32, (C, W), 1) % C
    strict = s_idx < t_idx
    incl = s_idx <= t_idx
    eye_cat = (s_idx == t_idx).astype(f32)
    lane_lo = lax.broadcasted_iota(jnp.int32, (C, LANES), 1) < RWKV_HEAD
    zero_tile = jnp.zeros((C, LANES), bf16)

    def bd(y):
        yb = y.astype(bf16)
        rows = []
        for hh in range(G):
            lt = hh // 2
            t = yb[:, lt * LANES:(lt + 1) * LANES]
            keep = jnp.where(lane_lo, t, zero_tile) if hh % 2 == 0 else jnp.where(lane_lo, zero_tile, t)
            tiles = [zero_tile] * (W // LANES)
            tiles[lt] = keep
            rows.append(jnp.concatenate(tiles, axis=1))
        return jnp.concatenate(rows, axis=0)

    def bdmm(x, y):
        return _dot(x.astype(bf16), bd(y))

    def cumsum_rows(x):
        sh = 1
        while sh < C:
            x = x + jnp.where(t_idx >= sh, pltpu.roll(x, sh, 0), 0.0)
            sh *= 2
        return x

    NG = r_ref.shape[1] // W
    n_waves = chunks // wave

    def precompute(wv, out):
        streams = [(wv * wave + ci, gi) for ci in range(wave) for gi in range(NG)]
        ns = range(len(streams))

        def load(ref, s):
            ci, gi = streams[s]
            return ref[ci * C:(ci + 1) * C, gi * W:(gi + 1) * W]

        ld = [load(ld_ref, s) for s in ns]
        cs = [cumsum_rows(ld[s]) for s in ns]
        tot_col = [jnp.broadcast_to(cs[s][C - 1:C, :], (LANES, W)).T for s in ns]
        yield
        r = [load(r_ref, s).astype(f32) for s in ns]
        k = [load(k_ref, s).astype(f32) for s in ns]
        v = [load(v_ref, s).astype(f32) for s in ns]
        kn = [load(kn_ref, s).astype(f32) for s in ns]
        b = [load(b_ref, s).astype(f32) for s in ns]
        p_inv = [jnp.exp(-cs[s]) for s in ns]
        p_rest = [jnp.exp(cs[s][C - 1:C, :] - cs[s]) for s in ns]
        a_t = [-kn[s] * jnp.exp(cs[s] - ld[s]) for s in ns]
        r_t = [r[s] * jnp.exp(cs[s]) for s in ns]
        lhs = [jnp.concatenate([a_t[s], r_t[s]], axis=0).astype(bf16) for s in ns]
        yield
        s_b = [_dot_nt(lhs[s], bd(b[s] * p_inv[s])) for s in ns]
        yield
        s_k = [_dot_nt(lhs[s], bd(k[s] * p_inv[s])) for s in ns]
        yield
        l_ab = [jnp.where(strict, s_b[s][:C], 0.0) for s in ns]
        l_ak = [jnp.where(strict, s_k[s][:C], 0.0) for s in ns]
        a_rb = [jnp.where(incl, s_b[s][C:], 0.0) for s in ns]
        a_rk = [jnp.where(incl, s_k[s][C:], 0.0) for s in ns]
        t_inv = [eye_cat + l_ab[s] for s in ns]
        pw = [bdmm(l_ab[s], l_ab[s]) for s in ns]
        yield
        step = 2
        while step < C:
            rhs = [bd(pw[s]) for s in ns]
            if 2 * step < C:
                both = [_dot(jnp.concatenate([pw[s], t_inv[s]], axis=0).astype(bf16), rhs[s]) for s in ns]
                pw = [both[s][:C] for s in ns]
                t_inv = [t_inv[s] + both[s][C:] for s in ns]
            else:
                t_inv = [t_inv[s] + _dot(t_inv[s].astype(bf16), rhs[s]) for s in ns]
            step *= 2
            yield
        from_v = [bdmm(jnp.concatenate([l_ak[s], a_rk[s]], axis=0), v[s]) for s in ns]
        yield
        w1 = [bdmm(t_inv[s], a_t[s]) for s in ns]
        yield
        w2 = [bdmm(t_inv[s], from_v[s][:C]) for s in ns]
        out["st_lhs"] = [jnp.concatenate([w1[s], r_t[s]], axis=0).astype(bf16) for s in ns]
        out["w2"] = w2
        out["y_v"] = [from_v[s][C:] for s in ns]
        out["a_rb"] = a_rb
        out["v"] = v
        out["bk_rest"] = [jnp.concatenate([b[s] * p_rest[s], k[s] * p_rest[s]], axis=0).astype(bf16) for s in ns]
        out["decay_col"] = [jnp.exp(jnp.concatenate([tot_col[s]] * (W // LANES), axis=1)) for s in ns]
        yield

    def state_chain(wv, d, m):
        for ci in range(wave):
            c_glob = wv * wave + ci
            ss = [ci * NG + gi for gi in range(NG)]
            from_state = [_dot(d["st_lhs"][s], m[gi].astype(bf16)) for gi, s in enumerate(ss)]
            yield
            u = [from_state[gi][:C] + d["w2"][s] for gi, s in enumerate(ss)]
            y_u = [bdmm(d["a_rb"][s], u[gi]) for gi, s in enumerate(ss)]
            upd = [_dot_tn(d["bk_rest"][s], jnp.concatenate([u[gi], d["v"][s]], axis=0).astype(bf16))
                   for gi, s in enumerate(ss)]
            yield
            for gi, s in enumerate(ss):
                y_ref[c_glob * C:(c_glob + 1) * C, gi * W:(gi + 1) * W] = from_state[gi][C:] + y_u[gi] + d["y_v"][s]
                m[gi] = m[gi] * d["decay_col"][s] + jnp.where(bd_mask, upd[gi], 0.0)
            yield

    front_stages = 5
    m = [m_ref[gi] for gi in range(NG)]
    data = [{} for _ in range(n_waves)]
    pres = [precompute(wv, data[wv]) for wv in range(n_waves)]
    for t in range(n_waves + 2):
        lanes = []
        if 0 <= t - 1 < n_waves:
            lanes.append(pres[t - 1])
        if t < n_waves:
            lanes.append(itertools.islice(pres[t], front_stages))
        if 0 <= t - 2 < n_waves:
            lanes.append(state_chain(t - 2, data[t - 2], m))
        while lanes:
            lanes = [g for g in lanes if next(g, _DONE) is not _DONE]
    for gi in range(NG):
        m_ref[gi] = m[gi]
        acc = m[gi][:RWKV_HEAD]
        for hh in range(1, G):
            acc = acc + m[gi][hh * RWKV_HEAD:(hh + 1) * RWKV_HEAD]
        st_ref[0, :, gi * W:(gi + 1) * W] = acc


def _wkv_chunked(r, k, v, kn, b, ld, B, S):
    T = B * S
    chunks = WKV_CHUNKS_PER_STEP if S % (WKV_CHUNK * WKV_CHUNKS_PER_STEP) == 0 else 1
    wave = WKV_WAVE if chunks % WKV_WAVE == 0 else chunks
    rows = WKV_CHUNK * chunks
    nc = S // rows
    tile = pl.BlockSpec((rows, D_MODEL), lambda bi, ci: (bi * nc + ci, 0))
    return pl.pallas_call(
        functools.partial(_wkv_chunk_kernel, chunks=chunks, wave=wave),
        grid=(B, nc),
        in_specs=[tile] * 6,
        out_specs=[tile, pl.BlockSpec((1, RWKV_HEAD, D_MODEL), lambda bi, ci: (bi, 0, 0))],
        out_shape=[jax.ShapeDtypeStruct((T, D_MODEL), f32),
                   jax.ShapeDtypeStruct((B, RWKV_HEAD, D_MODEL), f32)],
        scratch_shapes=[pltpu.VMEM((D_MODEL // GROUP_W, GROUP_W, GROUP_W), f32)],
        compiler_params=_cparams(("parallel", "arbitrary")),
        name="wkv_chunked",
    )(r, k, v, kn, b, ld)


def _wkv_step_kernel(s_ref, vec_ref, snew_ref, y_ref):
    r, k, kn, b, ld = (vec_ref[i, 0] for i in (0, 1, 3, 4, 5))
    a = -kn
    w = jnp.exp(ld)

    def value_row(vi, c):
        st = s_ref[0, vi]
        sa = jnp.sum(st * a, axis=0, keepdims=True)
        v_row = vec_ref[2, 0, pl.ds(vi, 1), :]
        st = st * w + sa * b + v_row * k
        snew_ref[0, vi] = st
        y_ref[0, pl.ds(vi, 1), :] = jnp.sum(st * r, axis=0, keepdims=True)
        return c

    lax.fori_loop(0, RWKV_HEAD, value_row, 0, unroll=4)


def _wkv_step(state_t, vecs):
    H, N, _, DB = state_t.shape
    st = pl.BlockSpec((1, N, N, DB), lambda h: (h, 0, 0, 0))
    return pl.pallas_call(
        _wkv_step_kernel,
        grid=(H,),
        in_specs=[st, pl.BlockSpec((6, 1, N, DB), lambda h: (0, h, 0, 0))],
        out_specs=[st, pl.BlockSpec((1, N, DB), lambda h: (h, 0, 0))],
        out_shape=[jax.ShapeDtypeStruct(state_t.shape, f32),
                   jax.ShapeDtypeStruct((H, N, DB), f32)],
        compiler_params=_cparams(("parallel",)),
        name="wkv_step",
    )(state_t, vecs)


def _rwkv_post_kernel(y_ref, bonus_ref, g_ref, lnw_ref, lnb_ref, ones_ref, o_ref):
    ones_blk = ones_ref[...]
    y = y_ref[...]
    inv_n = 1.0 / RWKV_HEAD
    mu = _seg_sum(y, ones_blk) * inv_n
    d = y - mu
    var = _seg_sum(d * d, ones_blk) * inv_n
    yn = d * lax.rsqrt(var + GN_EPS) * lnw_ref[...] + lnb_ref[...]
    o_ref[...] = ((yn + bonus_ref[...].astype(f32)) * g_ref[...].astype(f32)).astype(bf16)


def _rwkv_post(y, bonus, g, lnw, lnb, ones_blk, tm):
    T = y.shape[0]
    row = lambda i: (i, 0)
    const = lambda i: (0, 0)
    tile = pl.BlockSpec((tm, D_MODEL), row)
    return pl.pallas_call(
        _rwkv_post_kernel,
        grid=(T // tm,),
        in_specs=[tile, tile, tile, pl.BlockSpec((1, D_MODEL), const), pl.BlockSpec((1, D_MODEL), const),
                  pl.BlockSpec((LANES, LANES), const)],
        out_specs=tile,
        out_shape=jax.ShapeDtypeStruct((T, D_MODEL), bf16),
        compiler_params=_cparams(("parallel",)),
        name="rwkv_post",
    )(y, bonus, g, lnw, lnb, ones_blk)


def _rope_tables(pos):
    half = QK_ROPE // 2
    inv = 1.0 / (ROPE_THETA ** (jnp.arange(0, QK_ROPE, 2, dtype=f32) / QK_ROPE))
    ang = pos[:, None] * inv[None, :]
    cos, sin = jnp.cos(ang), jnp.sin(ang)
    n = pos.shape[0]
    ones = jnp.ones((n, QK_NOPE), f32)
    z = lambda w: jnp.zeros((n, w), f32)
    tab_c = jnp.concatenate([ones, cos, cos, z(HEAD_TILE - QK_NOPE - QK_ROPE)], axis=1)
    tab_s1 = jnp.concatenate([z(QK_NOPE), -sin, z(HEAD_TILE - QK_NOPE - half)], axis=1)
    tab_s2 = jnp.concatenate([z(QK_NOPE + half), sin, z(HEAD_TILE - QK_NOPE - QK_ROPE)], axis=1)
    return tab_c, tab_s1, tab_s2


def _pick_tile(n, pref):
    t = min(n, pref)
    while n % t:
        t //= 2
    return t


def kernel(x_prompt, x_sample, cache_ckv, cache_kpe, state_wkv, state_shift, page_table,
           norm_mix, norm_ffn, norm_final,
           mla_w_a, mla_q_norm, mla_kv_norm, mla_w_uq, mla_w_ukv, mla_w_o,
           rw_mix, rw_w_rkv, rw_w0, rw_w1, rw_w2, rw_a0, rw_a1, rw_a2, rw_g1, rw_g2,
           rw_k_k, rw_k_a, rw_r_k, rw_ln_w, rw_ln_b, rw_w_o,
           ffn_w_in, ffn_w_out):
    B, S, D = x_prompt.shape
    DB, DS, _ = x_sample.shape
    assert D == D_MODEL and DS == 1 and S % WKV_CHUNK == 0
    n_pages = page_table.shape[1]
    past_len = n_pages * PAGE_SIZE
    H = MLA_HEADS
    row = lambda t: t.reshape(1, -1)
    xp = x_prompt.reshape(B * S, D)
    xs = x_sample.reshape(DB, D)

    w_a = mla_w_a[0]
    pad_pe = jnp.zeros((D, HEAD_TILE), f32).at[:, QK_NOPE:QK_NOPE + QK_ROPE].set(w_a[:, Q_LORA + KV_LORA:])
    wa_ext = jnp.concatenate([w_a[:, :Q_LORA + KV_LORA], pad_pe], axis=1).astype(bf16)
    wuq = (mla_w_uq[0] * (MLA_SCALE * LOG2_E)).reshape(Q_LORA, H, QK_NOPE + QK_ROPE)
    wuq = jnp.pad(wuq, ((0, 0), (0, 0), (0, HEAD_TILE - QK_NOPE - QK_ROPE))).reshape(Q_LORA, H * HEAD_TILE).astype(bf16)
    wukv = mla_w_ukv[0].reshape(KV_LORA, H * HEAD_TILE).astype(bf16)
    mla_wo = mla_w_o[0].astype(bf16)
    qn, kvn = row(mla_q_norm[0]), row(mla_kv_norm[0])
    ffn_in = ffn_w_in.astype(bf16)
    ffn_out = ffn_w_out.astype(bf16)
    hh = jnp.arange(LANES) // RWKV_HEAD
    ones_blk = (hh[:, None] == hh[None, :]).astype(bf16)
    rwp = dict(
        gmix=row(norm_mix[1]), mix=jnp.pad(rw_mix[0], ((0, 2), (0, 0))), w_rkv=rw_w_rkv[0].astype(bf16),
        w0=row(rw_w0[0]), w1=rw_w1[0].astype(bf16), w2=rw_w2[0].astype(bf16),
        a0=row(rw_a0[0]), a1=rw_a1[0].astype(bf16), a2=rw_a2[0].astype(bf16),
        g1=rw_g1[0].astype(bf16), g2=rw_g2[0].astype(bf16),
        k_k=row(rw_k_k[0]), k_a=row(rw_k_a[0]), r_k=row(rw_r_k[0]), ones_blk=ones_blk)
    rw_wo = rw_w_o[0].astype(bf16)
    lnw, lnb = row(rw_ln_w[0]), row(rw_ln_b[0])
    gfin = row(norm_final)

    tm_p = _pick_tile(B * S, 512)
    tm_s = _pick_tile(DB, 128)

    tabs_p = _rope_tables(jnp.arange(S, dtype=f32))
    tabs_s = _rope_tables(jnp.full((tm_s,), past_len, f32))
    q_s, _, _, ckv_s, kpe_s = _mla_proj(xs, row(norm_mix[0]), wa_ext, qn, kvn, wuq, wukv, tabs_s, tm_s)
    qlat = jnp.swapaxes(_absorb_q(q_s, wukv), 0, 1)
    qpe = q_s.reshape(DB, H, HEAD_TILE)[:, :, QK_NOPE:QK_NOPE + QK_ROPE]
    o_lat = _decode_attention(page_table, qlat, qpe, ckv_s.reshape(DB, 1, KV_LORA), kpe_s.reshape(DB, 1, QK_ROPE),
                              cache_ckv.reshape(-1, PAGE_SIZE, KV_LORA),
                              jnp.swapaxes(cache_kpe.reshape(-1, PAGE_SIZE, QK_ROPE), 1, 2))
    o_s = _unabsorb_o(jnp.swapaxes(o_lat, 0, 1), wukv)
    o_s = o_s.reshape(DB, H, HEAD_TILE)[:, :, QK_NOPE:].reshape(DB, H * V_HEAD).astype(bf16)
    xs = _proj_ffn(xs, o_s, mla_wo, row(norm_ffn[0]), ffn_in[0], ffn_out[0], gfin, False, tm_s)

    q_p, k_p, kv_p, ckv_p, kpe_p = _mla_proj(xp, row(norm_mix[0]), wa_ext, qn, kvn, wuq, wukv, tabs_p,
                                             _pick_tile(S, 512))
    o_p = _flash_attention(q_p, k_p, kv_p, B, S)
    xp = _proj_ffn(xp, o_p, mla_wo, row(norm_ffn[0]), ffn_in[0], ffn_out[0], gfin, False, tm_p)

    tm_r = _pick_tile(S, 512)
    r, k, v, kn, b, ld, g, bonus, hl = _rwkv_proj(xp, xp, rwp, S, tm_r, bf16)
    shift_p = hl.reshape(B, S // tm_r, 8, D)[:, -1, 7, :]
    y, st = _wkv_chunked(r, k, v, kn, b, ld, B, S)
    wkv_p = jnp.transpose(st.reshape(B, RWKV_HEAD, RWKV_HEADS, RWKV_HEAD), (0, 2, 3, 1))
    yo = _rwkv_post(y, bonus, g, lnw, lnb, ones_blk, _pick_tile(B * S, 1024))
    y_prompt = _proj_ffn(xp, yo, rw_wo, row(norm_ffn[1]), ffn_in[1], ffn_out[1], gfin, True, tm_p)

    r, k, v, kn, b, ld, g, bonus, shift_s = _rwkv_proj(xs, state_shift[0], rwp, 1, tm_s, f32)
    vecs = jnp.transpose(jnp.stack([r, k, v, kn, b, ld]), (0, 2, 1)).reshape(6, RWKV_HEADS, RWKV_HEAD, DB)
    wkv_t, y_t = _wkv_step(jnp.transpose(state_wkv[0].astype(f32), (1, 2, 3, 0)), vecs)
    wkv_s = jnp.transpose(wkv_t, (3, 0, 1, 2)).astype(state_wkv.dtype)
    y = jnp.transpose(y_t.reshape(D, DB))
    yo = _rwkv_post(y, bonus, g, lnw, lnb, ones_blk, tm_s)
    y_sample = _proj_ffn(xs, yo, rw_wo, row(norm_ffn[1]), ffn_in[1], ffn_out[1], gfin, True, tm_s)

    return (y_prompt.reshape(B, S, D), y_sample.reshape(DB, DS, D),
            ckv_p.reshape(1, B, S, KV_LORA), kpe_p.reshape(1, B, S, QK_ROPE),
            ckv_s.reshape(1, DB, DS, KV_LORA), kpe_s.reshape(1, DB, DS, QK_ROPE),
            wkv_p[None].astype(x_prompt.dtype), shift_p[None],
            wkv_s[None], shift_s[None])
```

```python
import functools
import itertools
import math

import jax
import jax.numpy as jnp
from jax import lax
from jax.experimental import pallas as pl
from jax.experimental.pallas import tpu as pltpu

f32 = jnp.float32
bf16 = jnp.bfloat16

D_MODEL = 1024
MLA_HEADS = 16
QK_NOPE = 64
QK_ROPE = 32
V_HEAD = 64
Q_LORA = 512
KV_LORA = 256
ROPE_THETA = 10000.0
MLA_SCALE = 1.0 / math.sqrt(QK_NOPE + QK_ROPE)
LOG2_E = math.log2(math.e)
RWKV_HEAD = 64
RWKV_HEADS = D_MODEL // RWKV_HEAD
FFN_HIDDEN = 2816
FFN_TILE = 256
NORM_EPS = 1e-6
GN_EPS = 64e-5
DECAY_SCALE = math.exp(-0.5)
PAGE_SIZE = 128

LANES = 128
HEAD_TILE = 128
DECODE_CHAINS = 4
WKV_CHUNK = 64
WKV_GROUP = 4
WKV_CHUNKS_PER_STEP = 8
WKV_WAVE = 2
GROUP_W = WKV_GROUP * RWKV_HEAD
VMEM_LIMIT = 56 * 1024 * 1024
_DONE = object()


def _cparams(sem):
    return pltpu.CompilerParams(dimension_semantics=sem, vmem_limit_bytes=VMEM_LIMIT)


def _rms(x, g):
    return x * lax.rsqrt(jnp.mean(x * x, axis=-1, keepdims=True) + NORM_EPS) * g


def _dot(a, b):
    return jnp.dot(a, b, preferred_element_type=f32)


def _dot_nt(a, b):
    return lax.dot_general(a, b, (((1,), (1,)), ((), ())), preferred_element_type=f32)


def _dot_tn(a, b):
    return lax.dot_general(a, b, (((0,), (0,)), ((), ())), preferred_element_type=f32)


def _mla_proj_kernel(x_ref, gmix_ref, wa_ref, qn_ref, kvn_ref, wuq_ref, wukv_ref,
                     c_ref, s1_ref, s2_ref, q_ref, k_ref, kv_ref, ckv_ref, kpe_ref):
    tm = x_ref.shape[0]
    h = _rms(x_ref[...], gmix_ref[...]).astype(bf16)
    a = _dot(h, wa_ref[...])
    cq = _rms(a[:, :Q_LORA], qn_ref[...]).astype(bf16)
    ckv = _rms(a[:, Q_LORA:Q_LORA + KV_LORA], kvn_ref[...])
    ckv_ref[...] = ckv
    cos = c_ref[...]
    sin_lo = s1_ref[...]
    sin_hi = s2_ref[...]

    def rope(t):
        return t * cos + pltpu.roll(t, LANES - QK_ROPE // 2, 1) * sin_lo + pltpu.roll(t, QK_ROPE // 2, 1) * sin_hi

    kpe_t = rope(a[:, Q_LORA + KV_LORA:])
    kpe_ref[...] = kpe_t[:, QK_NOPE:QK_NOPE + QK_ROPE]
    ckv_b = ckv.astype(bf16)
    lane = lax.broadcasted_iota(jnp.int32, (tm, HEAD_TILE), 1)
    for pr in range(MLA_HEADS // 2):
        cols = slice(pr * 2 * HEAD_TILE, (pr + 1) * 2 * HEAD_TILE)
        kv = _dot(ckv_b, wukv_ref[:, cols])
        q = _dot(cq, wuq_ref[:, cols])
        kv_ref[:, cols] = kv.astype(bf16)
        for hh in range(2):
            sl = slice(hh * HEAD_TILE, (hh + 1) * HEAD_TILE)
            out = slice((2 * pr + hh) * HEAD_TILE, (2 * pr + hh + 1) * HEAD_TILE)
            q_ref[:, out] = rope(q[:, sl]).astype(bf16)
            k_ref[:, out] = jnp.where(lane < QK_NOPE, kv[:, sl], kpe_t).astype(bf16)


def _mla_proj(x, gmix, wa, qn, kvn, wuq, wukv, tabs, tm):
    T = x.shape[0]
    tab_c, tab_s1, tab_s2 = tabs
    nt = tab_c.shape[0] // tm
    HW = MLA_HEADS * HEAD_TILE
    const = lambda i: (0, 0)
    row = lambda i: (i, 0)
    tabmap = lambda i: (i % nt, 0)
    return pl.pallas_call(
        _mla_proj_kernel,
        grid=(T // tm,),
        in_specs=[
            pl.BlockSpec((tm, D_MODEL), row),
            pl.BlockSpec((1, D_MODEL), const),
            pl.BlockSpec(wa.shape, const),
            pl.BlockSpec((1, Q_LORA), const),
            pl.BlockSpec((1, KV_LORA), const),
            pl.BlockSpec(wuq.shape, const),
            pl.BlockSpec(wukv.shape, const),
            pl.BlockSpec((tm, HEAD_TILE), tabmap),
            pl.BlockSpec((tm, HEAD_TILE), tabmap),
            pl.BlockSpec((tm, HEAD_TILE), tabmap),
        ],
        out_specs=[
            pl.BlockSpec((tm, HW), row),
            pl.BlockSpec((tm, HW), row),
            pl.BlockSpec((tm, HW), row),
            pl.BlockSpec((tm, KV_LORA), row),
            pl.BlockSpec((tm, QK_ROPE), row),
        ],
        out_shape=[
            jax.ShapeDtypeStruct((T, HW), bf16),
            jax.ShapeDtypeStruct((T, HW), bf16),
            jax.ShapeDtypeStruct((T, HW), bf16),
            jax.ShapeDtypeStruct((T, KV_LORA), f32),
            jax.ShapeDtypeStruct((T, QK_ROPE), f32),
        ],
        compiler_params=_cparams(("parallel",)),
        name="mla_proj",
    )(x, gmix, wa, qn, kvn, wuq, wukv, tab_c, tab_s1, tab_s2)


def _flash_kernel(q_ref, k_ref, kv_ref, o_ref, vt_ref, m_ref, acc_ref, *, tile):
    S = q_ref.shape[0]
    nt = S // tile
    lane = lax.broadcasted_iota(jnp.int32, (tile, HEAD_TILE), 1)
    ones_row = lax.broadcasted_iota(jnp.int32, (HEAD_TILE, tile), 0) == 0
    heads = range(q_ref.shape[1] // HEAD_TILE)
    hs = [slice(hh * HEAD_TILE, (hh + 1) * HEAD_TILE) for hh in heads]

    for hh in heads:
        for j in range(nt):
            vt = kv_ref[j * tile:(j + 1) * tile, hs[hh]].astype(f32).T
            vt_ref[hh, j] = jnp.where(ones_row, 1.0, vt).astype(bf16)

    def q_body(qi, c):
        q0 = qi * tile
        qs = [q_ref[pl.ds(q0, tile), hs[hh]] for hh in heads]
        for hh in heads:
            m_ref[hh] = jnp.full((1, tile), -jnp.inf, f32)
            acc_ref[hh] = jnp.zeros((HEAD_TILE, tile), f32)

        def block(ki, kr, qr, diag):
            k0 = ki * tile + kr.start
            nk = kr.stop - kr.start
            sts = [_dot_nt(k_ref[pl.ds(k0, nk), hs[hh]], qs[hh][qr]) for hh in heads]
            if diag:
                shape = (nk, qr.stop - qr.start)
                visible = lax.broadcasted_iota(jnp.int32, shape, 0) <= lax.broadcasted_iota(jnp.int32, shape, 1)
                sts = [jnp.where(visible, st, -jnp.inf) for st in sts]
            m_prevs = [m_ref[hh, :, qr] for hh in heads]
            m_news = [jnp.maximum(m_prevs[hh], jnp.max(sts[hh], axis=0, keepdims=True)) for hh in heads]
            ps = [jnp.exp2(sts[hh] - m_news[hh]) for hh in heads]
            alphas = [jnp.exp2(m_prevs[hh] - m_news[hh]) for hh in heads]
            pvs = [_dot(vt_ref[hh, ki, :, kr], ps[hh].astype(bf16)) for hh in heads]
            for hh in heads:
                m_ref[hh, :, qr] = m_news[hh]
                acc_ref[hh, :, qr] = alphas[hh] * acc_ref[hh, :, qr] + pvs[hh]

        for ki in range(qi):
            block(ki, slice(0, tile), slice(0, tile), False)
        half = tile // 2
        block(qi, slice(0, half), slice(0, tile), True)
        block(qi, slice(half, tile), slice(half, tile), True)
        accs = [acc_ref[hh] for hh in heads]
        outs = [(a / a[0:1, :]).T for a in accs]
        for pr in range(len(outs) // 2):
            o = jnp.where(lane < V_HEAD, pltpu.roll(outs[2 * pr], V_HEAD, 1), outs[2 * pr + 1])
            o_ref[pl.ds(q0, tile), pr * 2 * V_HEAD:(pr + 1) * 2 * V_HEAD] = o.astype(bf16)
        return c

    for qi in range(nt):
        q_body(qi, 0)


def _flash_attention(q, k, kv, B, S, tile=512, heads_per_step=4):
    T = B * S
    tile = min(tile, S)
    nh = heads_per_step
    blk = pl.BlockSpec((S, nh * HEAD_TILE), lambda b, hp: (b, hp))
    return pl.pallas_call(
        functools.partial(_flash_kernel, tile=tile),
        grid=(B, MLA_HEADS // nh),
        in_specs=[blk, blk, blk],
        out_specs=pl.BlockSpec((S, nh * V_HEAD), lambda b, hp: (b, hp)),
        out_shape=jax.ShapeDtypeStruct((T, MLA_HEADS * V_HEAD), bf16),
        scratch_shapes=[
            pltpu.VMEM((nh, S // tile, HEAD_TILE, tile), bf16),
            pltpu.VMEM((nh, 1, tile), f32),
            pltpu.VMEM((nh, HEAD_TILE, tile), f32),
        ],
        compiler_params=_cparams(("parallel", "parallel")),
        name="mla_flash",
    )(q, k, kv)


def _absorb_q_kernel(q_ref, w_ref, o_ref):
    lane = lax.broadcasted_iota(jnp.int32, q_ref.shape, 1)
    qn = jnp.where(lane < QK_NOPE, q_ref[...], jnp.zeros_like(q_ref[...]))
    o_ref[0] = _dot_nt(qn, w_ref[...]).astype(bf16)


def _absorb_q(q, wukv):
    DB = q.shape[0]
    return pl.pallas_call(
        _absorb_q_kernel,
        grid=(MLA_HEADS,),
        in_specs=[
            pl.BlockSpec((DB, HEAD_TILE), lambda h: (0, h)),
            pl.BlockSpec((KV_LORA, HEAD_TILE), lambda h: (0, h)),
        ],
        out_specs=pl.BlockSpec((1, DB, KV_LORA), lambda h: (h, 0, 0)),
        out_shape=jax.ShapeDtypeStruct((MLA_HEADS, DB, KV_LORA), bf16),
        compiler_params=_cparams(("parallel",)),
        name="mla_absorb_q",
    )(q, wukv)


def _unabsorb_o_kernel(o_ref, w_ref, out_ref):
    out_ref[...] = _dot(o_ref[0].astype(bf16), w_ref[...])


def _unabsorb_o(o_lat, wukv):
    DB = o_lat.shape[1]
    return pl.pallas_call(
        _unabsorb_o_kernel,
        grid=(MLA_HEADS,),
        in_specs=[
            pl.BlockSpec((1, DB, KV_LORA), lambda h: (h, 0, 0)),
            pl.BlockSpec((KV_LORA, HEAD_TILE), lambda h: (0, h)),
        ],
        out_specs=pl.BlockSpec((DB, HEAD_TILE), lambda h: (0, h)),
        out_shape=jax.ShapeDtypeStruct((DB, MLA_HEADS * HEAD_TILE), f32),
        compiler_params=_cparams(("parallel",)),
        name="mla_unabsorb_o",
    )(o_lat, wukv)


def _decode_kernel(pt_ref, qlat_ref, qpe_ref, cnew_ref, pnew_ref, ckv_hbm, kpe_hbm, o_ref,
                   ckv_buf, kpe_buf, sem, m_ref, l_ref, acc_ref, *, pages_per_step, chains):
    PP = pages_per_step
    NC = chains
    DB, n_pages = pt_ref.shape
    G = n_pages // PP
    per_chain = DB // NC
    total = per_chain * G

    def page_copy(page, slot, idx, which):
        if which == 0:
            return pltpu.make_async_copy(ckv_hbm.at[page], ckv_buf.at[slot, idx], sem.at[0, slot])
        return pltpu.make_async_copy(kpe_hbm.at[page], kpe_buf.at[slot, idx], sem.at[1, slot])

    def start_group(it, slot):
        bb = it // G
        g0 = (it % G) * PP
        for c in range(NC):
            for p in range(PP):
                page = pt_ref[c * per_chain + bb, g0 + p]
                page_copy(page, slot, c * PP + p, 0).start()
                page_copy(page, slot, c * PP + p, 1).start()

    def wait_group(slot):
        for idx in range(NC * PP):
            page_copy(0, slot, idx, 0).wait()
            page_copy(0, slot, idx, 1).wait()

    it = pl.program_id(0)
    slot = it % 2

    @pl.when(it == 0)
    def _():
        start_group(0, 0)

    @pl.when(it + 1 < total)
    def _():
        start_group(it + 1, 1 - slot)

    wait_group(slot)
    _decode_group(it // G, it % G, G, per_chain, slot, qlat_ref, qpe_ref, cnew_ref, pnew_ref, o_ref,
                  ckv_buf, kpe_buf, m_ref, l_ref, acc_ref, PP, NC)


def _decode_group(bb, g, G, per_chain, slot, qlat_ref, qpe_ref, cnew_ref, pnew_ref, o_ref,
                  ckv_buf, kpe_buf, m_ref, l_ref, acc_ref, PP, NC):
    chains = range(NC)
    bs = [c * per_chain + bb for c in chains]

    @pl.when(g == 0)
    def _():
        m_ref[...] = jnp.full(m_ref.shape, -jnp.inf, f32)
        l_ref[...] = jnp.zeros(l_ref.shape, f32)
        acc_ref[...] = jnp.zeros(acc_ref.shape, f32)

    qlat = [qlat_ref[bs[c]] for c in chains]
    qpe = [qpe_ref[bs[c]] for c in chains]
    cks = [[ckv_buf[slot, c * PP + p].astype(bf16) for p in range(PP)] for c in chains]
    s = [jnp.concatenate(
        [_dot_nt(qlat[c], cks[c][p]) + _dot(qpe[c], kpe_buf[slot, c * PP + p].astype(bf16)) for p in range(PP)],
        axis=1) for c in chains]
    m_prev = [m_ref[c] for c in chains]
    m_new = [jnp.maximum(m_prev[c], jnp.max(s[c], axis=-1, keepdims=True)) for c in chains]
    p_all = [jnp.exp2(s[c] - m_new[c]) for c in chains]
    alpha = [jnp.exp2(m_prev[c] - m_new[c]) for c in chains]
    for c in chains:
        l_ref[c] = alpha[c] * l_ref[c] + jnp.sum(p_all[c], axis=-1, keepdims=True)
        m_ref[c] = m_new[c]
    for c in chains:
        pv = _dot(p_all[c][:, :PAGE_SIZE].astype(bf16), cks[c][0])
        for p in range(1, PP):
            pv = pv + _dot(p_all[c][:, p * PAGE_SIZE:(p + 1) * PAGE_SIZE].astype(bf16), cks[c][p])
        acc_ref[c] = alpha[c] * acc_ref[c] + pv

    @pl.when(g == G - 1)
    def _():
        for c in chains:
            cnew = cnew_ref[bs[c]].astype(bf16).astype(f32)
            pnew = pnew_ref[bs[c]].astype(bf16).astype(f32)
            s_self = (jnp.sum(qlat[c].astype(f32) * cnew, axis=-1, keepdims=True)
                      + jnp.sum(qpe[c].astype(f32) * pnew, axis=-1, keepdims=True))
            m_last = m_ref[c]
            m_fin = jnp.maximum(m_last, s_self)
            p_self = jnp.exp2(s_self - m_fin)
            a_fin = jnp.exp2(m_last - m_fin)
            l_fin = a_fin * l_ref[c] + p_self
            o_ref[bs[c]] = (a_fin * acc_ref[c] + p_self * cnew) / l_fin


def _decode_attention(page_table, qlat, qpe, ckv_new, kpe_new, pool_ckv, pool_kpe_t, pages_per_step=16):
    DB, n_pages = page_table.shape
    PP = math.gcd(n_pages, pages_per_step)
    NC = math.gcd(DB, DECODE_CHAINS)
    H = MLA_HEADS
    vmem = pl.BlockSpec(memory_space=pltpu.VMEM)
    hbm = pl.BlockSpec(memory_space=pl.ANY)
    grid_spec = pltpu.PrefetchScalarGridSpec(
        num_scalar_prefetch=1,
        grid=((DB // NC) * (n_pages // PP),),
        in_specs=[vmem, vmem, vmem, vmem, hbm, hbm],
        out_specs=vmem,
        scratch_shapes=[
            pltpu.VMEM((2, NC * PP, PAGE_SIZE, KV_LORA), pool_ckv.dtype),
            pltpu.VMEM((2, NC * PP, QK_ROPE, PAGE_SIZE), pool_kpe_t.dtype),
            pltpu.SemaphoreType.DMA((2, 2)),
            pltpu.VMEM((NC, H, 1), f32),
            pltpu.VMEM((NC, H, 1), f32),
            pltpu.VMEM((NC, H, KV_LORA), f32),
        ],
    )
    return pl.pallas_call(
        functools.partial(_decode_kernel, pages_per_step=PP, chains=NC),
        grid_spec=grid_spec,
        out_shape=jax.ShapeDtypeStruct((DB, H, KV_LORA), f32),
        compiler_params=_cparams(("arbitrary",)),
        name="mla_decode",
    )(page_table, qlat, qpe, ckv_new, kpe_new, pool_ckv, pool_kpe_t)


def _proj_ffn_kernel(x_ref, o_ref, wo_ref, gffn_ref, win_ref, wout_ref, gfin_ref, out_ref, *, final_norm):
    x1 = x_ref[...] + _dot(o_ref[...], wo_ref[...])
    h = _rms(x1, gffn_ref[...]).astype(bf16)
    acc = x1
    for c in range(FFN_HIDDEN // FFN_TILE):
        lo = c * FFN_TILE
        gate = _dot(h, win_ref[:, lo:lo + FFN_TILE])
        up = _dot(h, win_ref[:, FFN_HIDDEN + lo:FFN_HIDDEN + lo + FFN_TILE])
        act = (gate * jax.nn.sigmoid(gate) * up).astype(bf16)
        acc = acc + _dot(act, wout_ref[lo:lo + FFN_TILE, :])
    if final_norm:
        acc = _rms(acc, gfin_ref[...])
    out_ref[...] = acc


def _proj_ffn(x, o, wo, gffn, w_in, w_out, gfin, final_norm, tm):
    T = x.shape[0]
    row = lambda i: (i, 0)
    held = lambda arr: pl.BlockSpec(arr.shape, lambda i: (0, 0), pipeline_mode=pl.Buffered(1))
    return pl.pallas_call(
        functools.partial(_proj_ffn_kernel, final_norm=final_norm),
        grid=(T // tm,),
        in_specs=[
            pl.BlockSpec((tm, D_MODEL), row),
            pl.BlockSpec((tm, D_MODEL), row),
            held(wo), held(gffn), held(w_in), held(w_out), held(gfin),
        ],
        out_specs=pl.BlockSpec((tm, D_MODEL), row),
        out_shape=jax.ShapeDtypeStruct((T, D_MODEL), f32),
        compiler_params=_cparams(("parallel",)),
        name="proj_ffn",
    )(x, o, wo, gffn, w_in, w_out, gfin)


def _seg_sum(x, ones_blk):
    parts = [_dot(x[:, gidx * LANES:(gidx + 1) * LANES].astype(bf16), ones_blk)
             for gidx in range(x.shape[1] // LANES)]
    return jnp.concatenate(parts, axis=1)


def _rwkv_proj_kernel(x_ref, prev_ref, gmix_ref, mix_ref, wrkv_ref, w0_ref, w1_ref, w2_ref,
                      a0_ref, a1_ref, a2_ref, g1_ref, g2_ref, kk_ref, ka_ref, rk_ref, ones_ref,
                      r_out, k_out, v_out, kn_out, b_out, ld_out, g_out, bonus_out, h_out,
                      *, seq_mode, tiles_per_seq):
    tm = x_ref.shape[0]
    gm = gmix_ref[...]
    h = _rms(x_ref[...], gm)
    if seq_mode:
        hp_row = _rms(prev_ref[...], gm)[7:8, :]
        is_start = (pl.program_id(0) % tiles_per_seq) == 0
        hp_row = jnp.where(is_start, jnp.zeros_like(hp_row), hp_row)
        rowid = lax.broadcasted_iota(jnp.int32, (tm, 1), 0)
        hprev = jnp.where(rowid == 0, hp_row, pltpu.roll(h, 1, 0))
        h_out[0] = h[tm - 8:, :]
    else:
        hprev = prev_ref[...]
        h_out[...] = h
    xx = hprev - h
    mix = mix_ref[...]
    xr, xw, xk, xv, xa, xg = ((h + xx * mix[n:n + 1]).astype(bf16) for n in range(6))
    w_hid = jnp.tanh(_dot(xw, w1_ref[...])).astype(bf16)
    a_hid = _dot(xa, a1_ref[...]).astype(bf16)
    g_hid = jax.nn.sigmoid(_dot(xg, g1_ref[...])).astype(bf16)
    ones_blk = ones_ref[...]
    ct = 2 * LANES
    for j in range(D_MODEL // ct):
        cols = slice(j * ct, (j + 1) * ct)
        r = _dot(xr, wrkv_ref[0, :, cols])
        k = _dot(xk, wrkv_ref[1, :, cols])
        v = _dot(xv, wrkv_ref[2, :, cols])
        wl = w0_ref[:, cols] + _dot(w_hid, w2_ref[:, cols])
        ld_out[:, cols] = -DECAY_SCALE * jax.nn.sigmoid(wl)
        a = jax.nn.sigmoid(a0_ref[:, cols] + _dot(a_hid, a2_ref[:, cols]))
        g_out[:, cols] = _dot(g_hid, g2_ref[:, cols]).astype(g_out.dtype)
        kk = k * kk_ref[:, cols]
        kk = kk / jnp.maximum(jnp.sqrt(_seg_sum(kk * kk, ones_blk)), 1e-12)
        k = k * (1.0 + (a - 1.0) * ka_ref[:, cols])
        bonus_out[:, cols] = (_seg_sum(r * k * rk_ref[:, cols], ones_blk) * v).astype(bonus_out.dtype)
        r_out[:, cols] = r.astype(r_out.dtype)
        k_out[:, cols] = k.astype(k_out.dtype)
        v_out[:, cols] = v.astype(v_out.dtype)
        kn_out[:, cols] = kk.astype(kn_out.dtype)
        b_out[:, cols] = (kk * a).astype(b_out.dtype)


def _rwkv_proj(x, prev, p, seq_len, tm, vec_dtype):
    T = x.shape[0]
    seq_mode = seq_len > 1
    tiles_per_seq = max(seq_len // tm, 1)
    row = lambda i: (i, 0)
    const = lambda i: (0, 0)
    const3 = lambda i: (0, 0, 0)
    if seq_mode:
        prev_spec = pl.BlockSpec((8, D_MODEL), lambda i: (jnp.maximum(i * (tm // 8) - 1, 0), 0))
        h_spec = pl.BlockSpec((1, 8, D_MODEL), lambda i: (i, 0, 0))
        h_shape = jax.ShapeDtypeStruct((T // tm, 8, D_MODEL), f32)
    else:
        prev_spec = pl.BlockSpec((tm, D_MODEL), row)
        h_spec = pl.BlockSpec((tm, D_MODEL), row)
        h_shape = jax.ShapeDtypeStruct((T, D_MODEL), f32)
    vec = lambda dt: jax.ShapeDtypeStruct((T, D_MODEL), dt)
    tile = pl.BlockSpec((tm, D_MODEL), row)
    full = lambda arr: pl.BlockSpec(arr.shape, const3 if arr.ndim == 3 else const)
    weights = [p["gmix"], p["mix"], p["w_rkv"], p["w0"], p["w1"], p["w2"], p["a0"], p["a1"], p["a2"],
               p["g1"], p["g2"], p["k_k"], p["k_a"], p["r_k"], p["ones_blk"]]
    return pl.pallas_call(
        functools.partial(_rwkv_proj_kernel, seq_mode=seq_mode, tiles_per_seq=tiles_per_seq),
        grid=(T // tm,),
        in_specs=[tile, prev_spec] + [full(w) for w in weights],
        out_specs=[tile] * 8 + [h_spec],
        out_shape=[vec(vec_dtype)] * 5 + [vec(f32), vec(bf16), vec(bf16), h_shape],
        compiler_params=_cparams(("parallel",)),
        name="rwkv_proj",
    )(x, prev, *weights)


def _wkv_chunk_kernel(r_ref, k_ref, v_ref, kn_ref, b_ref, ld_ref, y_ref, st_ref, m_ref, *, chunks, wave):
    C = WKV_CHUNK
    G = WKV_GROUP
    W = GROUP_W
    c_idx = pl.program_id(1)

    @pl.when(c_idx == 0)
    def _():
        m_ref[...] = jnp.zeros(m_ref.shape, f32)

    rr = lax.broadcasted_iota(jnp.int32, (W, W), 0)
    cc = lax.broadcasted_iota(jnp.int32, (W, W), 1)
    bd_mask = (rr // RWKV_HEAD) == (cc // RWKV_HEAD)
    t_idx = lax.broadcasted_iota(jnp.int32, (C, W), 0)
    s_idx = lax.broadcasted_iota(jnp.int32, (C, W), 1) % C
    strict = s_idx < t_idx
    incl = s_idx <= t_idx
    eye_cat = (s_idx == t_idx).astype(f32)
    lane_lo = lax.broadcasted_iota(jnp.int32, (C, LANES), 1) < RWKV_HEAD
    zero_tile = jnp.zeros((C, LANES), bf16)

    def bd(y):
        yb = y.astype(bf16)
        rows = []
        for hh in range(G):
            lt = hh // 2
            t = yb[:, lt * LANES:(lt + 1) * LANES]
            keep = jnp.where(lane_lo, t, zero_tile) if hh % 2 == 0 else jnp.where(lane_lo, zero_tile, t)
            tiles = [zero_tile] * (W // LANES)
            tiles[lt] = keep
            rows.append(jnp.concatenate(tiles, axis=1))
        return jnp.concatenate(rows, axis=0)

    def bdmm(x, y):
        return _dot(x.astype(bf16), bd(y))

    def cumsum_rows(x):
        sh = 1
        while sh < C:
            x = x + jnp.where(t_idx >= sh, pltpu.roll(x, sh, 0), 0.0)
            sh *= 2
        return x

    NG = r_ref.shape[1] // W
    n_waves = chunks // wave

    def precompute(wv, out):
        streams = [(wv * wave + ci, gi) for ci in range(wave) for gi in range(NG)]
        ns = range(len(streams))

        def load(ref, s):
            ci, gi = streams[s]
            return ref[ci * C:(ci + 1) * C, gi * W:(gi + 1) * W]

        ld = [load(ld_ref, s) for s in ns]
        cs = [cumsum_rows(ld[s]) for s in ns]
        tot_col = [jnp.broadcast_to(cs[s][C - 1:C, :], (LANES, W)).T for s in ns]
        yield
        r = [load(r_ref, s).astype(f32) for s in ns]
        k = [load(k_ref, s).astype(f32) for s in ns]
        v = [load(v_ref, s).astype(f32) for s in ns]
        kn = [load(kn_ref, s).astype(f32) for s in ns]
        b = [load(b_ref, s).astype(f32) for s in ns]
        p_inv = [jnp.exp(-cs[s]) for s in ns]
        p_rest = [jnp.exp(cs[s][C - 1:C, :] - cs[s]) for s in ns]
        a_t = [-kn[s] * jnp.exp(cs[s] - ld[s]) for s in ns]
        r_t = [r[s] * jnp.exp(cs[s]) for s in ns]
        lhs = [jnp.concatenate([a_t[s], r_t[s]], axis=0).astype(bf16) for s in ns]
        yield
        s_b = [_dot_nt(lhs[s], bd(b[s] * p_inv[s])) for s in ns]
        yield
        s_k = [_dot_nt(lhs[s], bd(k[s] * p_inv[s])) for s in ns]
        yield
        l_ab = [jnp.where(strict, s_b[s][:C], 0.0) for s in ns]
        l_ak = [jnp.where(strict, s_k[s][:C], 0.0) for s in ns]
        a_rb = [jnp.where(incl, s_b[s][C:], 0.0) for s in ns]
        a_rk = [jnp.where(incl, s_k[s][C:], 0.0) for s in ns]
        t_inv = [eye_cat + l_ab[s] for s in ns]
        pw = [bdmm(l_ab[s], l_ab[s]) for s in ns]
        yield
        step = 2
        while step < C:
            rhs = [bd(pw[s]) for s in ns]
            if 2 * step < C:
                both = [_dot(jnp.concatenate([pw[s], t_inv[s]], axis=0).astype(bf16), rhs[s]) for s in ns]
                pw = [both[s][:C] for s in ns]
                t_inv = [t_inv[s] + both[s][C:] for s in ns]
            else:
                t_inv = [t_inv[s] + _dot(t_inv[s].astype(bf16), rhs[s]) for s in ns]
            step *= 2
            yield
        from_v = [bdmm(jnp.concatenate([l_ak[s], a_rk[s]], axis=0), v[s]) for s in ns]
        yield
        w1 = [bdmm(t_inv[s], a_t[s]) for s in ns]
        yield
        w2 = [bdmm(t_inv[s], from_v[s][:C]) for s in ns]
        out["st_lhs"] = [jnp.concatenate([w1[s], r_t[s]], axis=0).astype(bf16) for s in ns]
        out["w2"] = w2
        out["y_v"] = [from_v[s][C:] for s in ns]
        out["a_rb"] = a_rb
        out["v"] = v
        out["bk_rest"] = [jnp.concatenate([b[s] * p_rest[s], k[s] * p_rest[s]], axis=0).astype(bf16) for s in ns]
        out["decay_col"] = [jnp.exp(jnp.concatenate([tot_col[s]] * (W // LANES), axis=1)) for s in ns]
        yield

    def state_chain(wv, d, m):
        for ci in range(wave):
            c_glob = wv * wave + ci
            ss = [ci * NG + gi for gi in range(NG)]
            from_state = [_dot(d["st_lhs"][s], m[gi].astype(bf16)) for gi, s in enumerate(ss)]
            yield
            u = [from_state[gi][:C] + d["w2"][s] for gi, s in enumerate(ss)]
            y_u = [bdmm(d["a_rb"][s], u[gi]) for gi, s in enumerate(ss)]
            upd = [_dot_tn(d["bk_rest"][s], jnp.concatenate([u[gi], d["v"][s]], axis=0).astype(bf16))
                   for gi, s in enumerate(ss)]
            yield
            for gi, s in enumerate(ss):
                y_ref[c_glob * C:(c_glob + 1) * C, gi * W:(gi + 1) * W] = from_state[gi][C:] + y_u[gi] + d["y_v"][s]
                m[gi] = m[gi] * d["decay_col"][s] + jnp.where(bd_mask, upd[gi], 0.0)
            yield

    front_stages = 5
    m = [m_ref[gi] for gi in range(NG)]
    data = [{} for _ in range(n_waves)]
    pres = [precompute(wv, data[wv]) for wv in range(n_waves)]
    for t in range(n_waves + 2):
        lanes = []
        if 0 <= t - 1 < n_waves:
            lanes.append(pres[t - 1])
        if t < n_waves:
            lanes.append(itertools.islice(pres[t], front_stages))
        if 0 <= t - 2 < n_waves:
            lanes.append(state_chain(t - 2, data[t - 2], m))
        while lanes:
            lanes = [g for g in lanes if next(g, _DONE) is not _DONE]
    for gi in range(NG):
        m_ref[gi] = m[gi]
        acc = m[gi][:RWKV_HEAD]
        for hh in range(1, G):
            acc = acc + m[gi][hh * RWKV_HEAD:(hh + 1) * RWKV_HEAD]
        st_ref[0, :, gi * W:(gi + 1) * W] = acc


def _wkv_chunked(r, k, v, kn, b, ld, B, S):
    T = B * S
    chunks = WKV_CHUNKS_PER_STEP if S % (WKV_CHUNK * WKV_CHUNKS_PER_STEP) == 0 else 1
    wave = WKV_WAVE if chunks % WKV_WAVE == 0 else chunks
    rows = WKV_CHUNK * chunks
    nc = S // rows
    tile = pl.BlockSpec((rows, D_MODEL), lambda bi, ci: (bi * nc + ci, 0))
    return pl.pallas_call(
        functools.partial(_wkv_chunk_kernel, chunks=chunks, wave=wave),
        grid=(B, nc),
        in_specs=[tile] * 6,
        out_specs=[tile, pl.BlockSpec((1, RWKV_HEAD, D_MODEL), lambda bi, ci: (bi, 0, 0))],
        out_shape=[jax.ShapeDtypeStruct((T, D_MODEL), f32),
                   jax.ShapeDtypeStruct((B, RWKV_HEAD, D_MODEL), f32)],
        scratch_shapes=[pltpu.VMEM((D_MODEL // GROUP_W, GROUP_W, GROUP_W), f32)],
        compiler_params=_cparams(("parallel", "arbitrary")),
        name="wkv_chunked",
    )(r, k, v, kn, b, ld)


def _wkv_step_kernel(s_ref, vec_ref, snew_ref, y_ref):
    r, k, kn, b, ld = (vec_ref[i, 0] for i in (0, 1, 3, 4, 5))
    a = -kn
    w = jnp.exp(ld)

    def value_row(vi, c):
        st = s_ref[0, vi]
        sa = jnp.sum(st * a, axis=0, keepdims=True)
        v_row = vec_ref[2, 0, pl.ds(vi, 1), :]
        st = st * w + sa * b + v_row * k
        snew_ref[0, vi] = st
        y_ref[0, pl.ds(vi, 1), :] = jnp.sum(st * r, axis=0, keepdims=True)
        return c

    lax.fori_loop(0, RWKV_HEAD, value_row, 0, unroll=4)


def _wkv_step(state_t, vecs):
    H, N, _, DB = state_t.shape
    st = pl.BlockSpec((1, N, N, DB), lambda h: (h, 0, 0, 0))
    return pl.pallas_call(
        _wkv_step_kernel,
        grid=(H,),
        in_specs=[st, pl.BlockSpec((6, 1, N, DB), lambda h: (0, h, 0, 0))],
        out_specs=[st, pl.BlockSpec((1, N, DB), lambda h: (h, 0, 0))],
        out_shape=[jax.ShapeDtypeStruct(state_t.shape, f32),
                   jax.ShapeDtypeStruct((H, N, DB), f32)],
        compiler_params=_cparams(("parallel",)),
        name="wkv_step",
    )(state_t, vecs)


def _rwkv_post_kernel(y_ref, bonus_ref, g_ref, lnw_ref, lnb_ref, ones_ref, o_ref):
    ones_blk = ones_ref[...]
    y = y_ref[...]
    inv_n = 1.0 / RWKV_HEAD
    mu = _seg_sum(y, ones_blk) * inv_n
    d = y - mu
    var = _seg_sum(d * d, ones_blk) * inv_n
    yn = d * lax.rsqrt(var + GN_EPS) * lnw_ref[...] + lnb_ref[...]
    o_ref[...] = ((yn + bonus_ref[...].astype(f32)) * g_ref[...].astype(f32)).astype(bf16)


def _rwkv_post(y, bonus, g, lnw, lnb, ones_blk, tm):
    T = y.shape[0]
    row = lambda i: (i, 0)
    const = lambda i: (0, 0)
    tile = pl.BlockSpec((tm, D_MODEL), row)
    return pl.pallas_call(
        _rwkv_post_kernel,
        grid=(T // tm,),
        in_specs=[tile, tile, tile, pl.BlockSpec((1, D_MODEL), const), pl.BlockSpec((1, D_MODEL), const),
                  pl.BlockSpec((LANES, LANES), const)],
        out_specs=tile,
        out_shape=jax.ShapeDtypeStruct((T, D_MODEL), bf16),
        compiler_params=_cparams(("parallel",)),
        name="rwkv_post",
    )(y, bonus, g, lnw, lnb, ones_blk)


def _rope_tables(pos):
    half = QK_ROPE // 2
    inv = 1.0 / (ROPE_THETA ** (jnp.arange(0, QK_ROPE, 2, dtype=f32) / QK_ROPE))
    ang = pos[:, None] * inv[None, :]
    cos, sin = jnp.cos(ang), jnp.sin(ang)
    n = pos.shape[0]
    ones = jnp.ones((n, QK_NOPE), f32)
    z = lambda w: jnp.zeros((n, w), f32)
    tab_c = jnp.concatenate([ones, cos, cos, z(HEAD_TILE - QK_NOPE - QK_ROPE)], axis=1)
    tab_s1 = jnp.concatenate([z(QK_NOPE), -sin, z(HEAD_TILE - QK_NOPE - half)], axis=1)
    tab_s2 = jnp.concatenate([z(QK_NOPE + half), sin, z(HEAD_TILE - QK_NOPE - QK_ROPE)], axis=1)
    return tab_c, tab_s1, tab_s2


def _pick_tile(n, pref):
    t = min(n, pref)
    while n % t:
        t //= 2
    return t


def kernel(x_prompt, x_sample, cache_ckv, cache_kpe, state_wkv, state_shift, page_table,
           norm_mix, norm_ffn, norm_final,
           mla_w_a, mla_q_norm, mla_kv_norm, mla_w_uq, mla_w_ukv, mla_w_o,
           rw_mix, rw_w_rkv, rw_w0, rw_w1, rw_w2, rw_a0, rw_a1, rw_a2, rw_g1, rw_g2,
           rw_k_k, rw_k_a, rw_r_k, rw_ln_w, rw_ln_b, rw_w_o,
           ffn_w_in, ffn_w_out):
    B, S, D = x_prompt.shape
    DB, DS, _ = x_sample.shape
    assert D == D_MODEL and DS == 1 and S % WKV_CHUNK == 0
    n_pages = page_table.shape[1]
    past_len = n_pages * PAGE_SIZE
    H = MLA_HEADS
    row = lambda t: t.reshape(1, -1)
    xp = x_prompt.reshape(B * S, D)
    xs = x_sample.reshape(DB, D)

    w_a = mla_w_a[0]
    pad_pe = jnp.zeros((D, HEAD_TILE), f32).at[:, QK_NOPE:QK_NOPE + QK_ROPE].set(w_a[:, Q_LORA + KV_LORA:])
    wa_ext = jnp.concatenate([w_a[:, :Q_LORA + KV_LORA], pad_pe], axis=1).astype(bf16)
    wuq = (mla_w_uq[0] * (MLA_SCALE * LOG2_E)).reshape(Q_LORA, H, QK_NOPE + QK_ROPE)
    wuq = jnp.pad(wuq, ((0, 0), (0, 0), (0, HEAD_TILE - QK_NOPE - QK_ROPE))).reshape(Q_LORA, H * HEAD_TILE).astype(bf16)
    wukv = mla_w_ukv[0].reshape(KV_LORA, H * HEAD_TILE).astype(bf16)
    mla_wo = mla_w_o[0].astype(bf16)
    qn, kvn = row(mla_q_norm[0]), row(mla_kv_norm[0])
    ffn_in = [ffn_w_in[i].astype(bf16) for i in range(ffn_w_in.shape[0])]
    ffn_out = [ffn_w_out[i].astype(bf16) for i in range(ffn_w_out.shape[0])]
    hh = jnp.arange(LANES) // RWKV_HEAD
    ones_blk = (hh[:, None] == hh[None, :]).astype(bf16)
    rwp = dict(
        gmix=row(norm_mix[1]), mix=jnp.pad(rw_mix[0], ((0, 2), (0, 0))), w_rkv=rw_w_rkv[0].astype(bf16),
        w0=row(rw_w0[0]), w1=rw_w1[0].astype(bf16), w2=rw_w2[0].astype(bf16),
        a0=row(rw_a0[0]), a1=rw_a1[0].astype(bf16), a2=rw_a2[0].astype(bf16),
        g1=rw_g1[0].astype(bf16), g2=rw_g2[0].astype(bf16),
        k_k=row(rw_k_k[0]), k_a=row(rw_k_a[0]), r_k=row(rw_r_k[0]), ones_blk=ones_blk)
    rw_wo = rw_w_o[0].astype(bf16)
    lnw, lnb = row(rw_ln_w[0]), row(rw_ln_b[0])
    gfin = row(norm_final)

    tm_p = _pick_tile(B * S, 512)
    tm_s = _pick_tile(DB, 128)

    tabs_p = _rope_tables(jnp.arange(S, dtype=f32))
    tabs_s = _rope_tables(jnp.full((tm_s,), past_len, f32))
    q_s, _, _, ckv_s, kpe_s = _mla_proj(xs, row(norm_mix[0]), wa_ext, qn, kvn, wuq, wukv, tabs_s, tm_s)
    qlat = jnp.swapaxes(_absorb_q(q_s, wukv), 0, 1)
    qpe = q_s.reshape(DB, H, HEAD_TILE)[:, :, QK_NOPE:QK_NOPE + QK_ROPE]
    o_lat = _decode_attention(page_table, qlat, qpe, ckv_s.reshape(DB, 1, KV_LORA), kpe_s.reshape(DB, 1, QK_ROPE),
                              cache_ckv.reshape(-1, PAGE_SIZE, KV_LORA),
                              jnp.swapaxes(cache_kpe.reshape(-1, PAGE_SIZE, QK_ROPE), 1, 2))
    o_s = _unabsorb_o(jnp.swapaxes(o_lat, 0, 1), wukv)
    o_s = o_s.reshape(DB, H, HEAD_TILE)[:, :, QK_NOPE:].reshape(DB, H * V_HEAD).astype(bf16)
    xs = _proj_ffn(xs, o_s, mla_wo, row(norm_ffn[0]), ffn_in[0], ffn_out[0], gfin, False, tm_s)

    q_p, k_p, kv_p, ckv_p, kpe_p = _mla_proj(xp, row(norm_mix[0]), wa_ext, qn, kvn, wuq, wukv, tabs_p,
                                             _pick_tile(S, 512))
    o_p = _flash_attention(q_p, k_p, kv_p, B, S)
    xp = _proj_ffn(xp, o_p, mla_wo, row(norm_ffn[0]), ffn_in[0], ffn_out[0], gfin, False, tm_p)

    tm_r = _pick_tile(S, 512)
    r, k, v, kn, b, ld, g, bonus, hl = _rwkv_proj(xp, xp, rwp, S, tm_r, bf16)
    shift_p = hl.reshape(B, S // tm_r, 8, D)[:, -1, 7, :]
    y, st = _wkv_chunked(r, k, v, kn, b, ld, B, S)
    wkv_p = jnp.transpose(st.reshape(B, RWKV_HEAD, RWKV_HEADS, RWKV_HEAD), (0, 2, 3, 1))
    yo = _rwkv_post(y, bonus, g, lnw, lnb, ones_blk, _pick_tile(B * S, 1024))
    y_prompt = _proj_ffn(xp, yo, rw_wo, row(norm_ffn[1]), ffn_in[1], ffn_out[1], gfin, True, tm_p)

    r, k, v, kn, b, ld, g, bonus, shift_s = _rwkv_proj(xs, state_shift[0], rwp, 1, tm_s, f32)
    vecs = jnp.transpose(jnp.stack([r, k, v, kn, b, ld]), (0, 2, 1)).reshape(6, RWKV_HEADS, RWKV_HEAD, DB)
    wkv_t, y_t = _wkv_step(jnp.transpose(state_wkv[0].astype(f32), (1, 2, 3, 0)), vecs)
    wkv_s = jnp.transpose(wkv_t, (3, 0, 1, 2)).astype(state_wkv.dtype)
    y = jnp.transpose(y_t.reshape(D, DB))
    yo = _rwkv_post(y, bonus, g, lnw, lnb, ones_blk, tm_s)
    y_sample = _proj_ffn(xs, yo, rw_wo, row(norm_ffn[1]), ffn_in[1], ffn_out[1], gfin, True, tm_s)

    return (y_prompt.reshape(B, S, D), y_sample.reshape(DB, DS, D),
            ckv_p.reshape(1, B, S, KV_LORA), kpe_p.reshape(1, B, S, QK_ROPE),
            ckv_s.reshape(1, DB, DS, KV_LORA), kpe_s.reshape(1, DB, DS, QK_ROPE),
            wkv_p[None].astype(x_prompt.dtype), shift_p[None],
            wkv_s[None], shift_s[None])
```

```python
import functools
import itertools
import math

import jax
import jax.numpy as jnp
from jax import lax
from jax.experimental import pallas as pl
from jax.experimental.pallas import tpu as pltpu

f32 = jnp.float32
bf16 = jnp.bfloat16

D_MODEL = 1024
MLA_HEADS = 16
QK_NOPE = 64
QK_ROPE = 32
V_HEAD = 64
Q_LORA = 512
KV_LORA = 256
ROPE_THETA = 10000.0
MLA_SCALE = 1.0 / math.sqrt(QK_NOPE + QK_ROPE)
LOG2_E = math.log2(math.e)
RWKV_HEAD = 64
RWKV_HEADS = D_MODEL // RWKV_HEAD
FFN_HIDDEN = 2816
FFN_TILE = 256
NORM_EPS = 1e-6
GN_EPS = 64e-5
DECAY_SCALE = math.exp(-0.5)
PAGE_SIZE = 128

LANES = 128
HEAD_TILE = 128
DECODE_CHAINS = 4
WKV_CHUNK = 64
WKV_GROUP = 4
WKV_CHUNKS_PER_STEP = 8
WKV_WAVE = 2
GROUP_W = WKV_GROUP * RWKV_HEAD
VMEM_LIMIT = 56 * 1024 * 1024
_DONE = object()


def _cparams(sem):
    return pltpu.CompilerParams(dimension_semantics=sem, vmem_limit_bytes=VMEM_LIMIT)


def _rms(x, g):
    return x * lax.rsqrt(jnp.mean(x * x, axis=-1, keepdims=True) + NORM_EPS) * g


def _dot(a, b):
    return jnp.dot(a, b, preferred_element_type=f32)


def _dot_nt(a, b):
    return lax.dot_general(a, b, (((1,), (1,)), ((), ())), preferred_element_type=f32)


def _dot_tn(a, b):
    return lax.dot_general(a, b, (((0,), (0,)), ((), ())), preferred_element_type=f32)


def _mla_proj_kernel(x_ref, gmix_ref, wa_ref, qn_ref, kvn_ref, wuq_ref, wukv_ref,
                     c_ref, s1_ref, s2_ref, q_ref, k_ref, kv_ref, ckv_ref, kpe_ref):
    tm = x_ref.shape[0]
    h = _rms(x_ref[...], gmix_ref[...]).astype(bf16)
    a = _dot(h, wa_ref[...])
    cq = _rms(a[:, :Q_LORA], qn_ref[...]).astype(bf16)
    ckv = _rms(a[:, Q_LORA:Q_LORA + KV_LORA], kvn_ref[...])
    ckv_ref[...] = ckv
    cos = c_ref[...]
    sin_lo = s1_ref[...]
    sin_hi = s2_ref[...]

    def rope(t):
        return t * cos + pltpu.roll(t, LANES - QK_ROPE // 2, 1) * sin_lo + pltpu.roll(t, QK_ROPE // 2, 1) * sin_hi

    kpe_t = rope(a[:, Q_LORA + KV_LORA:])
    kpe_ref[...] = kpe_t[:, QK_NOPE:QK_NOPE + QK_ROPE]
    ckv_b = ckv.astype(bf16)
    lane = lax.broadcasted_iota(jnp.int32, (tm, HEAD_TILE), 1)
    for pr in range(MLA_HEADS // 2):
        cols = slice(pr * 2 * HEAD_TILE, (pr + 1) * 2 * HEAD_TILE)
        kv = _dot(ckv_b, wukv_ref[:, cols])
        q = _dot(cq, wuq_ref[:, cols])
        kv_ref[:, cols] = kv.astype(bf16)
        for hh in range(2):
            sl = slice(hh * HEAD_TILE, (hh + 1) * HEAD_TILE)
            out = slice((2 * pr + hh) * HEAD_TILE, (2 * pr + hh + 1) * HEAD_TILE)
            q_ref[:, out] = rope(q[:, sl]).astype(bf16)
            k_ref[:, out] = jnp.where(lane < QK_NOPE, kv[:, sl], kpe_t).astype(bf16)


def _mla_proj(x, gmix, wa, qn, kvn, wuq, wukv, tabs, tm):
    T = x.shape[0]
    tab_c, tab_s1, tab_s2 = tabs
    nt = tab_c.shape[0] // tm
    HW = MLA_HEADS * HEAD_TILE
    const = lambda i: (0, 0)
    row = lambda i: (i, 0)
    tabmap = lambda i: (i % nt, 0)
    return pl.pallas_call(
        _mla_proj_kernel,
        grid=(T // tm,),
        in_specs=[
            pl.BlockSpec((tm, D_MODEL), row),
            pl.BlockSpec((1, D_MODEL), const),
            pl.BlockSpec(wa.shape, const),
            pl.BlockSpec((1, Q_LORA), const),
            pl.BlockSpec((1, KV_LORA), const),
            pl.BlockSpec(wuq.shape, const),
            pl.BlockSpec(wukv.shape, const),
            pl.BlockSpec((tm, HEAD_TILE), tabmap),
            pl.BlockSpec((tm, HEAD_TILE), tabmap),
            pl.BlockSpec((tm, HEAD_TILE), tabmap),
        ],
        out_specs=[
            pl.BlockSpec((tm, HW), row),
            pl.BlockSpec((tm, HW), row),
            pl.BlockSpec((tm, HW), row),
            pl.BlockSpec((tm, KV_LORA), row),
            pl.BlockSpec((tm, QK_ROPE), row),
        ],
        out_shape=[
            jax.ShapeDtypeStruct((T, HW), bf16),
            jax.ShapeDtypeStruct((T, HW), bf16),
            jax.ShapeDtypeStruct((T, HW), bf16),
            jax.ShapeDtypeStruct((T, KV_LORA), f32),
            jax.ShapeDtypeStruct((T, QK_ROPE), f32),
        ],
        compiler_params=_cparams(("parallel",)),
        name="mla_proj",
    )(x, gmix, wa, qn, kvn, wuq, wukv, tab_c, tab_s1, tab_s2)


def _flash_kernel(q_ref, k_ref, kv_ref, o_ref, vt_ref, m_ref, acc_ref, *, tile):
    S = q_ref.shape[0]
    nt = S // tile
    lane = lax.broadcasted_iota(jnp.int32, (tile, HEAD_TILE), 1)
    ones_row = lax.broadcasted_iota(jnp.int32, (HEAD_TILE, tile), 0) == 0
    heads = range(q_ref.shape[1] // HEAD_TILE)
    hs = [slice(hh * HEAD_TILE, (hh + 1) * HEAD_TILE) for hh in heads]

    for hh in heads:
        for j in range(nt):
            vt = kv_ref[j * tile:(j + 1) * tile, hs[hh]].astype(f32).T
            vt_ref[hh, j] = jnp.where(ones_row, 1.0, vt).astype(bf16)

    def q_tile(qi):
        q0 = qi * tile
        qs = [q_ref[pl.ds(q0, tile), hs[hh]] for hh in heads]
        for hh in heads:
            m_ref[hh] = jnp.full((1, tile), -jnp.inf, f32)
            acc_ref[hh] = jnp.zeros((HEAD_TILE, tile), f32)

        def block(ki, kr, qr, diag):
            k0 = ki * tile + kr.start
            nk = kr.stop - kr.start
            sts = [_dot_nt(k_ref[pl.ds(k0, nk), hs[hh]], qs[hh][qr]) for hh in heads]
            if diag:
                shape = (nk, qr.stop - qr.start)
                visible = lax.broadcasted_iota(jnp.int32, shape, 0) <= lax.broadcasted_iota(jnp.int32, shape, 1)
                sts = [jnp.where(visible, st, -jnp.inf) for st in sts]
            m_prevs = [m_ref[hh, :, qr] for hh in heads]
            m_news = [jnp.maximum(m_prevs[hh], jnp.max(sts[hh], axis=0, keepdims=True)) for hh in heads]
            ps = [jnp.exp2(sts[hh] - m_news[hh]) for hh in heads]
            alphas = [jnp.exp2(m_prevs[hh] - m_news[hh]) for hh in heads]
            pvs = [_dot(vt_ref[hh, ki, :, kr], ps[hh].astype(bf16)) for hh in heads]
            for hh in heads:
                m_ref[hh, :, qr] = m_news[hh]
                acc_ref[hh, :, qr] = alphas[hh] * acc_ref[hh, :, qr] + pvs[hh]

        for ki in range(qi):
            block(ki, slice(0, tile), slice(0, tile), False)
        half = tile // 2
        block(qi, slice(0, half), slice(0, tile), True)
        block(qi, slice(half, tile), slice(half, tile), True)
        accs = [acc_ref[hh] for hh in heads]
        outs = [(a / a[0:1, :]).T for a in accs]
        for pr in range(len(outs) // 2):
            o = jnp.where(lane < V_HEAD, pltpu.roll(outs[2 * pr], V_HEAD, 1), outs[2 * pr + 1])
            o_ref[pl.ds(q0, tile), pr * 2 * V_HEAD:(pr + 1) * 2 * V_HEAD] = o.astype(bf16)

    for qi in range(nt):
        q_tile(qi)


def _flash_attention(q, k, kv, B, S, tile=512, heads_per_step=4):
    T = B * S
    tile = min(tile, S)
    nh = heads_per_step
    blk = pl.BlockSpec((S, nh * HEAD_TILE), lambda b, hp: (b, hp))
    return pl.pallas_call(
        functools.partial(_flash_kernel, tile=tile),
        grid=(B, MLA_HEADS // nh),
        in_specs=[blk, blk, blk],
        out_specs=pl.BlockSpec((S, nh * V_HEAD), lambda b, hp: (b, hp)),
        out_shape=jax.ShapeDtypeStruct((T, MLA_HEADS * V_HEAD), bf16),
        scratch_shapes=[
            pltpu.VMEM((nh, S // tile, HEAD_TILE, tile), bf16),
            pltpu.VMEM((nh, 1, tile), f32),
            pltpu.VMEM((nh, HEAD_TILE, tile), f32),
        ],
        compiler_params=_cparams(("parallel", "parallel")),
        name="mla_flash",
    )(q, k, kv)


def _absorb_q_kernel(q_ref, w_ref, o_ref):
    lane = lax.broadcasted_iota(jnp.int32, q_ref.shape, 1)
    qn = jnp.where(lane < QK_NOPE, q_ref[...], jnp.zeros_like(q_ref[...]))
    o_ref[0] = _dot_nt(qn, w_ref[...]).astype(bf16)


def _absorb_q(q, wukv):
    DB = q.shape[0]
    return pl.pallas_call(
        _absorb_q_kernel,
        grid=(MLA_HEADS,),
        in_specs=[
            pl.BlockSpec((DB, HEAD_TILE), lambda h: (0, h)),
            pl.BlockSpec((KV_LORA, HEAD_TILE), lambda h: (0, h)),
        ],
        out_specs=pl.BlockSpec((1, DB, KV_LORA), lambda h: (h, 0, 0)),
        out_shape=jax.ShapeDtypeStruct((MLA_HEADS, DB, KV_LORA), bf16),
        compiler_params=_cparams(("parallel",)),
        name="mla_absorb_q",
    )(q, wukv)


def _unabsorb_o_kernel(o_ref, w_ref, out_ref):
    out_ref[...] = _dot(o_ref[0].astype(bf16), w_ref[...])


def _unabsorb_o(o_lat, wukv):
    DB = o_lat.shape[1]
    return pl.pallas_call(
        _unabsorb_o_kernel,
        grid=(MLA_HEADS,),
        in_specs=[
            pl.BlockSpec((1, DB, KV_LORA), lambda h: (h, 0, 0)),
            pl.BlockSpec((KV_LORA, HEAD_TILE), lambda h: (0, h)),
        ],
        out_specs=pl.BlockSpec((DB, HEAD_TILE), lambda h: (0, h)),
        out_shape=jax.ShapeDtypeStruct((DB, MLA_HEADS * HEAD_TILE), f32),
        compiler_params=_cparams(("parallel",)),
        name="mla_unabsorb_o",
    )(o_lat, wukv)


def _decode_kernel(pt_ref, qlat_ref, qpe_ref, cnew_ref, pnew_ref, ckv_hbm, kpe_hbm, o_ref,
                   ckv_buf, kpe_buf, sem, m_ref, l_ref, acc_ref, *, pages_per_step, chains):
    PP = pages_per_step
    NC = chains
    DB, n_pages = pt_ref.shape
    G = n_pages // PP
    per_chain = DB // NC
    total = per_chain * G

    def page_copy(page, slot, idx, which):
        if which == 0:
            return pltpu.make_async_copy(ckv_hbm.at[page], ckv_buf.at[slot, idx], sem.at[0, slot])
        return pltpu.make_async_copy(kpe_hbm.at[page], kpe_buf.at[slot, idx], sem.at[1, slot])

    def start_group(it, slot):
        bb = it // G
        g0 = (it % G) * PP
        for c in range(NC):
            for p in range(PP):
                page = pt_ref[c * per_chain + bb, g0 + p]
                page_copy(page, slot, c * PP + p, 0).start()
                page_copy(page, slot, c * PP + p, 1).start()

    def wait_group(slot):
        for idx in range(NC * PP):
            page_copy(0, slot, idx, 0).wait()
            page_copy(0, slot, idx, 1).wait()

    it = pl.program_id(0)
    slot = it % 2

    @pl.when(it == 0)
    def _():
        start_group(0, 0)

    @pl.when(it + 1 < total)
    def _():
        start_group(it + 1, 1 - slot)

    wait_group(slot)
    _decode_group(it // G, it % G, G, per_chain, slot, qlat_ref, qpe_ref, cnew_ref, pnew_ref, o_ref,
                  ckv_buf, kpe_buf, m_ref, l_ref, acc_ref, PP, NC)


def _decode_group(bb, g, G, per_chain, slot, qlat_ref, qpe_ref, cnew_ref, pnew_ref, o_ref,
                  ckv_buf, kpe_buf, m_ref, l_ref, acc_ref, PP, NC):
    chains = range(NC)
    bs = [c * per_chain + bb for c in chains]

    @pl.when(g == 0)
    def _():
        m_ref[...] = jnp.full(m_ref.shape, -jnp.inf, f32)
        l_ref[...] = jnp.zeros(l_ref.shape, f32)
        acc_ref[...] = jnp.zeros(acc_ref.shape, f32)

    qlat = [qlat_ref[bs[c]] for c in chains]
    qpe = [qpe_ref[bs[c]] for c in chains]
    cks = [[ckv_buf[slot, c * PP + p].astype(bf16) for p in range(PP)] for c in chains]
    s = [jnp.concatenate(
        [_dot_nt(qlat[c], cks[c][p]) + _dot(qpe[c], kpe_buf[slot, c * PP + p].astype(bf16)) for p in range(PP)],
        axis=1) for c in chains]
    m_prev = [m_ref[c] for c in chains]
    m_new = [jnp.maximum(m_prev[c], jnp.max(s[c], axis=-1, keepdims=True)) for c in chains]
    p_all = [jnp.exp2(s[c] - m_new[c]) for c in chains]
    alpha = [jnp.exp2(m_prev[c] - m_new[c]) for c in chains]
    for c in chains:
        l_ref[c] = alpha[c] * l_ref[c] + jnp.sum(p_all[c], axis=-1, keepdims=True)
        m_ref[c] = m_new[c]
    for c in chains:
        pv = _dot(p_all[c][:, :PAGE_SIZE].astype(bf16), cks[c][0])
        for p in range(1, PP):
            pv = pv + _dot(p_all[c][:, p * PAGE_SIZE:(p + 1) * PAGE_SIZE].astype(bf16), cks[c][p])
        acc_ref[c] = alpha[c] * acc_ref[c] + pv

    @pl.when(g == G - 1)
    def _():
        for c in chains:
            cnew = cnew_ref[bs[c]].astype(bf16).astype(f32)
            pnew = pnew_ref[bs[c]].astype(bf16).astype(f32)
            s_self = (jnp.sum(qlat[c].astype(f32) * cnew, axis=-1, keepdims=True)
                      + jnp.sum(qpe[c].astype(f32) * pnew, axis=-1, keepdims=True))
            m_last = m_ref[c]
            m_fin = jnp.maximum(m_last, s_self)
            p_self = jnp.exp2(s_self - m_fin)
            a_fin = jnp.exp2(m_last - m_fin)
            l_fin = a_fin * l_ref[c] + p_self
            o_ref[bs[c]] = (a_fin * acc_ref[c] + p_self * cnew) / l_fin


def _decode_attention(page_table, qlat, qpe, ckv_new, kpe_new, pool_ckv, pool_kpe_t, pages_per_step=16):
    DB, n_pages = page_table.shape
    PP = math.gcd(n_pages, pages_per_step)
    NC = math.gcd(DB, DECODE_CHAINS)
    H = MLA_HEADS
    vmem = pl.BlockSpec(memory_space=pltpu.VMEM)
    hbm = pl.BlockSpec(memory_space=pl.ANY)
    grid_spec = pltpu.PrefetchScalarGridSpec(
        num_scalar_prefetch=1,
        grid=((DB // NC) * (n_pages // PP),),
        in_specs=[vmem, vmem, vmem, vmem, hbm, hbm],
        out_specs=vmem,
        scratch_shapes=[
            pltpu.VMEM((2, NC * PP, PAGE_SIZE, KV_LORA), pool_ckv.dtype),
            pltpu.VMEM((2, NC * PP, QK_ROPE, PAGE_SIZE), pool_kpe_t.dtype),
            pltpu.SemaphoreType.DMA((2, 2)),
            pltpu.VMEM((NC, H, 1), f32),
            pltpu.VMEM((NC, H, 1), f32),
            pltpu.VMEM((NC, H, KV_LORA), f32),
        ],
    )
    return pl.pallas_call(
        functools.partial(_decode_kernel, pages_per_step=PP, chains=NC),
        grid_spec=grid_spec,
        out_shape=jax.ShapeDtypeStruct((DB, H, KV_LORA), f32),
        compiler_params=_cparams(("arbitrary",)),
        name="mla_decode",
    )(page_table, qlat, qpe, ckv_new, kpe_new, pool_ckv, pool_kpe_t)


def _proj_ffn_kernel(x_ref, o_ref, wo_ref, gffn_ref, win_ref, wout_ref, gfin_ref, out_ref, *, final_norm):
    x1 = x_ref[...] + _dot(o_ref[...], wo_ref[...])
    h = _rms(x1, gffn_ref[...]).astype(bf16)
    acc = x1
    for c in range(FFN_HIDDEN // FFN_TILE):
        lo = c * FFN_TILE
        gate = _dot(h, win_ref[:, lo:lo + FFN_TILE])
        up = _dot(h, win_ref[:, FFN_HIDDEN + lo:FFN_HIDDEN + lo + FFN_TILE])
        act = (gate * jax.nn.sigmoid(gate) * up).astype(bf16)
        acc = acc + _dot(act, wout_ref[lo:lo + FFN_TILE, :])
    if final_norm:
        acc = _rms(acc, gfin_ref[...])
    out_ref[...] = acc


def _proj_ffn(x, o, wo, gffn, w_in, w_out, gfin, final_norm, tm):
    T = x.shape[0]
    row = lambda i: (i, 0)
    held = lambda arr: pl.BlockSpec(arr.shape, lambda i: (0, 0), pipeline_mode=pl.Buffered(1))
    return pl.pallas_call(
        functools.partial(_proj_ffn_kernel, final_norm=final_norm),
        grid=(T // tm,),
        in_specs=[
            pl.BlockSpec((tm, D_MODEL), row),
            pl.BlockSpec((tm, D_MODEL), row),
            held(wo), held(gffn), held(w_in), held(w_out), held(gfin),
        ],
        out_specs=pl.BlockSpec((tm, D_MODEL), row),
        out_shape=jax.ShapeDtypeStruct((T, D_MODEL), f32),
        compiler_params=_cparams(("parallel",)),
        name="proj_ffn",
    )(x, o, wo, gffn, w_in, w_out, gfin)


def _seg_sum(x, ones_blk):
    parts = [_dot(x[:, gidx * LANES:(gidx + 1) * LANES].astype(bf16), ones_blk)
             for gidx in range(x.shape[1] // LANES)]
    return jnp.concatenate(parts, axis=1)


def _rwkv_proj_kernel(x_ref, prev_ref, gmix_ref, mix_ref, wrkv_ref, w0_ref, w1_ref, w2_ref,
                      a0_ref, a1_ref, a2_ref, g1_ref, g2_ref, kk_ref, ka_ref, rk_ref, ones_ref,
                      r_out, k_out, v_out, kn_out, b_out, ld_out, g_out, bonus_out, h_out,
                      *, seq_mode, tiles_per_seq):
    tm = x_ref.shape[0]
    gm = gmix_ref[...]
    h = _rms(x_ref[...], gm)
    if seq_mode:
        hp_row = _rms(prev_ref[...], gm)[7:8, :]
        is_start = (pl.program_id(0) % tiles_per_seq) == 0
        hp_row = jnp.where(is_start, jnp.zeros_like(hp_row), hp_row)
        rowid = lax.broadcasted_iota(jnp.int32, (tm, 1), 0)
        hprev = jnp.where(rowid == 0, hp_row, pltpu.roll(h, 1, 0))
        h_out[0] = h[tm - 8:, :]
    else:
        hprev = prev_ref[...]
        h_out[...] = h
    xx = hprev - h
    mix = mix_ref[...]
    xr, xw, xk, xv, xa, xg = ((h + xx * mix[n:n + 1]).astype(bf16) for n in range(6))
    w_hid = jnp.tanh(_dot(xw, w1_ref[...])).astype(bf16)
    a_hid = _dot(xa, a1_ref[...]).astype(bf16)
    g_hid = jax.nn.sigmoid(_dot(xg, g1_ref[...])).astype(bf16)
    ones_blk = ones_ref[...]
    ct = 2 * LANES
    for j in range(D_MODEL // ct):
        cols = slice(j * ct, (j + 1) * ct)
        r = _dot(xr, wrkv_ref[0, :, cols])
        k = _dot(xk, wrkv_ref[1, :, cols])
        v = _dot(xv, wrkv_ref[2, :, cols])
        wl = w0_ref[:, cols] + _dot(w_hid, w2_ref[:, cols])
        ld_out[:, cols] = -DECAY_SCALE * jax.nn.sigmoid(wl)
        a = jax.nn.sigmoid(a0_ref[:, cols] + _dot(a_hid, a2_ref[:, cols]))
        g_out[:, cols] = _dot(g_hid, g2_ref[:, cols]).astype(g_out.dtype)
        kk = k * kk_ref[:, cols]
        kk = kk * jnp.minimum(lax.rsqrt(_seg_sum(kk * kk, ones_blk)), 1e12)
        k = k * (1.0 + (a - 1.0) * ka_ref[:, cols])
        bonus_out[:, cols] = (_seg_sum(r * k * rk_ref[:, cols], ones_blk) * v).astype(bonus_out.dtype)
        r_out[:, cols] = r.astype(r_out.dtype)
        k_out[:, cols] = k.astype(k_out.dtype)
        v_out[:, cols] = v.astype(v_out.dtype)
        kn_out[:, cols] = kk.astype(kn_out.dtype)
        b_out[:, cols] = (kk * a).astype(b_out.dtype)


def _rwkv_proj(x, prev, p, seq_len, tm, vec_dtype):
    T = x.shape[0]
    seq_mode = seq_len > 1
    tiles_per_seq = max(seq_len // tm, 1)
    row = lambda i: (i, 0)
    const = lambda i: (0, 0)
    const3 = lambda i: (0, 0, 0)
    if seq_mode:
        prev_spec = pl.BlockSpec((8, D_MODEL), lambda i: (jnp.maximum(i * (tm // 8) - 1, 0), 0))
        h_spec = pl.BlockSpec((1, 8, D_MODEL), lambda i: (i, 0, 0))
        h_shape = jax.ShapeDtypeStruct((T // tm, 8, D_MODEL), f32)
    else:
        prev_spec = pl.BlockSpec((tm, D_MODEL), row)
        h_spec = pl.BlockSpec((tm, D_MODEL), row)
        h_shape = jax.ShapeDtypeStruct((T, D_MODEL), f32)
    vec = lambda dt: jax.ShapeDtypeStruct((T, D_MODEL), dt)
    tile = pl.BlockSpec((tm, D_MODEL), row)
    full = lambda arr: pl.BlockSpec(arr.shape, const3 if arr.ndim == 3 else const)
    weights = [p["gmix"], p["mix"], p["w_rkv"], p["w0"], p["w1"], p["w2"], p["a0"], p["a1"], p["a2"],
               p["g1"], p["g2"], p["k_k"], p["k_a"], p["r_k"], p["ones_blk"]]
    return pl.pallas_call(
        functools.partial(_rwkv_proj_kernel, seq_mode=seq_mode, tiles_per_seq=tiles_per_seq),
        grid=(T // tm,),
        in_specs=[tile, prev_spec] + [full(w) for w in weights],
        out_specs=[tile] * 8 + [h_spec],
        out_shape=[vec(vec_dtype)] * 5 + [vec(f32), vec(bf16), vec(bf16), h_shape],
        compiler_params=_cparams(("parallel",)),
        name="rwkv_proj",
    )(x, prev, *weights)


def _wkv_chunk_kernel(r_ref, k_ref, v_ref, kn_ref, b_ref, ld_ref, y_ref, st_ref, m_ref, *, chunks, wave):
    C = WKV_CHUNK
    G = WKV_GROUP
    W = GROUP_W
    c_idx = pl.program_id(1)

    @pl.when(c_idx == 0)
    def _():
        m_ref[...] = jnp.zeros(m_ref.shape, f32)

    rr = lax.broadcasted_iota(jnp.int32, (W, W), 0)
    cc = lax.broadcasted_iota(jnp.int32, (W, W), 1)
    bd_mask = (rr // RWKV_HEAD) == (cc // RWKV_HEAD)
    t_idx = lax.broadcasted_iota(jnp.int32, (C, W), 0)
    s_idx = lax.broadcasted_iota(jnp.int32, (C, W), 1) % C
    strict = s_idx < t_idx
    incl = s_idx <= t_idx
    eye_cat = (s_idx == t_idx).astype(f32)
    lane_lo = lax.broadcasted_iota(jnp.int32, (C, LANES), 1) < RWKV_HEAD
    zero_tile = jnp.zeros((C, LANES), bf16)

    def bd(y):
        yb = y.astype(bf16)
        rows = []
        for hh in range(G):
            lt = hh // 2
            t = yb[:, lt * LANES:(lt + 1) * LANES]
            keep = jnp.where(lane_lo, t, zero_tile) if hh % 2 == 0 else jnp.where(lane_lo, zero_tile, t)
            tiles = [zero_tile] * (W // LANES)
            tiles[lt] = keep
            rows.append(jnp.concatenate(tiles, axis=1))
        return jnp.concatenate(rows, axis=0)

    def bdmm(x, y):
        return _dot(x.astype(bf16), bd(y))

    def cumsum_rows(x):
        sh = 1
        while sh < C:
            x = x + jnp.where(t_idx >= sh, pltpu.roll(x, sh, 0), 0.0)
            sh *= 2
        return x

    NG = r_ref.shape[1] // W
    n_waves = chunks // wave

    def precompute(wv, out):
        streams = [(wv * wave + ci, gi) for ci in range(wave) for gi in range(NG)]
        ns = range(len(streams))

        def load(ref, s):
            ci, gi = streams[s]
            return ref[ci * C:(ci + 1) * C, gi * W:(gi + 1) * W]

        ld = [load(ld_ref, s) for s in ns]
        cs = [cumsum_rows(ld[s]) for s in ns]
        tot_col = [jnp.broadcast_to(cs[s][C - 1:C, :], (LANES, W)).T for s in ns]
        yield
        r = [load(r_ref, s).astype(f32) for s in ns]
        k = [load(k_ref, s).astype(f32) for s in ns]
        v = [load(v_ref, s).astype(f32) for s in ns]
        kn = [load(kn_ref, s).astype(f32) for s in ns]
        b = [load(b_ref, s).astype(f32) for s in ns]
        p_inv = [jnp.exp(-cs[s]) for s in ns]
        p_rest = [jnp.exp(cs[s][C - 1:C, :] - cs[s]) for s in ns]
        a_t = [-kn[s] * jnp.exp(cs[s] - ld[s]) for s in ns]
        r_t = [r[s] * jnp.exp(cs[s]) for s in ns]
        lhs = [jnp.concatenate([a_t[s], r_t[s]], axis=0).astype(bf16) for s in ns]
        yield
        s_b = [_dot_nt(lhs[s], bd(b[s] * p_inv[s])) for s in ns]
        yield
        s_k = [_dot_nt(lhs[s], bd(k[s] * p_inv[s])) for s in ns]
        yield
        l_ab = [jnp.where(strict, s_b[s][:C], 0.0) for s in ns]
        l_ak = [jnp.where(strict, s_k[s][:C], 0.0) for s in ns]
        a_rb = [jnp.where(incl, s_b[s][C:], 0.0) for s in ns]
        a_rk = [jnp.where(incl, s_k[s][C:], 0.0) for s in ns]
        t_inv = [eye_cat + l_ab[s] for s in ns]
        pw = [bdmm(l_ab[s], l_ab[s]) for s in ns]
        yield
        step = 2
        while step < C:
            rhs = [bd(pw[s]) for s in ns]
            if 2 * step < C:
                both = [_dot(jnp.concatenate([pw[s], t_inv[s]], axis=0).astype(bf16), rhs[s]) for s in ns]
                pw = [both[s][:C] for s in ns]
                t_inv = [t_inv[s] + both[s][C:] for s in ns]
            else:
                t_inv = [t_inv[s] + _dot(t_inv[s].astype(bf16), rhs[s]) for s in ns]
            step *= 2
            yield
        from_v = [bdmm(jnp.concatenate([l_ak[s], a_rk[s]], axis=0), v[s]) for s in ns]
        yield
        w1 = [bdmm(t_inv[s], a_t[s]) for s in ns]
        yield
        w2 = [bdmm(t_inv[s], from_v[s][:C]) for s in ns]
        out["st_lhs"] = [jnp.concatenate([w1[s], r_t[s]], axis=0).astype(bf16) for s in ns]
        out["w2"] = w2
        out["y_v"] = [from_v[s][C:] for s in ns]
        out["a_rb"] = a_rb
        out["v"] = v
        out["bk_rest"] = [jnp.concatenate([b[s] * p_rest[s], k[s] * p_rest[s]], axis=0).astype(bf16) for s in ns]
        out["decay_col"] = [jnp.exp(jnp.concatenate([tot_col[s]] * (W // LANES), axis=1)) for s in ns]
        yield

    def state_chain(wv, d, m):
        for ci in range(wave):
            c_glob = wv * wave + ci
            ss = [ci * NG + gi for gi in range(NG)]
            from_state = [_dot(d["st_lhs"][s], m[gi].astype(bf16)) for gi, s in enumerate(ss)]
            yield
            u = [from_state[gi][:C] + d["w2"][s] for gi, s in enumerate(ss)]
            y_u = [bdmm(d["a_rb"][s], u[gi]) for gi, s in enumerate(ss)]
            upd = [_dot_tn(d["bk_rest"][s], jnp.concatenate([u[gi], d["v"][s]], axis=0).astype(bf16))
                   for gi, s in enumerate(ss)]
            yield
            for gi, s in enumerate(ss):
                y_ref[c_glob * C:(c_glob + 1) * C, gi * W:(gi + 1) * W] = from_state[gi][C:] + y_u[gi] + d["y_v"][s]
                m[gi] = m[gi] * d["decay_col"][s] + jnp.where(bd_mask, upd[gi], 0.0)
            yield

    front_stages = 5
    m = [m_ref[gi] for gi in range(NG)]
    data = [{} for _ in range(n_waves)]
    pres = [precompute(wv, data[wv]) for wv in range(n_waves)]
    for t in range(n_waves + 2):
        lanes = []
        if 0 <= t - 1 < n_waves:
            lanes.append(pres[t - 1])
        if t < n_waves:
            lanes.append(itertools.islice(pres[t], front_stages))
        if 0 <= t - 2 < n_waves:
            lanes.append(state_chain(t - 2, data[t - 2], m))
        while lanes:
            lanes = [g for g in lanes if next(g, _DONE) is not _DONE]
    for gi in range(NG):
        m_ref[gi] = m[gi]
        acc = m[gi][:RWKV_HEAD]
        for hh in range(1, G):
            acc = acc + m[gi][hh * RWKV_HEAD:(hh + 1) * RWKV_HEAD]
        st_ref[0, :, gi * W:(gi + 1) * W] = acc


def _wkv_chunked(r, k, v, kn, b, ld, B, S):
    T = B * S
    chunks = WKV_CHUNKS_PER_STEP if S % (WKV_CHUNK * WKV_CHUNKS_PER_STEP) == 0 else 1
    wave = WKV_WAVE if chunks % WKV_WAVE == 0 else chunks
    rows = WKV_CHUNK * chunks
    nc = S // rows
    tile = pl.BlockSpec((rows, D_MODEL), lambda bi, ci: (bi * nc + ci, 0))
    return pl.pallas_call(
        functools.partial(_wkv_chunk_kernel, chunks=chunks, wave=wave),
        grid=(B, nc),
        in_specs=[tile] * 6,
        out_specs=[tile, pl.BlockSpec((1, RWKV_HEAD, D_MODEL), lambda bi, ci: (bi, 0, 0))],
        out_shape=[jax.ShapeDtypeStruct((T, D_MODEL), f32),
                   jax.ShapeDtypeStruct((B, RWKV_HEAD, D_MODEL), f32)],
        scratch_shapes=[pltpu.VMEM((D_MODEL // GROUP_W, GROUP_W, GROUP_W), f32)],
        compiler_params=_cparams(("parallel", "arbitrary")),
        name="wkv_chunked",
    )(r, k, v, kn, b, ld)


def _wkv_step_kernel(s_ref, vec_ref, snew_ref, y_ref):
    r, k, kn, b, ld = (vec_ref[i, 0] for i in (0, 1, 3, 4, 5))
    a = -kn
    w = jnp.exp(ld)

    def value_row(vi, c):
        st = s_ref[0, vi]
        sa = jnp.sum(st * a, axis=0, keepdims=True)
        v_row = vec_ref[2, 0, pl.ds(vi, 1), :]
        st = st * w + sa * b + v_row * k
        snew_ref[0, vi] = st
        y_ref[0, pl.ds(vi, 1), :] = jnp.sum(st * r, axis=0, keepdims=True)
        return c

    lax.fori_loop(0, RWKV_HEAD, value_row, 0, unroll=4)


def _wkv_step(state_t, vecs):
    H, N, _, DB = state_t.shape
    st = pl.BlockSpec((1, N, N, DB), lambda h: (h, 0, 0, 0))
    return pl.pallas_call(
        _wkv_step_kernel,
        grid=(H,),
        in_specs=[st, pl.BlockSpec((6, 1, N, DB), lambda h: (0, h, 0, 0))],
        out_specs=[st, pl.BlockSpec((1, N, DB), lambda h: (h, 0, 0))],
        out_shape=[jax.ShapeDtypeStruct(state_t.shape, f32),
                   jax.ShapeDtypeStruct((H, N, DB), f32)],
        compiler_params=_cparams(("parallel",)),
        name="wkv_step",
    )(state_t, vecs)


def _rwkv_post_kernel(y_ref, bonus_ref, g_ref, lnw_ref, lnb_ref, ones_ref, o_ref):
    ones_blk = ones_ref[...]
    y = y_ref[...]
    inv_n = 1.0 / RWKV_HEAD
    mu = _seg_sum(y, ones_blk) * inv_n
    d = y - mu
    var = _seg_sum(d * d, ones_blk) * inv_n
    yn = d * lax.rsqrt(var + GN_EPS) * lnw_ref[...] + lnb_ref[...]
    o_ref[...] = ((yn + bonus_ref[...].astype(f32)) * g_ref[...].astype(f32)).astype(bf16)


def _rwkv_post(y, bonus, g, lnw, lnb, ones_blk, tm):
    T = y.shape[0]
    row = lambda i: (i, 0)
    const = lambda i: (0, 0)
    tile = pl.BlockSpec((tm, D_MODEL), row)
    return pl.pallas_call(
        _rwkv_post_kernel,
        grid=(T // tm,),
        in_specs=[tile, tile, tile, pl.BlockSpec((1, D_MODEL), const), pl.BlockSpec((1, D_MODEL), const),
                  pl.BlockSpec((LANES, LANES), const)],
        out_specs=tile,
        out_shape=jax.ShapeDtypeStruct((T, D_MODEL), bf16),
        compiler_params=_cparams(("parallel",)),
        name="rwkv_post",
    )(y, bonus, g, lnw, lnb, ones_blk)


def _rope_tables(pos):
    half = QK_ROPE // 2
    inv = 1.0 / (ROPE_THETA ** (jnp.arange(0, QK_ROPE, 2, dtype=f32) / QK_ROPE))
    ang = pos[:, None] * inv[None, :]
    cos, sin = jnp.cos(ang), jnp.sin(ang)
    n = pos.shape[0]
    ones = jnp.ones((n, QK_NOPE), f32)
    z = lambda w: jnp.zeros((n, w), f32)
    tab_c = jnp.concatenate([ones, cos, cos, z(HEAD_TILE - QK_NOPE - QK_ROPE)], axis=1)
    tab_s1 = jnp.concatenate([z(QK_NOPE), -sin, z(HEAD_TILE - QK_NOPE - half)], axis=1)
    tab_s2 = jnp.concatenate([z(QK_NOPE + half), sin, z(HEAD_TILE - QK_NOPE - QK_ROPE)], axis=1)
    return tab_c, tab_s1, tab_s2


def _pick_tile(n, pref):
    t = min(n, pref)
    while n % t:
        t //= 2
    return t


def kernel(x_prompt, x_sample, cache_ckv, cache_kpe, state_wkv, state_shift, page_table,
           norm_mix, norm_ffn, norm_final,
           mla_w_a, mla_q_norm, mla_kv_norm, mla_w_uq, mla_w_ukv, mla_w_o,
           rw_mix, rw_w_rkv, rw_w0, rw_w1, rw_w2, rw_a0, rw_a1, rw_a2, rw_g1, rw_g2,
           rw_k_k, rw_k_a, rw_r_k, rw_ln_w, rw_ln_b, rw_w_o,
           ffn_w_in, ffn_w_out):
    B, S, D = x_prompt.shape
    DB, DS, _ = x_sample.shape
    assert D == D_MODEL and DS == 1 and S % WKV_CHUNK == 0
    n_pages = page_table.shape[1]
    past_len = n_pages * PAGE_SIZE
    H = MLA_HEADS
    row = lambda t: t.reshape(1, -1)
    xp = x_prompt.reshape(B * S, D)
    xs = x_sample.reshape(DB, D)

    w_a = mla_w_a[0]
    pad_pe = jnp.zeros((D, HEAD_TILE), f32).at[:, QK_NOPE:QK_NOPE + QK_ROPE].set(w_a[:, Q_LORA + KV_LORA:])
    wa_ext = jnp.concatenate([w_a[:, :Q_LORA + KV_LORA], pad_pe], axis=1).astype(bf16)
    wuq = (mla_w_uq[0] * (MLA_SCALE * LOG2_E)).reshape(Q_LORA, H, QK_NOPE + QK_ROPE)
    wuq = jnp.pad(wuq, ((0, 0), (0, 0), (0, HEAD_TILE - QK_NOPE - QK_ROPE))).reshape(Q_LORA, H * HEAD_TILE).astype(bf16)
    wukv = mla_w_ukv[0].reshape(KV_LORA, H * HEAD_TILE).astype(bf16)
    mla_wo = mla_w_o[0].astype(bf16)
    qn, kvn = row(mla_q_norm[0]), row(mla_kv_norm[0])
    ffn_in = [ffn_w_in[i].astype(bf16) for i in range(ffn_w_in.shape[0])]
    ffn_out = [ffn_w_out[i].astype(bf16) for i in range(ffn_w_out.shape[0])]
    hh = jnp.arange(LANES) // RWKV_HEAD
    ones_blk = (hh[:, None] == hh[None, :]).astype(bf16)
    rwp = dict(
        gmix=row(norm_mix[1]), mix=jnp.pad(rw_mix[0], ((0, 2), (0, 0))), w_rkv=rw_w_rkv[0].astype(bf16),
        w0=row(rw_w0[0]), w1=rw_w1[0].astype(bf16), w2=rw_w2[0].astype(bf16),
        a0=row(rw_a0[0]), a1=rw_a1[0].astype(bf16), a2=rw_a2[0].astype(bf16),
        g1=rw_g1[0].astype(bf16), g2=rw_g2[0].astype(bf16),
        k_k=row(rw_k_k[0]), k_a=row(rw_k_a[0]), r_k=row(rw_r_k[0]), ones_blk=ones_blk)
    rw_wo = rw_w_o[0].astype(bf16)
    lnw, lnb = row(rw_ln_w[0]), row(rw_ln_b[0])
    gfin = row(norm_final)

    tm_p = _pick_tile(B * S, 512)
    tm_s = _pick_tile(DB, 128)

    tabs_p = _rope_tables(jnp.arange(S, dtype=f32))
    tabs_s = _rope_tables(jnp.full((tm_s,), past_len, f32))
    q_s, _, _, ckv_s, kpe_s = _mla_proj(xs, row(norm_mix[0]), wa_ext, qn, kvn, wuq, wukv, tabs_s, tm_s)
    qlat = jnp.swapaxes(_absorb_q(q_s, wukv), 0, 1)
    qpe = q_s.reshape(DB, H, HEAD_TILE)[:, :, QK_NOPE:QK_NOPE + QK_ROPE]
    o_lat = _decode_attention(page_table, qlat, qpe, ckv_s.reshape(DB, 1, KV_LORA), kpe_s.reshape(DB, 1, QK_ROPE),
                              cache_ckv.reshape(-1, PAGE_SIZE, KV_LORA),
                              jnp.swapaxes(cache_kpe.reshape(-1, PAGE_SIZE, QK_ROPE), 1, 2))
    o_s = _unabsorb_o(jnp.swapaxes(o_lat, 0, 1), wukv)
    o_s = o_s.reshape(DB, H, HEAD_TILE)[:, :, QK_NOPE:].reshape(DB, H * V_HEAD).astype(bf16)
    xs = _proj_ffn(xs, o_s, mla_wo, row(norm_ffn[0]), ffn_in[0], ffn_out[0], gfin, False, tm_s)

    q_p, k_p, kv_p, ckv_p, kpe_p = _mla_proj(xp, row(norm_mix[0]), wa_ext, qn, kvn, wuq, wukv, tabs_p,
                                             _pick_tile(S, 512))
    o_p = _flash_attention(q_p, k_p, kv_p, B, S)
    xp = _proj_ffn(xp, o_p, mla_wo, row(norm_ffn[0]), ffn_in[0], ffn_out[0], gfin, False, tm_p)

    tm_r = _pick_tile(S, 512)
    r, k, v, kn, b, ld, g, bonus, hl = _rwkv_proj(xp, xp, rwp, S, tm_r, bf16)
    shift_p = hl.reshape(B, S // tm_r, 8, D)[:, -1, 7, :]
    y, st = _wkv_chunked(r, k, v, kn, b, ld, B, S)
    wkv_p = jnp.transpose(st.reshape(B, RWKV_HEAD, RWKV_HEADS, RWKV_HEAD), (0, 2, 3, 1))
    yo = _rwkv_post(y, bonus, g, lnw, lnb, ones_blk, _pick_tile(B * S, 1024))
    y_prompt = _proj_ffn(xp, yo, rw_wo, row(norm_ffn[1]), ffn_in[1], ffn_out[1], gfin, True, tm_p)

    r, k, v, kn, b, ld, g, bonus, shift_s = _rwkv_proj(xs, state_shift[0], rwp, 1, tm_s, f32)
    vecs = jnp.transpose(jnp.stack([r, k, v, kn, b, ld]), (0, 2, 1)).reshape(6, RWKV_HEADS, RWKV_HEAD, DB)
    wkv_t, y_t = _wkv_step(jnp.transpose(state_wkv[0].astype(f32), (1, 2, 3, 0)), vecs)
    wkv_s = jnp.transpose(wkv_t, (3, 0, 1, 2)).astype(state_wkv.dtype)
    y = jnp.transpose(y_t.reshape(D, DB))
    yo = _rwkv_post(y, bonus, g, lnw, lnb, ones_blk, tm_s)
    y_sample = _proj_ffn(xs, yo, rw_wo, row(norm_ffn[1]), ffn_in[1], ffn_out[1], gfin, True, tm_s)

    return (y_prompt.reshape(B, S, D), y_sample.reshape(DB, DS, D),
            ckv_p.reshape(1, B, S, KV_LORA), kpe_p.reshape(1, B, S, QK_ROPE),
            ckv_s.reshape(1, DB, DS, KV_LORA), kpe_s.reshape(1, DB, DS, QK_ROPE),
            wkv_p[None].astype(x_prompt.dtype), shift_p[None],
            wkv_s[None], shift_s[None])
```

```python
import functools
import itertools
import math

import jax
import jax.numpy as jnp
from jax import lax
from jax.experimental import pallas as pl
from jax.experimental.pallas import tpu as pltpu

f32 = jnp.float32
bf16 = jnp.bfloat16

D_MODEL = 1024
MLA_HEADS = 16
QK_NOPE = 64
QK_ROPE = 32
V_HEAD = 64
Q_LORA = 512
KV_LORA = 256
ROPE_THETA = 10000.0
MLA_SCALE = 1.0 / math.sqrt(QK_NOPE + QK_ROPE)
LOG2_E = math.log2(math.e)
RWKV_HEAD = 64
RWKV_HEADS = D_MODEL // RWKV_HEAD
FFN_HIDDEN = 2816
FFN_TILE = 256
NORM_EPS = 1e-6
GN_EPS = 64e-5
DECAY_SCALE = math.exp(-0.5)
PAGE_SIZE = 128

LANES = 128
HEAD_TILE = 128
DECODE_CHAINS = 4
WKV_CHUNK = 64
WKV_GROUP = 2
WKV_CHUNKS_PER_STEP = 8
WKV_WAVE = 2
GROUP_W = WKV_GROUP * RWKV_HEAD
VMEM_LIMIT = 56 * 1024 * 1024
_DONE = object()


def _cparams(sem):
    return pltpu.CompilerParams(dimension_semantics=sem, vmem_limit_bytes=VMEM_LIMIT)


def _rms(x, g):
    return x * lax.rsqrt(jnp.mean(x * x, axis=-1, keepdims=True) + NORM_EPS) * g


def _dot(a, b):
    return jnp.dot(a, b, preferred_element_type=f32)


def _dot_nt(a, b):
    return lax.dot_general(a, b, (((1,), (1,)), ((), ())), preferred_element_type=f32)


def _dot_tn(a, b):
    return lax.dot_general(a, b, (((0,), (0,)), ((), ())), preferred_element_type=f32)


def _mla_proj_kernel(x_ref, gmix_ref, wa_ref, qn_ref, kvn_ref, wuq_ref, wukv_ref,
                     c_ref, s1_ref, s2_ref, q_ref, k_ref, kv_ref, ckv_ref, kpe_ref):
    tm = x_ref.shape[0]
    h = _rms(x_ref[...], gmix_ref[...]).astype(bf16)
    a = _dot(h, wa_ref[...])
    cq = _rms(a[:, :Q_LORA], qn_ref[...]).astype(bf16)
    ckv = _rms(a[:, Q_LORA:Q_LORA + KV_LORA], kvn_ref[...])
    ckv_ref[...] = ckv
    cos = c_ref[...]
    sin_lo = s1_ref[...]
    sin_hi = s2_ref[...]

    def rope(t):
        return t * cos + pltpu.roll(t, LANES - QK_ROPE // 2, 1) * sin_lo + pltpu.roll(t, QK_ROPE // 2, 1) * sin_hi

    kpe_t = rope(a[:, Q_LORA + KV_LORA:])
    kpe_ref[...] = kpe_t[:, QK_NOPE:QK_NOPE + QK_ROPE]
    ckv_b = ckv.astype(bf16)
    lane = lax.broadcasted_iota(jnp.int32, (tm, HEAD_TILE), 1)
    for pr in range(MLA_HEADS // 2):
        cols = slice(pr * 2 * HEAD_TILE, (pr + 1) * 2 * HEAD_TILE)
        kv = _dot(ckv_b, wukv_ref[:, cols])
        q = _dot(cq, wuq_ref[:, cols])
        kv_ref[:, cols] = kv.astype(bf16)
        for hh in range(2):
            sl = slice(hh * HEAD_TILE, (hh + 1) * HEAD_TILE)
            out = slice((2 * pr + hh) * HEAD_TILE, (2 * pr + hh + 1) * HEAD_TILE)
            q_ref[:, out] = rope(q[:, sl]).astype(bf16)
            k_ref[:, out] = jnp.where(lane < QK_NOPE, kv[:, sl], kpe_t).astype(bf16)


def _mla_proj(x, gmix, wa, qn, kvn, wuq, wukv, tabs, tm):
    T = x.shape[0]
    tab_c, tab_s1, tab_s2 = tabs
    nt = tab_c.shape[0] // tm
    HW = MLA_HEADS * HEAD_TILE
    const = lambda i: (0, 0)
    row = lambda i: (i, 0)
    tabmap = lambda i: (i % nt, 0)
    return pl.pallas_call(
        _mla_proj_kernel,
        grid=(T // tm,),
        in_specs=[
            pl.BlockSpec((tm, D_MODEL), row),
            pl.BlockSpec((1, D_MODEL), const),
            pl.BlockSpec(wa.shape, const),
            pl.BlockSpec((1, Q_LORA), const),
            pl.BlockSpec((1, KV_LORA), const),
            pl.BlockSpec(wuq.shape, const),
            pl.BlockSpec(wukv.shape, const),
            pl.BlockSpec((tm, HEAD_TILE), tabmap),
            pl.BlockSpec((tm, HEAD_TILE), tabmap),
            pl.BlockSpec((tm, HEAD_TILE), tabmap),
        ],
        out_specs=[
            pl.BlockSpec((tm, HW), row),
            pl.BlockSpec((tm, HW), row),
            pl.BlockSpec((tm, HW), row),
            pl.BlockSpec((tm, KV_LORA), row),
            pl.BlockSpec((tm, QK_ROPE), row),
        ],
        out_shape=[
            jax.ShapeDtypeStruct((T, HW), bf16),
            jax.ShapeDtypeStruct((T, HW), bf16),
            jax.ShapeDtypeStruct((T, HW), bf16),
            jax.ShapeDtypeStruct((T, KV_LORA), f32),
            jax.ShapeDtypeStruct((T, QK_ROPE), f32),
        ],
        compiler_params=_cparams(("parallel",)),
        name="mla_proj",
    )(x, gmix, wa, qn, kvn, wuq, wukv, tab_c, tab_s1, tab_s2)


def _flash_kernel(q_ref, k_ref, kv_ref, o_ref, vt_ref, m_ref, acc_ref, *, tile):
    S = q_ref.shape[0]
    nt = S // tile
    lane = lax.broadcasted_iota(jnp.int32, (tile, HEAD_TILE), 1)
    ones_row = lax.broadcasted_iota(jnp.int32, (HEAD_TILE, tile), 0) == 0
    heads = range(q_ref.shape[1] // HEAD_TILE)
    hs = [slice(hh * HEAD_TILE, (hh + 1) * HEAD_TILE) for hh in heads]

    for hh in heads:
        for j in range(nt):
            vt = kv_ref[j * tile:(j + 1) * tile, hs[hh]].astype(f32).T
            vt_ref[hh, j] = jnp.where(ones_row, 1.0, vt).astype(bf16)

    def q_tile(qi):
        q0 = qi * tile
        qs = [q_ref[pl.ds(q0, tile), hs[hh]] for hh in heads]
        for hh in heads:
            m_ref[hh] = jnp.full((1, tile), -jnp.inf, f32)
            acc_ref[hh] = jnp.zeros((HEAD_TILE, tile), f32)

        def block(ki, kr, qr, diag):
            k0 = ki * tile + kr.start
            nk = kr.stop - kr.start
            sts = [_dot_nt(k_ref[pl.ds(k0, nk), hs[hh]], qs[hh][qr]) for hh in heads]
            if diag:
                shape = (nk, qr.stop - qr.start)
                visible = lax.broadcasted_iota(jnp.int32, shape, 0) <= lax.broadcasted_iota(jnp.int32, shape, 1)
                sts = [jnp.where(visible, st, -jnp.inf) for st in sts]
            m_prevs = [m_ref[hh, :, qr] for hh in heads]
            m_news = [jnp.maximum(m_prevs[hh], jnp.max(sts[hh], axis=0, keepdims=True)) for hh in heads]
            ps = [jnp.exp2(sts[hh] - m_news[hh]) for hh in heads]
            alphas = [jnp.exp2(m_prevs[hh] - m_news[hh]) for hh in heads]
            pvs = [_dot(vt_ref[hh, ki, :, kr], ps[hh].astype(bf16)) for hh in heads]
            for hh in heads:
                m_ref[hh, :, qr] = m_news[hh]
                acc_ref[hh, :, qr] = alphas[hh] * acc_ref[hh, :, qr] + pvs[hh]

        for ki in range(qi):
            block(ki, slice(0, tile), slice(0, tile), False)
        half = tile // 2
        block(qi, slice(0, half), slice(0, tile), True)
        block(qi, slice(half, tile), slice(half, tile), True)
        accs = [acc_ref[hh] for hh in heads]
        outs = [(a / a[0:1, :]).T for a in accs]
        for pr in range(len(outs) // 2):
            o = jnp.where(lane < V_HEAD, pltpu.roll(outs[2 * pr], V_HEAD, 1), outs[2 * pr + 1])
            o_ref[pl.ds(q0, tile), pr * 2 * V_HEAD:(pr + 1) * 2 * V_HEAD] = o.astype(bf16)

    for qi in range(nt):
        q_tile(qi)


def _flash_attention(q, k, kv, B, S, tile=512, heads_per_step=4):
    T = B * S
    tile = min(tile, S)
    nh = heads_per_step
    blk = pl.BlockSpec((S, nh * HEAD_TILE), lambda b, hp: (b, hp))
    return pl.pallas_call(
        functools.partial(_flash_kernel, tile=tile),
        grid=(B, MLA_HEADS // nh),
        in_specs=[blk, blk, blk],
        out_specs=pl.BlockSpec((S, nh * V_HEAD), lambda b, hp: (b, hp)),
        out_shape=jax.ShapeDtypeStruct((T, MLA_HEADS * V_HEAD), bf16),
        scratch_shapes=[
            pltpu.VMEM((nh, S // tile, HEAD_TILE, tile), bf16),
            pltpu.VMEM((nh, 1, tile), f32),
            pltpu.VMEM((nh, HEAD_TILE, tile), f32),
        ],
        compiler_params=_cparams(("parallel", "parallel")),
        name="mla_flash",
    )(q, k, kv)


def _absorb_q_kernel(q_ref, w_ref, o_ref):
    lane = lax.broadcasted_iota(jnp.int32, q_ref.shape, 1)
    qn = jnp.where(lane < QK_NOPE, q_ref[...], jnp.zeros_like(q_ref[...]))
    o_ref[0] = _dot_nt(qn, w_ref[...]).astype(bf16)


def _absorb_q(q, wukv):
    DB = q.shape[0]
    return pl.pallas_call(
        _absorb_q_kernel,
        grid=(MLA_HEADS,),
        in_specs=[
            pl.BlockSpec((DB, HEAD_TILE), lambda h: (0, h)),
            pl.BlockSpec((KV_LORA, HEAD_TILE), lambda h: (0, h)),
        ],
        out_specs=pl.BlockSpec((1, DB, KV_LORA), lambda h: (h, 0, 0)),
        out_shape=jax.ShapeDtypeStruct((MLA_HEADS, DB, KV_LORA), bf16),
        compiler_params=_cparams(("parallel",)),
        name="mla_absorb_q",
    )(q, wukv)


def _unabsorb_o_kernel(o_ref, w_ref, out_ref):
    out_ref[...] = _dot(o_ref[0].astype(bf16), w_ref[...])


def _unabsorb_o(o_lat, wukv):
    DB = o_lat.shape[1]
    return pl.pallas_call(
        _unabsorb_o_kernel,
        grid=(MLA_HEADS,),
        in_specs=[
            pl.BlockSpec((1, DB, KV_LORA), lambda h: (h, 0, 0)),
            pl.BlockSpec((KV_LORA, HEAD_TILE), lambda h: (0, h)),
        ],
        out_specs=pl.BlockSpec((DB, HEAD_TILE), lambda h: (0, h)),
        out_shape=jax.ShapeDtypeStruct((DB, MLA_HEADS * HEAD_TILE), f32),
        compiler_params=_cparams(("parallel",)),
        name="mla_unabsorb_o",
    )(o_lat, wukv)


def _decode_kernel(pt_ref, qlat_ref, qpe_ref, cnew_ref, pnew_ref, ckv_hbm, kpe_hbm, o_ref,
                   ckv_buf, kpe_buf, sem, m_ref, l_ref, acc_ref, *, pages_per_step, chains):
    PP = pages_per_step
    NC = chains
    DB, n_pages = pt_ref.shape
    G = n_pages // PP
    per_chain = DB // NC
    total = per_chain * G

    def page_copy(page, slot, idx, which):
        if which == 0:
            return pltpu.make_async_copy(ckv_hbm.at[page], ckv_buf.at[slot, idx], sem.at[0, slot])
        return pltpu.make_async_copy(kpe_hbm.at[page], kpe_buf.at[slot, idx], sem.at[1, slot])

    def start_group(it, slot):
        bb = it // G
        g0 = (it % G) * PP
        for c in range(NC):
            for p in range(PP):
                page = pt_ref[c * per_chain + bb, g0 + p]
                page_copy(page, slot, c * PP + p, 0).start()
                page_copy(page, slot, c * PP + p, 1).start()

    def wait_group(slot):
        for idx in range(NC * PP):
            page_copy(0, slot, idx, 0).wait()
            page_copy(0, slot, idx, 1).wait()

    it = pl.program_id(0)
    slot = it % 2

    @pl.when(it == 0)
    def _():
        start_group(0, 0)

    @pl.when(it + 1 < total)
    def _():
        start_group(it + 1, 1 - slot)

    wait_group(slot)
    _decode_group(it // G, it % G, G, per_chain, slot, qlat_ref, qpe_ref, cnew_ref, pnew_ref, o_ref,
                  ckv_buf, kpe_buf, m_ref, l_ref, acc_ref, PP, NC)


def _decode_group(bb, g, G, per_chain, slot, qlat_ref, qpe_ref, cnew_ref, pnew_ref, o_ref,
                  ckv_buf, kpe_buf, m_ref, l_ref, acc_ref, PP, NC):
    chains = range(NC)
    bs = [c * per_chain + bb for c in chains]

    @pl.when(g == 0)
    def _():
        m_ref[...] = jnp.full(m_ref.shape, -jnp.inf, f32)
        l_ref[...] = jnp.zeros(l_ref.shape, f32)
        acc_ref[...] = jnp.zeros(acc_ref.shape, f32)

    qlat = [qlat_ref[bs[c]] for c in chains]
    qpe = [qpe_ref[bs[c]] for c in chains]
    cks = [[ckv_buf[slot, c * PP + p].astype(bf16) for p in range(PP)] for c in chains]
    s = [jnp.concatenate(
        [_dot_nt(qlat[c], cks[c][p]) + _dot(qpe[c], kpe_buf[slot, c * PP + p].astype(bf16)) for p in range(PP)],
        axis=1) for c in chains]
    m_prev = [m_ref[c] for c in chains]
    m_new = [jnp.maximum(m_prev[c], jnp.max(s[c], axis=-1, keepdims=True)) for c in chains]
    p_all = [jnp.exp2(s[c] - m_new[c]) for c in chains]
    alpha = [jnp.exp2(m_prev[c] - m_new[c]) for c in chains]
    for c in chains:
        l_ref[c] = alpha[c] * l_ref[c] + jnp.sum(p_all[c], axis=-1, keepdims=True)
        m_ref[c] = m_new[c]
    for c in chains:
        pv = _dot(p_all[c][:, :PAGE_SIZE].astype(bf16), cks[c][0])
        for p in range(1, PP):
            pv = pv + _dot(p_all[c][:, p * PAGE_SIZE:(p + 1) * PAGE_SIZE].astype(bf16), cks[c][p])
        acc_ref[c] = alpha[c] * acc_ref[c] + pv

    @pl.when(g == G - 1)
    def _():
        for c in chains:
            cnew = cnew_ref[bs[c]].astype(bf16).astype(f32)
            pnew = pnew_ref[bs[c]].astype(bf16).astype(f32)
            s_self = (jnp.sum(qlat[c].astype(f32) * cnew, axis=-1, keepdims=True)
                      + jnp.sum(qpe[c].astype(f32) * pnew, axis=-1, keepdims=True))
            m_last = m_ref[c]
            m_fin = jnp.maximum(m_last, s_self)
            p_self = jnp.exp2(s_self - m_fin)
            a_fin = jnp.exp2(m_last - m_fin)
            l_fin = a_fin * l_ref[c] + p_self
            o_ref[bs[c]] = (a_fin * acc_ref[c] + p_self * cnew) / l_fin


def _decode_attention(page_table, qlat, qpe, ckv_new, kpe_new, pool_ckv, pool_kpe_t, pages_per_step=16):
    DB, n_pages = page_table.shape
    PP = math.gcd(n_pages, pages_per_step)
    NC = math.gcd(DB, DECODE_CHAINS)
    H = MLA_HEADS
    vmem = pl.BlockSpec(memory_space=pltpu.VMEM)
    hbm = pl.BlockSpec(memory_space=pl.ANY)
    grid_spec = pltpu.PrefetchScalarGridSpec(
        num_scalar_prefetch=1,
        grid=((DB // NC) * (n_pages // PP),),
        in_specs=[vmem, vmem, vmem, vmem, hbm, hbm],
        out_specs=vmem,
        scratch_shapes=[
            pltpu.VMEM((2, NC * PP, PAGE_SIZE, KV_LORA), pool_ckv.dtype),
            pltpu.VMEM((2, NC * PP, QK_ROPE, PAGE_SIZE), pool_kpe_t.dtype),
            pltpu.SemaphoreType.DMA((2, 2)),
            pltpu.VMEM((NC, H, 1), f32),
            pltpu.VMEM((NC, H, 1), f32),
            pltpu.VMEM((NC, H, KV_LORA), f32),
        ],
    )
    return pl.pallas_call(
        functools.partial(_decode_kernel, pages_per_step=PP, chains=NC),
        grid_spec=grid_spec,
        out_shape=jax.ShapeDtypeStruct((DB, H, KV_LORA), f32),
        compiler_params=_cparams(("arbitrary",)),
        name="mla_decode",
    )(page_table, qlat, qpe, ckv_new, kpe_new, pool_ckv, pool_kpe_t)


def _proj_ffn_kernel(x_ref, o_ref, wo_ref, gffn_ref, win_ref, wout_ref, gfin_ref, out_ref, *, final_norm):
    x1 = x_ref[...] + _dot(o_ref[...], wo_ref[...])
    h = _rms(x1, gffn_ref[...]).astype(bf16)
    acc = x1
    for c in range(FFN_HIDDEN // FFN_TILE):
        lo = c * FFN_TILE
        gate = _dot(h, win_ref[:, lo:lo + FFN_TILE])
        up = _dot(h, win_ref[:, FFN_HIDDEN + lo:FFN_HIDDEN + lo + FFN_TILE])
        act = (gate * jax.nn.sigmoid(gate) * up).astype(bf16)
        acc = acc + _dot(act, wout_ref[lo:lo + FFN_TILE, :])
    if final_norm:
        acc = _rms(acc, gfin_ref[...])
    out_ref[...] = acc


def _proj_ffn(x, o, wo, gffn, w_in, w_out, gfin, final_norm, tm):
    T = x.shape[0]
    row = lambda i: (i, 0)
    held = lambda arr: pl.BlockSpec(arr.shape, lambda i: (0, 0), pipeline_mode=pl.Buffered(1))
    return pl.pallas_call(
        functools.partial(_proj_ffn_kernel, final_norm=final_norm),
        grid=(T // tm,),
        in_specs=[
            pl.BlockSpec((tm, D_MODEL), row),
            pl.BlockSpec((tm, D_MODEL), row),
            held(wo), held(gffn), held(w_in), held(w_out), held(gfin),
        ],
        out_specs=pl.BlockSpec((tm, D_MODEL), row),
        out_shape=jax.ShapeDtypeStruct((T, D_MODEL), f32),
        compiler_params=_cparams(("parallel",)),
        name="proj_ffn",
    )(x, o, wo, gffn, w_in, w_out, gfin)


def _seg_sum(x, ones_blk):
    parts = [_dot(x[:, gidx * LANES:(gidx + 1) * LANES].astype(bf16), ones_blk)
             for gidx in range(x.shape[1] // LANES)]
    return jnp.concatenate(parts, axis=1)


def _rwkv_proj_kernel(x_ref, prev_ref, gmix_ref, mix_ref, wrkv_ref, w0_ref, w1_ref, w2_ref,
                      a0_ref, a1_ref, a2_ref, g1_ref, g2_ref, kk_ref, ka_ref, rk_ref, ones_ref,
                      r_out, k_out, v_out, kn_out, b_out, ld_out, g_out, bonus_out, h_out,
                      *, seq_mode, tiles_per_seq):
    tm = x_ref.shape[0]
    gm = gmix_ref[...]
    h = _rms(x_ref[...], gm)
    if seq_mode:
        hp_row = _rms(prev_ref[...], gm)[7:8, :]
        is_start = (pl.program_id(0) % tiles_per_seq) == 0
        hp_row = jnp.where(is_start, jnp.zeros_like(hp_row), hp_row)
        rowid = lax.broadcasted_iota(jnp.int32, (tm, 1), 0)
        hprev = jnp.where(rowid == 0, hp_row, pltpu.roll(h, 1, 0))
        h_out[0] = h[tm - 8:, :]
    else:
        hprev = prev_ref[...]
        h_out[...] = h
    xx = hprev - h
    mix = mix_ref[...]
    xr, xw, xk, xv, xa, xg = ((h + xx * mix[n:n + 1]).astype(bf16) for n in range(6))
    w_hid = jnp.tanh(_dot(xw, w1_ref[...])).astype(bf16)
    a_hid = _dot(xa, a1_ref[...]).astype(bf16)
    g_hid = jax.nn.sigmoid(_dot(xg, g1_ref[...])).astype(bf16)
    ones_blk = ones_ref[...]
    ct = 2 * LANES
    for j in range(D_MODEL // ct):
        cols = slice(j * ct, (j + 1) * ct)
        r = _dot(xr, wrkv_ref[0, :, cols])
        k = _dot(xk, wrkv_ref[1, :, cols])
        v = _dot(xv, wrkv_ref[2, :, cols])
        wl = w0_ref[:, cols] + _dot(w_hid, w2_ref[:, cols])
        ld_out[:, cols] = -DECAY_SCALE * jax.nn.sigmoid(wl)
        a = jax.nn.sigmoid(a0_ref[:, cols] + _dot(a_hid, a2_ref[:, cols]))
        g_out[:, cols] = _dot(g_hid, g2_ref[:, cols]).astype(g_out.dtype)
        kk = k * kk_ref[:, cols]
        kk = kk * jnp.minimum(lax.rsqrt(_seg_sum(kk * kk, ones_blk)), 1e12)
        k = k * (1.0 + (a - 1.0) * ka_ref[:, cols])
        bonus_out[:, cols] = (_seg_sum(r * k * rk_ref[:, cols], ones_blk) * v).astype(bonus_out.dtype)
        r_out[:, cols] = r.astype(r_out.dtype)
        k_out[:, cols] = k.astype(k_out.dtype)
        v_out[:, cols] = v.astype(v_out.dtype)
        kn_out[:, cols] = kk.astype(kn_out.dtype)
        b_out[:, cols] = (kk * a).astype(b_out.dtype)


def _rwkv_proj(x, prev, p, seq_len, tm, vec_dtype):
    T = x.shape[0]
    seq_mode = seq_len > 1
    tiles_per_seq = max(seq_len // tm, 1)
    row = lambda i: (i, 0)
    const = lambda i: (0, 0)
    const3 = lambda i: (0, 0, 0)
    if seq_mode:
        prev_spec = pl.BlockSpec((8, D_MODEL), lambda i: (jnp.maximum(i * (tm // 8) - 1, 0), 0))
        h_spec = pl.BlockSpec((1, 8, D_MODEL), lambda i: (i, 0, 0))
        h_shape = jax.ShapeDtypeStruct((T // tm, 8, D_MODEL), f32)
    else:
        prev_spec = pl.BlockSpec((tm, D_MODEL), row)
        h_spec = pl.BlockSpec((tm, D_MODEL), row)
        h_shape = jax.ShapeDtypeStruct((T, D_MODEL), f32)
    vec = lambda dt: jax.ShapeDtypeStruct((T, D_MODEL), dt)
    tile = pl.BlockSpec((tm, D_MODEL), row)
    full = lambda arr: pl.BlockSpec(arr.shape, const3 if arr.ndim == 3 else const)
    weights = [p["gmix"], p["mix"], p["w_rkv"], p["w0"], p["w1"], p["w2"], p["a0"], p["a1"], p["a2"],
               p["g1"], p["g2"], p["k_k"], p["k_a"], p["r_k"], p["ones_blk"]]
    return pl.pallas_call(
        functools.partial(_rwkv_proj_kernel, seq_mode=seq_mode, tiles_per_seq=tiles_per_seq),
        grid=(T // tm,),
        in_specs=[tile, prev_spec] + [full(w) for w in weights],
        out_specs=[tile] * 8 + [h_spec],
        out_shape=[vec(vec_dtype)] * 5 + [vec(f32), vec(bf16), vec(bf16), h_shape],
        compiler_params=_cparams(("parallel",)),
        name="rwkv_proj",
    )(x, prev, *weights)


def _wkv_chunk_kernel(r_ref, k_ref, v_ref, kn_ref, b_ref, ld_ref, y_ref, st_ref, m_ref, *, chunks, wave):
    C = WKV_CHUNK
    G = WKV_GROUP
    W = GROUP_W
    c_idx = pl.program_id(1)

    @pl.when(c_idx == 0)
    def _():
        m_ref[...] = jnp.zeros(m_ref.shape, f32)

    rr = lax.broadcasted_iota(jnp.int32, (W, W), 0)
    cc = lax.broadcasted_iota(jnp.int32, (W, W), 1)
    bd_mask = (rr // RWKV_HEAD) == (cc // RWKV_HEAD)
    t_idx = lax.broadcasted_iota(jnp.int32, (C, W), 0)
    s_idx = lax.broadcasted_iota(jnp.int32, (C, W), 1) % C
    strict = s_idx < t_idx
    incl = s_idx <= t_idx
    eye_cat = (s_idx == t_idx).astype(f32)
    lane_lo = lax.broadcasted_iota(jnp.int32, (C, LANES), 1) < RWKV_HEAD
    zero_tile = jnp.zeros((C, LANES), bf16)

    def bd(y):
        yb = y.astype(bf16)
        rows = []
        for hh in range(G):
            lt = hh // 2
            t = yb[:, lt * LANES:(lt + 1) * LANES]
            keep = jnp.where(lane_lo, t, zero_tile) if hh % 2 == 0 else jnp.where(lane_lo, zero_tile, t)
            tiles = [zero_tile] * (W // LANES)
            tiles[lt] = keep
            rows.append(jnp.concatenate(tiles, axis=1))
        return jnp.concatenate(rows, axis=0)

    def bdmm(x, y):
        return _dot(x.astype(bf16), bd(y))

    def cumsum_rows(x):
        sh = 1
        while sh < C:
            x = x + jnp.where(t_idx >= sh, pltpu.roll(x, sh, 0), 0.0)
            sh *= 2
        return x

    NG = r_ref.shape[1] // W
    n_waves = chunks // wave

    def precompute(wv, out):
        streams = [(wv * wave + ci, gi) for ci in range(wave) for gi in range(NG)]
        ns = range(len(streams))

        def load(ref, s):
            ci, gi = streams[s]
            return ref[ci * C:(ci + 1) * C, gi * W:(gi + 1) * W]

        ld = [load(ld_ref, s) for s in ns]
        cs = [cumsum_rows(ld[s]) for s in ns]
        tot_col = [jnp.broadcast_to(cs[s][C - 1:C, :], (LANES, W)).T for s in ns]
        yield
        r = [load(r_ref, s).astype(f32) for s in ns]
        k = [load(k_ref, s).astype(f32) for s in ns]
        v = [load(v_ref, s).astype(f32) for s in ns]
        kn = [load(kn_ref, s).astype(f32) for s in ns]
        b = [load(b_ref, s).astype(f32) for s in ns]
        p_inv = [jnp.exp(-cs[s]) for s in ns]
        p_rest = [jnp.exp(cs[s][C - 1:C, :] - cs[s]) for s in ns]
        a_t = [-kn[s] * jnp.exp(cs[s] - ld[s]) for s in ns]
        r_t = [r[s] * jnp.exp(cs[s]) for s in ns]
        lhs = [jnp.concatenate([a_t[s], r_t[s]], axis=0).astype(bf16) for s in ns]
        yield
        s_b = [_dot_nt(lhs[s], bd(b[s] * p_inv[s])) for s in ns]
        yield
        s_k = [_dot_nt(lhs[s], bd(k[s] * p_inv[s])) for s in ns]
        yield
        l_ab = [jnp.where(strict, s_b[s][:C], 0.0) for s in ns]
        l_ak = [jnp.where(strict, s_k[s][:C], 0.0) for s in ns]
        a_rb = [jnp.where(incl, s_b[s][C:], 0.0) for s in ns]
        a_rk = [jnp.where(incl, s_k[s][C:], 0.0) for s in ns]
        t_inv = [eye_cat + l_ab[s] for s in ns]
        pw = [bdmm(l_ab[s], l_ab[s]) for s in ns]
        yield
        step = 2
        while step < C:
            rhs = [bd(pw[s]) for s in ns]
            if 2 * step < C:
                both = [_dot(jnp.concatenate([pw[s], t_inv[s]], axis=0).astype(bf16), rhs[s]) for s in ns]
                pw = [both[s][:C] for s in ns]
                t_inv = [t_inv[s] + both[s][C:] for s in ns]
            else:
                t_inv = [t_inv[s] + _dot(t_inv[s].astype(bf16), rhs[s]) for s in ns]
            step *= 2
            yield
        from_v = [bdmm(jnp.concatenate([l_ak[s], a_rk[s]], axis=0), v[s]) for s in ns]
        yield
        w1 = [bdmm(t_inv[s], a_t[s]) for s in ns]
        yield
        w2 = [bdmm(t_inv[s], from_v[s][:C]) for s in ns]
        out["st_lhs"] = [jnp.concatenate([w1[s], r_t[s]], axis=0).astype(bf16) for s in ns]
        out["w2"] = w2
        out["y_v"] = [from_v[s][C:] for s in ns]
        out["a_rb"] = a_rb
        out["v"] = v
        out["bk_rest"] = [jnp.concatenate([b[s] * p_rest[s], k[s] * p_rest[s]], axis=0).astype(bf16) for s in ns]
        out["decay_col"] = [jnp.exp(jnp.concatenate([tot_col[s]] * (W // LANES), axis=1)) for s in ns]
        yield

    def state_chain(wv, d, m):
        for ci in range(wave):
            c_glob = wv * wave + ci
            ss = [ci * NG + gi for gi in range(NG)]
            from_state = [_dot(d["st_lhs"][s], m[gi].astype(bf16)) for gi, s in enumerate(ss)]
            yield
            u = [from_state[gi][:C] + d["w2"][s] for gi, s in enumerate(ss)]
            y_u = [bdmm(d["a_rb"][s], u[gi]) for gi, s in enumerate(ss)]
            upd = [_dot_tn(d["bk_rest"][s], jnp.concatenate([u[gi], d["v"][s]], axis=0).astype(bf16))
                   for gi, s in enumerate(ss)]
            yield
            for gi, s in enumerate(ss):
                y_ref[c_glob * C:(c_glob + 1) * C, gi * W:(gi + 1) * W] = from_state[gi][C:] + y_u[gi] + d["y_v"][s]
                m[gi] = m[gi] * d["decay_col"][s] + jnp.where(bd_mask, upd[gi], 0.0)
            yield

    front_stages = 5
    m = [m_ref[gi] for gi in range(NG)]
    data = [{} for _ in range(n_waves)]
    pres = [precompute(wv, data[wv]) for wv in range(n_waves)]
    for t in range(n_waves + 2):
        lanes = []
        if 0 <= t - 1 < n_waves:
            lanes.append(pres[t - 1])
        if t < n_waves:
            lanes.append(itertools.islice(pres[t], front_stages))
        if 0 <= t - 2 < n_waves:
            lanes.append(state_chain(t - 2, data[t - 2], m))
        while lanes:
            lanes = [g for g in lanes if next(g, _DONE) is not _DONE]
    for gi in range(NG):
        m_ref[gi] = m[gi]
        acc = m[gi][:RWKV_HEAD]
        for hh in range(1, G):
            acc = acc + m[gi][hh * RWKV_HEAD:(hh + 1) * RWKV_HEAD]
        st_ref[0, :, gi * W:(gi + 1) * W] = acc


def _wkv_chunked(r, k, v, kn, b, ld, B, S):
    T = B * S
    chunks = WKV_CHUNKS_PER_STEP if S % (WKV_CHUNK * WKV_CHUNKS_PER_STEP) == 0 else 1
    wave = WKV_WAVE if chunks % WKV_WAVE == 0 else chunks
    rows = WKV_CHUNK * chunks
    nc = S // rows
    tile = pl.BlockSpec((rows, D_MODEL), lambda bi, ci: (bi * nc + ci, 0))
    return pl.pallas_call(
        functools.partial(_wkv_chunk_kernel, chunks=chunks, wave=wave),
        grid=(B, nc),
        in_specs=[tile] * 6,
        out_specs=[tile, pl.BlockSpec((1, RWKV_HEAD, D_MODEL), lambda bi, ci: (bi, 0, 0))],
        out_shape=[jax.ShapeDtypeStruct((T, D_MODEL), f32),
                   jax.ShapeDtypeStruct((B, RWKV_HEAD, D_MODEL), f32)],
        scratch_shapes=[pltpu.VMEM((D_MODEL // GROUP_W, GROUP_W, GROUP_W), f32)],
        compiler_params=_cparams(("parallel", "arbitrary")),
        name="wkv_chunked",
    )(r, k, v, kn, b, ld)


def _wkv_step_kernel(s_ref, vec_ref, snew_ref, y_ref):
    r, k, kn, b, ld = (vec_ref[i, 0] for i in (0, 1, 3, 4, 5))
    a = -kn
    w = jnp.exp(ld)

    def value_row(vi, c):
        st = s_ref[0, vi]
        sa = jnp.sum(st * a, axis=0, keepdims=True)
        v_row = vec_ref[2, 0, pl.ds(vi, 1), :]
        st = st * w + sa * b + v_row * k
        snew_ref[0, vi] = st
        y_ref[0, pl.ds(vi, 1), :] = jnp.sum(st * r, axis=0, keepdims=True)
        return c

    lax.fori_loop(0, RWKV_HEAD, value_row, 0, unroll=4)


def _wkv_step(state_t, vecs):
    H, N, _, DB = state_t.shape
    st = pl.BlockSpec((1, N, N, DB), lambda h: (h, 0, 0, 0))
    return pl.pallas_call(
        _wkv_step_kernel,
        grid=(H,),
        in_specs=[st, pl.BlockSpec((6, 1, N, DB), lambda h: (0, h, 0, 0))],
        out_specs=[st, pl.BlockSpec((1, N, DB), lambda h: (h, 0, 0))],
        out_shape=[jax.ShapeDtypeStruct(state_t.shape, f32),
                   jax.ShapeDtypeStruct((H, N, DB), f32)],
        compiler_params=_cparams(("parallel",)),
        name="wkv_step",
    )(state_t, vecs)


def _rwkv_post_kernel(y_ref, bonus_ref, g_ref, lnw_ref, lnb_ref, ones_ref, o_ref):
    ones_blk = ones_ref[...]
    y = y_ref[...]
    inv_n = 1.0 / RWKV_HEAD
    mu = _seg_sum(y, ones_blk) * inv_n
    d = y - mu
    var = _seg_sum(d * d, ones_blk) * inv_n
    yn = d * lax.rsqrt(var + GN_EPS) * lnw_ref[...] + lnb_ref[...]
    o_ref[...] = ((yn + bonus_ref[...].astype(f32)) * g_ref[...].astype(f32)).astype(bf16)


def _rwkv_post(y, bonus, g, lnw, lnb, ones_blk, tm):
    T = y.shape[0]
    row = lambda i: (i, 0)
    const = lambda i: (0, 0)
    tile = pl.BlockSpec((tm, D_MODEL), row)
    return pl.pallas_call(
        _rwkv_post_kernel,
        grid=(T // tm,),
        in_specs=[tile, tile, tile, pl.BlockSpec((1, D_MODEL), const), pl.BlockSpec((1, D_MODEL), const),
                  pl.BlockSpec((LANES, LANES), const)],
        out_specs=tile,
        out_shape=jax.ShapeDtypeStruct((T, D_MODEL), bf16),
        compiler_params=_cparams(("parallel",)),
        name="rwkv_post",
    )(y, bonus, g, lnw, lnb, ones_blk)


def _rope_tables(pos):
    half = QK_ROPE // 2
    inv = 1.0 / (ROPE_THETA ** (jnp.arange(0, QK_ROPE, 2, dtype=f32) / QK_ROPE))
    ang = pos[:, None] * inv[None, :]
    cos, sin = jnp.cos(ang), jnp.sin(ang)
    n = pos.shape[0]
    ones = jnp.ones((n, QK_NOPE), f32)
    z = lambda w: jnp.zeros((n, w), f32)
    tab_c = jnp.concatenate([ones, cos, cos, z(HEAD_TILE - QK_NOPE - QK_ROPE)], axis=1)
    tab_s1 = jnp.concatenate([z(QK_NOPE), -sin, z(HEAD_TILE - QK_NOPE - half)], axis=1)
    tab_s2 = jnp.concatenate([z(QK_NOPE + half), sin, z(HEAD_TILE - QK_NOPE - QK_ROPE)], axis=1)
    return tab_c, tab_s1, tab_s2


def _pick_tile(n, pref):
    t = min(n, pref)
    while n % t:
        t //= 2
    return t


def kernel(x_prompt, x_sample, cache_ckv, cache_kpe, state_wkv, state_shift, page_table,
           norm_mix, norm_ffn, norm_final,
           mla_w_a, mla_q_norm, mla_kv_norm, mla_w_uq, mla_w_ukv, mla_w_o,
           rw_mix, rw_w_rkv, rw_w0, rw_w1, rw_w2, rw_a0, rw_a1, rw_a2, rw_g1, rw_g2,
           rw_k_k, rw_k_a, rw_r_k, rw_ln_w, rw_ln_b, rw_w_o,
           ffn_w_in, ffn_w_out):
    B, S, D = x_prompt.shape
    DB, DS, _ = x_sample.shape
    assert D == D_MODEL and DS == 1 and S % WKV_CHUNK == 0
    n_pages = page_table.shape[1]
    past_len = n_pages * PAGE_SIZE
    H = MLA_HEADS
    row = lambda t: t.reshape(1, -1)
    xp = x_prompt.reshape(B * S, D)
    xs = x_sample.reshape(DB, D)

    w_a = mla_w_a[0]
    pad_pe = jnp.zeros((D, HEAD_TILE), f32).at[:, QK_NOPE:QK_NOPE + QK_ROPE].set(w_a[:, Q_LORA + KV_LORA:])
    wa_ext = jnp.concatenate([w_a[:, :Q_LORA + KV_LORA], pad_pe], axis=1).astype(bf16)
    wuq = (mla_w_uq[0] * (MLA_SCALE * LOG2_E)).reshape(Q_LORA, H, QK_NOPE + QK_ROPE)
    wuq = jnp.pad(wuq, ((0, 0), (0, 0), (0, HEAD_TILE - QK_NOPE - QK_ROPE))).reshape(Q_LORA, H * HEAD_TILE).astype(bf16)
    wukv = mla_w_ukv[0].reshape(KV_LORA, H * HEAD_TILE).astype(bf16)
    mla_wo = mla_w_o[0].astype(bf16)
    qn, kvn = row(mla_q_norm[0]), row(mla_kv_norm[0])
    ffn_in = [ffn_w_in[i].astype(bf16) for i in range(ffn_w_in.shape[0])]
    ffn_out = [ffn_w_out[i].astype(bf16) for i in range(ffn_w_out.shape[0])]
    hh = jnp.arange(LANES) // RWKV_HEAD
    ones_blk = (hh[:, None] == hh[None, :]).astype(bf16)
    rwp = dict(
        gmix=row(norm_mix[1]), mix=jnp.pad(rw_mix[0], ((0, 2), (0, 0))), w_rkv=rw_w_rkv[0].astype(bf16),
        w0=row(rw_w0[0]), w1=rw_w1[0].astype(bf16), w2=rw_w2[0].astype(bf16),
        a0=row(rw_a0[0]), a1=rw_a1[0].astype(bf16), a2=rw_a2[0].astype(bf16),
        g1=rw_g1[0].astype(bf16), g2=rw_g2[0].astype(bf16),
        k_k=row(rw_k_k[0]), k_a=row(rw_k_a[0]), r_k=row(rw_r_k[0]), ones_blk=ones_blk)
    rw_wo = rw_w_o[0].astype(bf16)
    lnw, lnb = row(rw_ln_w[0]), row(rw_ln_b[0])
    gfin = row(norm_final)

    tm_p = _pick_tile(B * S, 512)
    tm_s = _pick_tile(DB, 128)

    tabs_p = _rope_tables(jnp.arange(S, dtype=f32))
    tabs_s = _rope_tables(jnp.full((tm_s,), past_len, f32))
    q_s, _, _, ckv_s, kpe_s = _mla_proj(xs, row(norm_mix[0]), wa_ext, qn, kvn, wuq, wukv, tabs_s, tm_s)
    qlat = jnp.swapaxes(_absorb_q(q_s, wukv), 0, 1)
    qpe = q_s.reshape(DB, H, HEAD_TILE)[:, :, QK_NOPE:QK_NOPE + QK_ROPE]
    o_lat = _decode_attention(page_table, qlat, qpe, ckv_s.reshape(DB, 1, KV_LORA), kpe_s.reshape(DB, 1, QK_ROPE),
                              cache_ckv.reshape(-1, PAGE_SIZE, KV_LORA),
                              jnp.swapaxes(cache_kpe.reshape(-1, PAGE_SIZE, QK_ROPE), 1, 2))
    o_s = _unabsorb_o(jnp.swapaxes(o_lat, 0, 1), wukv)
    o_s = o_s.reshape(DB, H, HEAD_TILE)[:, :, QK_NOPE:].reshape(DB, H * V_HEAD).astype(bf16)
    xs = _proj_ffn(xs, o_s, mla_wo, row(norm_ffn[0]), ffn_in[0], ffn_out[0], gfin, False, tm_s)

    q_p, k_p, kv_p, ckv_p, kpe_p = _mla_proj(xp, row(norm_mix[0]), wa_ext, qn, kvn, wuq, wukv, tabs_p,
                                             _pick_tile(S, 512))
    o_p = _flash_attention(q_p, k_p, kv_p, B, S)
    xp = _proj_ffn(xp, o_p, mla_wo, row(norm_ffn[0]), ffn_in[0], ffn_out[0], gfin, False, tm_p)

    tm_r = _pick_tile(S, 512)
    r, k, v, kn, b, ld, g, bonus, hl = _rwkv_proj(xp, xp, rwp, S, tm_r, bf16)
    shift_p = hl.reshape(B, S // tm_r, 8, D)[:, -1, 7, :]
    y, st = _wkv_chunked(r, k, v, kn, b, ld, B, S)
    wkv_p = jnp.transpose(st.reshape(B, RWKV_HEAD, RWKV_HEADS, RWKV_HEAD), (0, 2, 3, 1))
    yo = _rwkv_post(y, bonus, g, lnw, lnb, ones_blk, _pick_tile(B * S, 1024))
    y_prompt = _proj_ffn(xp, yo, rw_wo, row(norm_ffn[1]), ffn_in[1], ffn_out[1], gfin, True, tm_p)

    r, k, v, kn, b, ld, g, bonus, shift_s = _rwkv_proj(xs, state_shift[0], rwp, 1, tm_s, f32)
    vecs = jnp.transpose(jnp.stack([r, k, v, kn, b, ld]), (0, 2, 1)).reshape(6, RWKV_HEADS, RWKV_HEAD, DB)
    wkv_t, y_t = _wkv_step(jnp.transpose(state_wkv[0].astype(f32), (1, 2, 3, 0)), vecs)
    wkv_s = jnp.transpose(wkv_t, (3, 0, 1, 2)).astype(state_wkv.dtype)
    y = jnp.transpose(y_t.reshape(D, DB))
    yo = _rwkv_post(y, bonus, g, lnw, lnb, ones_blk, tm_s)
    y_sample = _proj_ffn(xs, yo, rw_wo, row(norm_ffn[1]), ffn_in[1], ffn_out[1], gfin, True, tm_s)

    return (y_prompt.reshape(B, S, D), y_sample.reshape(DB, DS, D),
            ckv_p.reshape(1, B, S, KV_LORA), kpe_p.reshape(1, B, S, QK_ROPE),
            ckv_s.reshape(1, DB, DS, KV_LORA), kpe_s.reshape(1, DB, DS, QK_ROPE),
            wkv_p[None].astype(x_prompt.dtype), shift_p[None],
            wkv_s[None], shift_s[None])
```

```python
import functools
import itertools
import math

import jax
import jax.numpy as jnp
from jax import lax
from jax.experimental import pallas as pl
from jax.experimental.pallas import tpu as pltpu

f32 = jnp.float32
bf16 = jnp.bfloat16

D_MODEL = 1024
MLA_HEADS = 16
QK_NOPE = 64
QK_ROPE = 32
V_HEAD = 64
Q_LORA = 512
KV_LORA = 256
ROPE_THETA = 10000.0
MLA_SCALE = 1.0 / math.sqrt(QK_NOPE + QK_ROPE)
LOG2_E = math.log2(math.e)
RWKV_HEAD = 64
RWKV_HEADS = D_MODEL // RWKV_HEAD
FFN_HIDDEN = 2816
FFN_TILE = 256
NORM_EPS = 1e-6
GN_EPS = 64e-5
DECAY_SCALE = math.exp(-0.5)
PAGE_SIZE = 128

LANES = 128
HEAD_TILE = 128
DECODE_CHAINS = 4
WKV_CHUNK = 64
WKV_GROUP = 2
WKV_CHUNKS_PER_STEP = 8
WKV_WAVE = 2
GROUP_W = WKV_GROUP * RWKV_HEAD
VMEM_LIMIT = 56 * 1024 * 1024
_DONE = object()


def _cparams(sem):
    return pltpu.CompilerParams(dimension_semantics=sem, vmem_limit_bytes=VMEM_LIMIT)


def _rms(x, g):
    return x * lax.rsqrt(jnp.mean(x * x, axis=-1, keepdims=True) + NORM_EPS) * g


def _dot(a, b):
    return jnp.dot(a, b, preferred_element_type=f32)


def _dot_nt(a, b):
    return lax.dot_general(a, b, (((1,), (1,)), ((), ())), preferred_element_type=f32)


def _dot_tn(a, b):
    return lax.dot_general(a, b, (((0,), (0,)), ((), ())), preferred_element_type=f32)


def _mla_proj_kernel(x_ref, gmix_ref, wa_ref, qn_ref, kvn_ref, wuq_ref, wukv_ref,
                     c_ref, s1_ref, s2_ref, q_ref, k_ref, kv_ref, ckv_ref, kpe_ref):
    tm = x_ref.shape[0]
    h = _rms(x_ref[...], gmix_ref[...]).astype(bf16)
    a = _dot(h, wa_ref[...])
    cq = _rms(a[:, :Q_LORA], qn_ref[...]).astype(bf16)
    ckv = _rms(a[:, Q_LORA:Q_LORA + KV_LORA], kvn_ref[...])
    ckv_ref[...] = ckv
    cos = c_ref[...]
    sin_lo = s1_ref[...]
    sin_hi = s2_ref[...]

    def rope(t):
        return t * cos + pltpu.roll(t, LANES - QK_ROPE // 2, 1) * sin_lo + pltpu.roll(t, QK_ROPE // 2, 1) * sin_hi

    kpe_t = rope(a[:, Q_LORA + KV_LORA:])
    kpe_ref[...] = kpe_t[:, QK_NOPE:QK_NOPE + QK_ROPE]
    ckv_b = ckv.astype(bf16)
    lane = lax.broadcasted_iota(jnp.int32, (tm, HEAD_TILE), 1)
    for pr in range(MLA_HEADS // 2):
        cols = slice(pr * 2 * HEAD_TILE, (pr + 1) * 2 * HEAD_TILE)
        kv = _dot(ckv_b, wukv_ref[:, cols])
        q = _dot(cq, wuq_ref[:, cols])
        kv_ref[:, cols] = kv.astype(bf16)
        for hh in range(2):
            sl = slice(hh * HEAD_TILE, (hh + 1) * HEAD_TILE)
            out = slice((2 * pr + hh) * HEAD_TILE, (2 * pr + hh + 1) * HEAD_TILE)
            q_ref[:, out] = rope(q[:, sl]).astype(bf16)
            k_ref[:, out] = jnp.where(lane < QK_NOPE, kv[:, sl], kpe_t).astype(bf16)


def _mla_proj(x, gmix, wa, qn, kvn, wuq, wukv, tabs, tm):
    T = x.shape[0]
    tab_c, tab_s1, tab_s2 = tabs
    nt = tab_c.shape[0] // tm
    HW = MLA_HEADS * HEAD_TILE
    const = lambda i: (0, 0)
    row = lambda i: (i, 0)
    tabmap = lambda i: (i % nt, 0)
    return pl.pallas_call(
        _mla_proj_kernel,
        grid=(T // tm,),
        in_specs=[
            pl.BlockSpec((tm, D_MODEL), row),
            pl.BlockSpec((1, D_MODEL), const),
            pl.BlockSpec(wa.shape, const),
            pl.BlockSpec((1, Q_LORA), const),
            pl.BlockSpec((1, KV_LORA), const),
            pl.BlockSpec(wuq.shape, const),
            pl.BlockSpec(wukv.shape, const),
            pl.BlockSpec((tm, HEAD_TILE), tabmap),
            pl.BlockSpec((tm, HEAD_TILE), tabmap),
            pl.BlockSpec((tm, HEAD_TILE), tabmap),
        ],
        out_specs=[
            pl.BlockSpec((tm, HW), row),
            pl.BlockSpec((tm, HW), row),
            pl.BlockSpec((tm, HW), row),
            pl.BlockSpec((tm, KV_LORA), row),
            pl.BlockSpec((tm, QK_ROPE), row),
        ],
        out_shape=[
            jax.ShapeDtypeStruct((T, HW), bf16),
            jax.ShapeDtypeStruct((T, HW), bf16),
            jax.ShapeDtypeStruct((T, HW), bf16),
            jax.ShapeDtypeStruct((T, KV_LORA), f32),
            jax.ShapeDtypeStruct((T, QK_ROPE), f32),
        ],
        compiler_params=_cparams(("parallel",)),
        name="mla_proj",
    )(x, gmix, wa, qn, kvn, wuq, wukv, tab_c, tab_s1, tab_s2)


def _flash_kernel(q_ref, k_ref, kv_ref, o_ref, vt_ref, m_ref, acc_ref, *, tile):
    S = q_ref.shape[0]
    nt = S // tile
    lane = lax.broadcasted_iota(jnp.int32, (tile, HEAD_TILE), 1)
    ones_row = lax.broadcasted_iota(jnp.int32, (HEAD_TILE, tile), 0) == 0
    heads = range(q_ref.shape[1] // HEAD_TILE)
    hs = [slice(hh * HEAD_TILE, (hh + 1) * HEAD_TILE) for hh in heads]

    for hh in heads:
        for j in range(nt):
            vt = kv_ref[j * tile:(j + 1) * tile, hs[hh]].astype(f32).T
            vt_ref[hh, j] = jnp.where(ones_row, 1.0, vt).astype(bf16)

    def q_tile(qi):
        q0 = qi * tile
        qs = [q_ref[pl.ds(q0, tile), hs[hh]] for hh in heads]
        for hh in heads:
            m_ref[hh] = jnp.full((1, tile), -jnp.inf, f32)
            acc_ref[hh] = jnp.zeros((HEAD_TILE, tile), f32)

        def block(ki, kr, qr, diag):
            k0 = ki * tile + kr.start
            nk = kr.stop - kr.start
            sts = [_dot_nt(k_ref[pl.ds(k0, nk), hs[hh]], qs[hh][qr]) for hh in heads]
            if diag:
                shape = (nk, qr.stop - qr.start)
                visible = lax.broadcasted_iota(jnp.int32, shape, 0) <= lax.broadcasted_iota(jnp.int32, shape, 1)
                sts = [jnp.where(visible, st, -jnp.inf) for st in sts]
            m_prevs = [m_ref[hh, :, qr] for hh in heads]
            m_news = [jnp.maximum(m_prevs[hh], jnp.max(sts[hh], axis=0, keepdims=True)) for hh in heads]
            ps = [jnp.exp2(sts[hh] - m_news[hh]) for hh in heads]
            alphas = [jnp.exp2(m_prevs[hh] - m_news[hh]) for hh in heads]
            pvs = [_dot(vt_ref[hh, ki, :, kr], ps[hh].astype(bf16)) for hh in heads]
            for hh in heads:
                m_ref[hh, :, qr] = m_news[hh]
                acc_ref[hh, :, qr] = alphas[hh] * acc_ref[hh, :, qr] + pvs[hh]

        for ki in range(qi):
            block(ki, slice(0, tile), slice(0, tile), False)
        half = tile // 2
        block(qi, slice(0, half), slice(0, tile), True)
        block(qi, slice(half, tile), slice(half, tile), True)
        accs = [acc_ref[hh] for hh in heads]
        outs = [(a / a[0:1, :]).T for a in accs]
        for pr in range(len(outs) // 2):
            o = jnp.where(lane < V_HEAD, pltpu.roll(outs[2 * pr], V_HEAD, 1), outs[2 * pr + 1])
            o_ref[pl.ds(q0, tile), pr * 2 * V_HEAD:(pr + 1) * 2 * V_HEAD] = o.astype(bf16)

    for qi in range(nt):
        q_tile(qi)


def _flash_attention(q, k, kv, B, S, tile=512, heads_per_step=4):
    T = B * S
    tile = min(tile, S)
    nh = heads_per_step
    blk = pl.BlockSpec((S, nh * HEAD_TILE), lambda b, hp: (b, hp))
    return pl.pallas_call(
        functools.partial(_flash_kernel, tile=tile),
        grid=(B, MLA_HEADS // nh),
        in_specs=[blk, blk, blk],
        out_specs=pl.BlockSpec((S, nh * V_HEAD), lambda b, hp: (b, hp)),
        out_shape=jax.ShapeDtypeStruct((T, MLA_HEADS * V_HEAD), bf16),
        scratch_shapes=[
            pltpu.VMEM((nh, S // tile, HEAD_TILE, tile), bf16),
            pltpu.VMEM((nh, 1, tile), f32),
            pltpu.VMEM((nh, HEAD_TILE, tile), f32),
        ],
        compiler_params=_cparams(("parallel", "parallel")),
        name="mla_flash",
    )(q, k, kv)


def _absorb_q_kernel(q_ref, w_ref, o_ref):
    lane = lax.broadcasted_iota(jnp.int32, q_ref.shape, 1)
    qn = jnp.where(lane < QK_NOPE, q_ref[...], jnp.zeros_like(q_ref[...]))
    o_ref[0] = _dot_nt(qn, w_ref[...]).astype(bf16)


def _absorb_q(q, wukv):
    DB = q.shape[0]
    return pl.pallas_call(
        _absorb_q_kernel,
        grid=(MLA_HEADS,),
        in_specs=[
            pl.BlockSpec((DB, HEAD_TILE), lambda h: (0, h)),
            pl.BlockSpec((KV_LORA, HEAD_TILE), lambda h: (0, h)),
        ],
        out_specs=pl.BlockSpec((1, DB, KV_LORA), lambda h: (h, 0, 0)),
        out_shape=jax.ShapeDtypeStruct((MLA_HEADS, DB, KV_LORA), bf16),
        compiler_params=_cparams(("parallel",)),
        name="mla_absorb_q",
    )(q, wukv)


def _unabsorb_o_kernel(o_ref, w_ref, out_ref):
    out_ref[...] = _dot(o_ref[0].astype(bf16), w_ref[...])


def _unabsorb_o(o_lat, wukv):
    DB = o_lat.shape[1]
    return pl.pallas_call(
        _unabsorb_o_kernel,
        grid=(MLA_HEADS,),
        in_specs=[
            pl.BlockSpec((1, DB, KV_LORA), lambda h: (h, 0, 0)),
            pl.BlockSpec((KV_LORA, HEAD_TILE), lambda h: (0, h)),
        ],
        out_specs=pl.BlockSpec((DB, HEAD_TILE), lambda h: (0, h)),
        out_shape=jax.ShapeDtypeStruct((DB, MLA_HEADS * HEAD_TILE), f32),
        compiler_params=_cparams(("parallel",)),
        name="mla_unabsorb_o",
    )(o_lat, wukv)


def _decode_kernel(pt_ref, qlat_ref, qpe_ref, cnew_ref, pnew_ref, ckv_hbm, kpe_hbm, o_ref,
                   ckv_buf, kpe_buf, sem, m_ref, l_ref, acc_ref, *, pages_per_step, chains):
    PP = pages_per_step
    NC = chains
    DB, n_pages = pt_ref.shape
    G = n_pages // PP
    per_chain = DB // NC
    total = per_chain * G

    def page_copy(page, slot, idx, which):
        if which == 0:
            return pltpu.make_async_copy(ckv_hbm.at[page], ckv_buf.at[slot, idx], sem.at[0, slot])
        return pltpu.make_async_copy(kpe_hbm.at[page], kpe_buf.at[slot, idx], sem.at[1, slot])

    def start_group(it, slot):
        bb = it // G
        g0 = (it % G) * PP
        for c in range(NC):
            for p in range(PP):
                page = pt_ref[c * per_chain + bb, g0 + p]
                page_copy(page, slot, c * PP + p, 0).start(priority=0)
                page_copy(page, slot, c * PP + p, 1).start(priority=1)

    def wait_group(slot):
        for idx in range(NC * PP):
            page_copy(0, slot, idx, 0).wait()
            page_copy(0, slot, idx, 1).wait()

    it = pl.program_id(0)
    slot = it % 2

    @pl.when(it == 0)
    def _():
        start_group(0, 0)

    @pl.when(it + 1 < total)
    def _():
        start_group(it + 1, 1 - slot)

    wait_group(slot)
    _decode_group(it // G, it % G, G, per_chain, slot, qlat_ref, qpe_ref, cnew_ref, pnew_ref, o_ref,
                  ckv_buf, kpe_buf, m_ref, l_ref, acc_ref, PP, NC)


def _decode_group(bb, g, G, per_chain, slot, qlat_ref, qpe_ref, cnew_ref, pnew_ref, o_ref,
                  ckv_buf, kpe_buf, m_ref, l_ref, acc_ref, PP, NC):
    chains = range(NC)
    bs = [c * per_chain + bb for c in chains]

    @pl.when(g == 0)
    def _():
        m_ref[...] = jnp.full(m_ref.shape, -jnp.inf, f32)
        l_ref[...] = jnp.zeros(l_ref.shape, f32)
        acc_ref[...] = jnp.zeros(acc_ref.shape, f32)

    qlat = [qlat_ref[bs[c]] for c in chains]
    qpe = [qpe_ref[bs[c]] for c in chains]
    cks = [[ckv_buf[slot, c * PP + p].astype(bf16) for p in range(PP)] for c in chains]
    s = [jnp.concatenate(
        [_dot_nt(qlat[c], cks[c][p]) + _dot(qpe[c], kpe_buf[slot, c * PP + p].astype(bf16)) for p in range(PP)],
        axis=1) for c in chains]
    m_prev = [m_ref[c] for c in chains]
    m_new = [jnp.maximum(m_prev[c], jnp.max(s[c], axis=-1, keepdims=True)) for c in chains]
    p_all = [jnp.exp2(s[c] - m_new[c]) for c in chains]
    alpha = [jnp.exp2(m_prev[c] - m_new[c]) for c in chains]
    for c in chains:
        l_ref[c] = alpha[c] * l_ref[c] + jnp.sum(p_all[c], axis=-1, keepdims=True)
        m_ref[c] = m_new[c]
    for c in chains:
        pv = _dot(p_all[c][:, :PAGE_SIZE].astype(bf16), cks[c][0])
        for p in range(1, PP):
            pv = pv + _dot(p_all[c][:, p * PAGE_SIZE:(p + 1) * PAGE_SIZE].astype(bf16), cks[c][p])
        acc_ref[c] = alpha[c] * acc_ref[c] + pv

    @pl.when(g == G - 1)
    def _():
        for c in chains:
            cnew = cnew_ref[bs[c]].astype(bf16).astype(f32)
            pnew = pnew_ref[bs[c]].astype(bf16).astype(f32)
            s_self = (jnp.sum(qlat[c].astype(f32) * cnew, axis=-1, keepdims=True)
                      + jnp.sum(qpe[c].astype(f32) * pnew, axis=-1, keepdims=True))
            m_last = m_ref[c]
            m_fin = jnp.maximum(m_last, s_self)
            p_self = jnp.exp2(s_self - m_fin)
            a_fin = jnp.exp2(m_last - m_fin)
            l_fin = a_fin * l_ref[c] + p_self
            o_ref[bs[c]] = (a_fin * acc_ref[c] + p_self * cnew) / l_fin


def _decode_attention(page_table, qlat, qpe, ckv_new, kpe_new, pool_ckv, pool_kpe_t, pages_per_step=16):
    DB, n_pages = page_table.shape
    PP = math.gcd(n_pages, pages_per_step)
    NC = math.gcd(DB, DECODE_CHAINS)
    H = MLA_HEADS
    vmem = pl.BlockSpec(memory_space=pltpu.VMEM)
    hbm = pl.BlockSpec(memory_space=pl.ANY)
    grid_spec = pltpu.PrefetchScalarGridSpec(
        num_scalar_prefetch=1,
        grid=((DB // NC) * (n_pages // PP),),
        in_specs=[vmem, vmem, vmem, vmem, hbm, hbm],
        out_specs=vmem,
        scratch_shapes=[
            pltpu.VMEM((2, NC * PP, PAGE_SIZE, KV_LORA), pool_ckv.dtype),
            pltpu.VMEM((2, NC * PP, QK_ROPE, PAGE_SIZE), pool_kpe_t.dtype),
            pltpu.SemaphoreType.DMA((2, 2)),
            pltpu.VMEM((NC, H, 1), f32),
            pltpu.VMEM((NC, H, 1), f32),
            pltpu.VMEM((NC, H, KV_LORA), f32),
        ],
    )
    return pl.pallas_call(
        functools.partial(_decode_kernel, pages_per_step=PP, chains=NC),
        grid_spec=grid_spec,
        out_shape=jax.ShapeDtypeStruct((DB, H, KV_LORA), f32),
        compiler_params=_cparams(("arbitrary",)),
        name="mla_decode",
    )(page_table, qlat, qpe, ckv_new, kpe_new, pool_ckv, pool_kpe_t)


def _proj_ffn_kernel(x_ref, o_ref, wo_ref, gffn_ref, win_ref, wout_ref, gfin_ref, out_ref, *, final_norm):
    x1 = x_ref[...] + _dot(o_ref[...], wo_ref[...])
    h = _rms(x1, gffn_ref[...]).astype(bf16)
    acc = x1
    for c in range(FFN_HIDDEN // FFN_TILE):
        lo = c * FFN_TILE
        gate = _dot(h, win_ref[:, lo:lo + FFN_TILE])
        up = _dot(h, win_ref[:, FFN_HIDDEN + lo:FFN_HIDDEN + lo + FFN_TILE])
        act = (gate * jax.nn.sigmoid(gate) * up).astype(bf16)
        acc = acc + _dot(act, wout_ref[lo:lo + FFN_TILE, :])
    if final_norm:
        acc = _rms(acc, gfin_ref[...])
    out_ref[...] = acc


def _proj_ffn(x, o, wo, gffn, w_in, w_out, gfin, final_norm, tm):
    T = x.shape[0]
    row = lambda i: (i, 0)
    held = lambda arr: pl.BlockSpec(arr.shape, lambda i: (0, 0), pipeline_mode=pl.Buffered(1))
    return pl.pallas_call(
        functools.partial(_proj_ffn_kernel, final_norm=final_norm),
        grid=(T // tm,),
        in_specs=[
            pl.BlockSpec((tm, D_MODEL), row),
            pl.BlockSpec((tm, D_MODEL), row),
            held(wo), held(gffn), held(w_in), held(w_out), held(gfin),
        ],
        out_specs=pl.BlockSpec((tm, D_MODEL), row),
        out_shape=jax.ShapeDtypeStruct((T, D_MODEL), f32),
        compiler_params=_cparams(("parallel",)),
        name="proj_ffn",
    )(x, o, wo, gffn, w_in, w_out, gfin)


def _seg_sum(x, ones_blk):
    parts = [_dot(x[:, gidx * LANES:(gidx + 1) * LANES].astype(bf16), ones_blk)
             for gidx in range(x.shape[1] // LANES)]
    return jnp.concatenate(parts, axis=1)


def _rwkv_proj_kernel(x_ref, prev_ref, gmix_ref, mix_ref, wrkv_ref, w0_ref, w1_ref, w2_ref,
                      a0_ref, a1_ref, a2_ref, g1_ref, g2_ref, kk_ref, ka_ref, rk_ref, ones_ref,
                      r_out, k_out, v_out, kn_out, b_out, ld_out, g_out, bonus_out, h_out,
                      *, seq_mode, tiles_per_seq):
    tm = x_ref.shape[0]
    gm = gmix_ref[...]
    h = _rms(x_ref[...], gm)
    if seq_mode:
        hp_row = _rms(prev_ref[...], gm)[7:8, :]
        is_start = (pl.program_id(0) % tiles_per_seq) == 0
        hp_row = jnp.where(is_start, jnp.zeros_like(hp_row), hp_row)
        rowid = lax.broadcasted_iota(jnp.int32, (tm, 1), 0)
        hprev = jnp.where(rowid == 0, hp_row, pltpu.roll(h, 1, 0))
        h_out[0] = h[tm - 8:, :]
    else:
        hprev = prev_ref[...]
        h_out[...] = h
    xx = hprev - h
    mix = mix_ref[...]
    xr, xw, xk, xv, xa, xg = ((h + xx * mix[n:n + 1]).astype(bf16) for n in range(6))
    w_hid = jnp.tanh(_dot(xw, w1_ref[...])).astype(bf16)
    a_hid = _dot(xa, a1_ref[...]).astype(bf16)
    g_hid = jax.nn.sigmoid(_dot(xg, g1_ref[...])).astype(bf16)
    ones_blk = ones_ref[...]
    ct = 2 * LANES
    for j in range(D_MODEL // ct):
        cols = slice(j * ct, (j + 1) * ct)
        r = _dot(xr, wrkv_ref[0, :, cols])
        k = _dot(xk, wrkv_ref[1, :, cols])
        v = _dot(xv, wrkv_ref[2, :, cols])
        wl = w0_ref[:, cols] + _dot(w_hid, w2_ref[:, cols])
        ld_out[:, cols] = -DECAY_SCALE * jax.nn.sigmoid(wl)
        a = jax.nn.sigmoid(a0_ref[:, cols] + _dot(a_hid, a2_ref[:, cols]))
        g_out[:, cols] = _dot(g_hid, g2_ref[:, cols]).astype(g_out.dtype)
        kk = k * kk_ref[:, cols]
        kk = kk * jnp.minimum(lax.rsqrt(_seg_sum(kk * kk, ones_blk)), 1e12)
        k = k * (1.0 + (a - 1.0) * ka_ref[:, cols])
        bonus_out[:, cols] = (_seg_sum(r * k * rk_ref[:, cols], ones_blk) * v).astype(bonus_out.dtype)
        r_out[:, cols] = r.astype(r_out.dtype)
        k_out[:, cols] = k.astype(k_out.dtype)
        v_out[:, cols] = v.astype(v_out.dtype)
        kn_out[:, cols] = kk.astype(kn_out.dtype)
        b_out[:, cols] = (kk * a).astype(b_out.dtype)


def _rwkv_proj(x, prev, p, seq_len, tm, vec_dtype):
    T = x.shape[0]
    seq_mode = seq_len > 1
    tiles_per_seq = max(seq_len // tm, 1)
    row = lambda i: (i, 0)
    const = lambda i: (0, 0)
    const3 = lambda i: (0, 0, 0)
    if seq_mode:
        prev_spec = pl.BlockSpec((8, D_MODEL), lambda i: (jnp.maximum(i * (tm // 8) - 1, 0), 0))
        h_spec = pl.BlockSpec((1, 8, D_MODEL), lambda i: (i, 0, 0))
        h_shape = jax.ShapeDtypeStruct((T // tm, 8, D_MODEL), f32)
    else:
        prev_spec = pl.BlockSpec((tm, D_MODEL), row)
        h_spec = pl.BlockSpec((tm, D_MODEL), row)
        h_shape = jax.ShapeDtypeStruct((T, D_MODEL), f32)
    vec = lambda dt: jax.ShapeDtypeStruct((T, D_MODEL), dt)
    tile = pl.BlockSpec((tm, D_MODEL), row)
    full = lambda arr: pl.BlockSpec(arr.shape, const3 if arr.ndim == 3 else const)
    weights = [p["gmix"], p["mix"], p["w_rkv"], p["w0"], p["w1"], p["w2"], p["a0"], p["a1"], p["a2"],
               p["g1"], p["g2"], p["k_k"], p["k_a"], p["r_k"], p["ones_blk"]]
    return pl.pallas_call(
        functools.partial(_rwkv_proj_kernel, seq_mode=seq_mode, tiles_per_seq=tiles_per_seq),
        grid=(T // tm,),
        in_specs=[tile, prev_spec] + [full(w) for w in weights],
        out_specs=[tile] * 8 + [h_spec],
        out_shape=[vec(vec_dtype)] * 5 + [vec(f32), vec(bf16), vec(bf16), h_shape],
        compiler_params=_cparams(("parallel",)),
        name="rwkv_proj",
    )(x, prev, *weights)


def _wkv_chunk_kernel(r_ref, k_ref, v_ref, kn_ref, b_ref, ld_ref, y_ref, st_ref, m_ref, *, chunks, wave):
    C = WKV_CHUNK
    G = WKV_GROUP
    W = GROUP_W
    c_idx = pl.program_id(1)

    @pl.when(c_idx == 0)
    def _():
        m_ref[...] = jnp.zeros(m_ref.shape, f32)

    rr = lax.broadcasted_iota(jnp.int32, (W, W), 0)
    cc = lax.broadcasted_iota(jnp.int32, (W, W), 1)
    bd_mask = (rr // RWKV_HEAD) == (cc // RWKV_HEAD)
    t_idx = lax.broadcasted_iota(jnp.int32, (C, W), 0)
    s_idx = lax.broadcasted_iota(jnp.int32, (C, W), 1) % C
    strict = s_idx < t_idx
    incl = s_idx <= t_idx
    eye_cat = (s_idx == t_idx).astype(f32)
    lane_lo = lax.broadcasted_iota(jnp.int32, (C, LANES), 1) < RWKV_HEAD
    zero_tile = jnp.zeros((C, LANES), bf16)

    def bd(y):
        yb = y.astype(bf16)
        rows = []
        for hh in range(G):
            lt = hh // 2
            t = yb[:, lt * LANES:(lt + 1) * LANES]
            keep = jnp.where(lane_lo, t, zero_tile) if hh % 2 == 0 else jnp.where(lane_lo, zero_tile, t)
            tiles = [zero_tile] * (W // LANES)
            tiles[lt] = keep
            rows.append(jnp.concatenate(tiles, axis=1))
        return jnp.concatenate(rows, axis=0)

    def bdmm(x, y):
        return _dot(x.astype(bf16), bd(y))

    def cumsum_rows(x):
        sh = 1
        while sh < C:
            x = x + jnp.where(t_idx >= sh, pltpu.roll(x, sh, 0), 0.0)
            sh *= 2
        return x

    NG = r_ref.shape[1] // W
    n_waves = chunks // wave

    def precompute(wv, out):
        streams = [(wv * wave + ci, gi) for ci in range(wave) for gi in range(NG)]
        ns = range(len(streams))

        def load(ref, s):
            ci, gi = streams[s]
            return ref[ci * C:(ci + 1) * C, gi * W:(gi + 1) * W]

        ld = [load(ld_ref, s) for s in ns]
        cs = [cumsum_rows(ld[s]) for s in ns]
        tot_col = [jnp.broadcast_to(cs[s][C - 1:C, :], (LANES, W)).T for s in ns]
        yield
        r = [load(r_ref, s).astype(f32) for s in ns]
        k = [load(k_ref, s).astype(f32) for s in ns]
        v = [load(v_ref, s).astype(f32) for s in ns]
        kn = [load(kn_ref, s).astype(f32) for s in ns]
        b = [load(b_ref, s).astype(f32) for s in ns]
        p_inv = [jnp.exp(-cs[s]) for s in ns]
        p_rest = [jnp.exp(cs[s][C - 1:C, :] - cs[s]) for s in ns]
        a_t = [-kn[s] * jnp.exp(cs[s] - ld[s]) for s in ns]
        r_t = [r[s] * jnp.exp(cs[s]) for s in ns]
        lhs = [jnp.concatenate([a_t[s], r_t[s]], axis=0).astype(bf16) for s in ns]
        yield
        s_b = [_dot_nt(lhs[s], bd(b[s] * p_inv[s])) for s in ns]
        yield
        s_k = [_dot_nt(lhs[s], bd(k[s] * p_inv[s])) for s in ns]
        yield
        l_ab = [jnp.where(strict, s_b[s][:C], 0.0) for s in ns]
        l_ak = [jnp.where(strict, s_k[s][:C], 0.0) for s in ns]
        a_rb = [jnp.where(incl, s_b[s][C:], 0.0) for s in ns]
        a_rk = [jnp.where(incl, s_k[s][C:], 0.0) for s in ns]
        t_inv = [eye_cat + l_ab[s] for s in ns]
        pw = [bdmm(l_ab[s], l_ab[s]) for s in ns]
        yield
        step = 2
        while step < C:
            rhs = [bd(pw[s]) for s in ns]
            if 2 * step < C:
                both = [_dot(jnp.concatenate([pw[s], t_inv[s]], axis=0).astype(bf16), rhs[s]) for s in ns]
                pw = [both[s][:C] for s in ns]
                t_inv = [t_inv[s] + both[s][C:] for s in ns]
            else:
                t_inv = [t_inv[s] + _dot(t_inv[s].astype(bf16), rhs[s]) for s in ns]
            step *= 2
            yield
        from_v = [bdmm(jnp.concatenate([l_ak[s], a_rk[s]], axis=0), v[s]) for s in ns]
        yield
        w1 = [bdmm(t_inv[s], a_t[s]) for s in ns]
        yield
        w2 = [bdmm(t_inv[s], from_v[s][:C]) for s in ns]
        out["st_lhs"] = [jnp.concatenate([w1[s], r_t[s]], axis=0).astype(bf16) for s in ns]
        out["w2"] = w2
        out["y_v"] = [from_v[s][C:] for s in ns]
        out["a_rb"] = a_rb
        out["v"] = v
        out["bk_rest"] = [jnp.concatenate([b[s] * p_rest[s], k[s] * p_rest[s]], axis=0).astype(bf16) for s in ns]
        out["decay_col"] = [jnp.exp(jnp.concatenate([tot_col[s]] * (W // LANES), axis=1)) for s in ns]
        yield

    def state_chain(wv, d, m):
        for ci in range(wave):
            c_glob = wv * wave + ci
            ss = [ci * NG + gi for gi in range(NG)]
            from_state = [_dot(d["st_lhs"][s], m[gi].astype(bf16)) for gi, s in enumerate(ss)]
            yield
            u = [from_state[gi][:C] + d["w2"][s] for gi, s in enumerate(ss)]
            y_u = [bdmm(d["a_rb"][s], u[gi]) for gi, s in enumerate(ss)]
            upd = [_dot_tn(d["bk_rest"][s], jnp.concatenate([u[gi], d["v"][s]], axis=0).astype(bf16))
                   for gi, s in enumerate(ss)]
            yield
            for gi, s in enumerate(ss):
                y_ref[c_glob * C:(c_glob + 1) * C, gi * W:(gi + 1) * W] = from_state[gi][C:] + y_u[gi] + d["y_v"][s]
                m[gi] = m[gi] * d["decay_col"][s] + jnp.where(bd_mask, upd[gi], 0.0)
            yield

    front_stages = 5
    m = [m_ref[gi] for gi in range(NG)]
    data = [{} for _ in range(n_waves)]
    pres = [precompute(wv, data[wv]) for wv in range(n_waves)]
    for t in range(n_waves + 2):
        lanes = []
        if 0 <= t - 1 < n_waves:
            lanes.append(pres[t - 1])
        if t < n_waves:
            lanes.append(itertools.islice(pres[t], front_stages))
        if 0 <= t - 2 < n_waves:
            lanes.append(state_chain(t - 2, data[t - 2], m))
        while lanes:
            lanes = [g for g in lanes if next(g, _DONE) is not _DONE]
    for gi in range(NG):
        m_ref[gi] = m[gi]
        acc = m[gi][:RWKV_HEAD]
        for hh in range(1, G):
            acc = acc + m[gi][hh * RWKV_HEAD:(hh + 1) * RWKV_HEAD]
        st_ref[0, :, gi * W:(gi + 1) * W] = acc


def _wkv_chunked(r, k, v, kn, b, ld, B, S):
    T = B * S
    chunks = WKV_CHUNKS_PER_STEP if S % (WKV_CHUNK * WKV_CHUNKS_PER_STEP) == 0 else 1
    wave = WKV_WAVE if chunks % WKV_WAVE == 0 else chunks
    rows = WKV_CHUNK * chunks
    nc = S // rows
    tile = pl.BlockSpec((rows, D_MODEL), lambda bi, ci: (bi * nc + ci, 0))
    return pl.pallas_call(
        functools.partial(_wkv_chunk_kernel, chunks=chunks, wave=wave),
        grid=(B, nc),
        in_specs=[tile] * 6,
        out_specs=[tile, pl.BlockSpec((1, RWKV_HEAD, D_MODEL), lambda bi, ci: (bi, 0, 0))],
        out_shape=[jax.ShapeDtypeStruct((T, D_MODEL), f32),
                   jax.ShapeDtypeStruct((B, RWKV_HEAD, D_MODEL), f32)],
        scratch_shapes=[pltpu.VMEM((D_MODEL // GROUP_W, GROUP_W, GROUP_W), f32)],
        compiler_params=_cparams(("parallel", "arbitrary")),
        name="wkv_chunked",
    )(r, k, v, kn, b, ld)


def _wkv_step_kernel(s_ref, vec_ref, snew_ref, y_ref):
    r, k, kn, b, ld = (vec_ref[i, 0] for i in (0, 1, 3, 4, 5))
    a = -kn
    w = jnp.exp(ld)

    def value_row(vi, c):
        st = s_ref[0, vi]
        sa = jnp.sum(st * a, axis=0, keepdims=True)
        v_row = vec_ref[2, 0, pl.ds(vi, 1), :]
        st = st * w + sa * b + v_row * k
        snew_ref[0, vi] = st
        y_ref[0, pl.ds(vi, 1), :] = jnp.sum(st * r, axis=0, keepdims=True)
        return c

    lax.fori_loop(0, RWKV_HEAD, value_row, 0, unroll=4)


def _wkv_step(state_t, vecs):
    H, N, _, DB = state_t.shape
    st = pl.BlockSpec((1, N, N, DB), lambda h: (h, 0, 0, 0))
    return pl.pallas_call(
        _wkv_step_kernel,
        grid=(H,),
        in_specs=[st, pl.BlockSpec((6, 1, N, DB), lambda h: (0, h, 0, 0))],
        out_specs=[st, pl.BlockSpec((1, N, DB), lambda h: (h, 0, 0))],
        out_shape=[jax.ShapeDtypeStruct(state_t.shape, f32),
                   jax.ShapeDtypeStruct((H, N, DB), f32)],
        compiler_params=_cparams(("parallel",)),
        name="wkv_step",
    )(state_t, vecs)


def _rwkv_post_kernel(y_ref, bonus_ref, g_ref, lnw_ref, lnb_ref, ones_ref, o_ref):
    ones_blk = ones_ref[...]
    y = y_ref[...]
    inv_n = 1.0 / RWKV_HEAD
    mu = _seg_sum(y, ones_blk) * inv_n
    d = y - mu
    var = _seg_sum(d * d, ones_blk) * inv_n
    yn = d * lax.rsqrt(var + GN_EPS) * lnw_ref[...] + lnb_ref[...]
    o_ref[...] = ((yn + bonus_ref[...].astype(f32)) * g_ref[...].astype(f32)).astype(bf16)


def _rwkv_post(y, bonus, g, lnw, lnb, ones_blk, tm):
    T = y.shape[0]
    row = lambda i: (i, 0)
    const = lambda i: (0, 0)
    tile = pl.BlockSpec((tm, D_MODEL), row)
    return pl.pallas_call(
        _rwkv_post_kernel,
        grid=(T // tm,),
        in_specs=[tile, tile, tile, pl.BlockSpec((1, D_MODEL), const), pl.BlockSpec((1, D_MODEL), const),
                  pl.BlockSpec((LANES, LANES), const)],
        out_specs=tile,
        out_shape=jax.ShapeDtypeStruct((T, D_MODEL), bf16),
        compiler_params=_cparams(("parallel",)),
        name="rwkv_post",
    )(y, bonus, g, lnw, lnb, ones_blk)


def _rope_tables(pos):
    half = QK_ROPE // 2
    inv = 1.0 / (ROPE_THETA ** (jnp.arange(0, QK_ROPE, 2, dtype=f32) / QK_ROPE))
    ang = pos[:, None] * inv[None, :]
    cos, sin = jnp.cos(ang), jnp.sin(ang)
    n = pos.shape[0]
    ones = jnp.ones((n, QK_NOPE), f32)
    z = lambda w: jnp.zeros((n, w), f32)
    tab_c = jnp.concatenate([ones, cos, cos, z(HEAD_TILE - QK_NOPE - QK_ROPE)], axis=1)
    tab_s1 = jnp.concatenate([z(QK_NOPE), -sin, z(HEAD_TILE - QK_NOPE - half)], axis=1)
    tab_s2 = jnp.concatenate([z(QK_NOPE + half), sin, z(HEAD_TILE - QK_NOPE - QK_ROPE)], axis=1)
    return tab_c, tab_s1, tab_s2


def _pick_tile(n, pref):
    t = min(n, pref)
    while n % t:
        t //= 2
    return t


def kernel(x_prompt, x_sample, cache_ckv, cache_kpe, state_wkv, state_shift, page_table,
           norm_mix, norm_ffn, norm_final,
           mla_w_a, mla_q_norm, mla_kv_norm, mla_w_uq, mla_w_ukv, mla_w_o,
           rw_mix, rw_w_rkv, rw_w0, rw_w1, rw_w2, rw_a0, rw_a1, rw_a2, rw_g1, rw_g2,
           rw_k_k, rw_k_a, rw_r_k, rw_ln_w, rw_ln_b, rw_w_o,
           ffn_w_in, ffn_w_out):
    B, S, D = x_prompt.shape
    DB, DS, _ = x_sample.shape
    assert D == D_MODEL and DS == 1 and S % WKV_CHUNK == 0
    n_pages = page_table.shape[1]
    past_len = n_pages * PAGE_SIZE
    H = MLA_HEADS
    row = lambda t: t.reshape(1, -1)
    xp = x_prompt.reshape(B * S, D)
    xs = x_sample.reshape(DB, D)

    w_a = mla_w_a[0]
    pad_pe = jnp.zeros((D, HEAD_TILE), f32).at[:, QK_NOPE:QK_NOPE + QK_ROPE].set(w_a[:, Q_LORA + KV_LORA:])
    wa_ext = jnp.concatenate([w_a[:, :Q_LORA + KV_LORA], pad_pe], axis=1).astype(bf16)
    wuq = (mla_w_uq[0] * (MLA_SCALE * LOG2_E)).reshape(Q_LORA, H, QK_NOPE + QK_ROPE)
    wuq = jnp.pad(wuq, ((0, 0), (0, 0), (0, HEAD_TILE - QK_NOPE - QK_ROPE))).reshape(Q_LORA, H * HEAD_TILE).astype(bf16)
    wukv = mla_w_ukv[0].reshape(KV_LORA, H * HEAD_TILE).astype(bf16)
    mla_wo = mla_w_o[0].astype(bf16)
    qn, kvn = row(mla_q_norm[0]), row(mla_kv_norm[0])
    ffn_in = [ffn_w_in[i].astype(bf16) for i in range(ffn_w_in.shape[0])]
    ffn_out = [ffn_w_out[i].astype(bf16) for i in range(ffn_w_out.shape[0])]
    hh = jnp.arange(LANES) // RWKV_HEAD
    ones_blk = (hh[:, None] == hh[None, :]).astype(bf16)
    rwp = dict(
        gmix=row(norm_mix[1]), mix=jnp.pad(rw_mix[0], ((0, 2), (0, 0))), w_rkv=rw_w_rkv[0].astype(bf16),
        w0=row(rw_w0[0]), w1=rw_w1[0].astype(bf16), w2=rw_w2[0].astype(bf16),
        a0=row(rw_a0[0]), a1=rw_a1[0].astype(bf16), a2=rw_a2[0].astype(bf16),
        g1=rw_g1[0].astype(bf16), g2=rw_g2[0].astype(bf16),
        k_k=row(rw_k_k[0]), k_a=row(rw_k_a[0]), r_k=row(rw_r_k[0]), ones_blk=ones_blk)
    rw_wo = rw_w_o[0].astype(bf16)
    lnw, lnb = row(rw_ln_w[0]), row(rw_ln_b[0])
    gfin = row(norm_final)

    tm_p = _pick_tile(B * S, 512)
    tm_s = _pick_tile(DB, 128)

    tabs_p = _rope_tables(jnp.arange(S, dtype=f32))
    tabs_s = _rope_tables(jnp.full((tm_s,), past_len, f32))
    q_s, _, _, ckv_s, kpe_s = _mla_proj(xs, row(norm_mix[0]), wa_ext, qn, kvn, wuq, wukv, tabs_s, tm_s)
    qlat = jnp.swapaxes(_absorb_q(q_s, wukv), 0, 1)
    qpe = q_s.reshape(DB, H, HEAD_TILE)[:, :, QK_NOPE:QK_NOPE + QK_ROPE]
    o_lat = _decode_attention(page_table, qlat, qpe, ckv_s.reshape(DB, 1, KV_LORA), kpe_s.reshape(DB, 1, QK_ROPE),
                              cache_ckv.reshape(-1, PAGE_SIZE, KV_LORA),
                              jnp.swapaxes(cache_kpe.reshape(-1, PAGE_SIZE, QK_ROPE), 1, 2))
    o_s = _unabsorb_o(jnp.swapaxes(o_lat, 0, 1), wukv)
    o_s = o_s.reshape(DB, H, HEAD_TILE)[:, :, QK_NOPE:].reshape(DB, H * V_HEAD).astype(bf16)
    xs = _proj_ffn(xs, o_s, mla_wo, row(norm_ffn[0]), ffn_in[0], ffn_out[0], gfin, False, tm_s)

    q_p, k_p, kv_p, ckv_p, kpe_p = _mla_proj(xp, row(norm_mix[0]), wa_ext, qn, kvn, wuq, wukv, tabs_p,
                                             _pick_tile(S, 512))
    o_p = _flash_attention(q_p, k_p, kv_p, B, S)
    xp = _proj_ffn(xp, o_p, mla_wo, row(norm_ffn[0]), ffn_in[0], ffn_out[0], gfin, False, tm_p)

    tm_r = _pick_tile(S, 512)
    r, k, v, kn, b, ld, g, bonus, hl = _rwkv_proj(xp, xp, rwp, S, tm_r, bf16)
    shift_p = hl.reshape(B, S // tm_r, 8, D)[:, -1, 7, :]
    y, st = _wkv_chunked(r, k, v, kn, b, ld, B, S)
    wkv_p = jnp.transpose(st.reshape(B, RWKV_HEAD, RWKV_HEADS, RWKV_HEAD), (0, 2, 3, 1))
    yo = _rwkv_post(y, bonus, g, lnw, lnb, ones_blk, _pick_tile(B * S, 1024))
    y_prompt = _proj_ffn(xp, yo, rw_wo, row(norm_ffn[1]), ffn_in[1], ffn_out[1], gfin, True, tm_p)

    r, k, v, kn, b, ld, g, bonus, shift_s = _rwkv_proj(xs, state_shift[0], rwp, 1, tm_s, f32)
    vecs = jnp.transpose(jnp.stack([r, k, v, kn, b, ld]), (0, 2, 1)).reshape(6, RWKV_HEADS, RWKV_HEAD, DB)
    wkv_t, y_t = _wkv_step(jnp.transpose(state_wkv[0].astype(f32), (1, 2, 3, 0)), vecs)
    wkv_s = jnp.transpose(wkv_t, (3, 0, 1, 2)).astype(state_wkv.dtype)
    y = jnp.transpose(y_t.reshape(D, DB))
    yo = _rwkv_post(y, bonus, g, lnw, lnb, ones_blk, tm_s)
    y_sample = _proj_ffn(xs, yo, rw_wo, row(norm_ffn[1]), ffn_in[1], ffn_out[1], gfin, True, tm_s)

    return (y_prompt.reshape(B, S, D), y_sample.reshape(DB, DS, D),
            ckv_p.reshape(1, B, S, KV_LORA), kpe_p.reshape(1, B, S, QK_ROPE),
            ckv_s.reshape(1, DB, DS, KV_LORA), kpe_s.reshape(1, DB, DS, QK_ROPE),
            wkv_p[None].astype(x_prompt.dtype), shift_p[None],
            wkv_s[None], shift_s[None])
```
